```python
import jax, jax.numpy as jnp
from jax import lax
import numpy as np

D_MODEL = 4096
BATCH = 1
SEQ = 8192
DEPTH = 1
DEC_BATCH = 128
DEC_SEQ = 8
PAST_LEN = 2048
PAGE_SIZE = 128

HEAD_DIM = 128
MIX_WIDTH = D_MODEL
N_HEADS = MIX_WIDTH // HEAD_DIM
FOX_HEADS = N_HEADS // 2
FOX_KV_HEADS = max(1, FOX_HEADS // 4)
FOX_REP = FOX_HEADS // FOX_KV_HEADS
NSA_HEADS = N_HEADS - FOX_HEADS
NSA_GROUPS = max(1, NSA_HEADS // 4)
NSA_REP = NSA_HEADS // NSA_GROUPS
BLOCK = 64
SEL_TOP_N = 16
WINDOW = 512
CMP_HIDDEN = 2 * HEAD_DIM
Q_BLOCK = 128
PLE_DIM = 256
D_FF = -(-(8 * D_MODEL) // (3 * 256)) * 256
ROPE_THETA = 10000.0
RMS_EPS = 1e-6
ATTN_SCALE = HEAD_DIM ** -0.5
NEG_INF = -1e30
FORCE_BONUS = 1e4
FGATE_BIAS = 2.0
FOX_Q_W = FOX_HEADS * HEAD_DIM
FOX_KV_W = FOX_KV_HEADS * HEAD_DIM
NSA_Q_W = NSA_HEADS * HEAD_DIM
NSA_KV_W = NSA_GROUPS * HEAD_DIM
N_IN = FOX_Q_W + 2 * FOX_KV_W + FOX_HEADS + NSA_Q_W + 6 * NSA_KV_W + 3 * NSA_HEADS

kernel_name = 'hybrid_fox_nsa_decoder_step'


def rmsnorm(x, g):
    xf = x.astype(jnp.float32)
    y = xf * lax.rsqrt(jnp.mean(xf * xf, axis=-1, keepdims=True) + RMS_EPS)
    return (y * g.astype(jnp.float32)).astype(x.dtype)


def rope(x, pos):
    half = HEAD_DIM // 2
    inv_freq = ROPE_THETA ** (-jnp.arange(half, dtype=jnp.float32) / half)
    ang = pos.astype(jnp.float32)[:, None] * inv_freq[None, :]
    cos, sin = jnp.cos(ang)[:, None, :], jnp.sin(ang)[:, None, :]
    x1, x2 = x[..., :half].astype(jnp.float32), x[..., half:].astype(jnp.float32)
    return jnp.concatenate([x1 * cos - x2 * sin, x2 * cos + x1 * sin], axis=-1).astype(x.dtype)


def masked_softmax(logits, mask):
    return jax.nn.softmax(jnp.where(mask, logits, NEG_INF), axis=-1)


def project_inputs(xn, pos, w_in, b_fgate):
    B, T, _ = xn.shape
    sizes = [FOX_Q_W, FOX_KV_W, FOX_KV_W, FOX_HEADS, NSA_Q_W] + [NSA_KV_W] * 6 + [3 * NSA_HEADS]
    cuts = [int(c) for c in np.cumsum(sizes)[:-1]]
    fq, fk, fv, fl, nq, kc, vc, ks, vs, kw, vw, g = jnp.split(xn @ w_in, cuts, axis=-1)
    hd = lambda t: t.reshape(B, T, -1, HEAD_DIM)
    logf = jax.nn.log_sigmoid((fl + b_fgate).astype(jnp.float32))
    gates = jax.nn.sigmoid(g.reshape(B, T, NSA_HEADS, 3))
    return (hd(fq), hd(fk), hd(fv), logf,
            rope(hd(nq), pos), rope(hd(kc), pos), hd(vc), rope(hd(ks), pos), hd(vs),
            rope(hd(kw), pos), hd(vw), gates)


def fox_attend(q, f_q, q_pos, k, v, f_k, k_pos):
    B, Tq = q.shape[:2]
    Tk = k.shape[1]
    qg = q.reshape(B, Tq, FOX_KV_HEADS, FOX_REP, HEAD_DIM)
    s = jnp.einsum('bqgrd,bkgd->bgrqk', qg, k, preferred_element_type=jnp.float32) * ATTN_SCALE
    fq_ = f_q.reshape(B, Tq, FOX_KV_HEADS, FOX_REP).transpose(0, 2, 3, 1)[..., None]
    fk_ = f_k.reshape(B, Tk, FOX_KV_HEADS, FOX_REP).transpose(0, 2, 3, 1)[..., None, :]
    mask = k_pos[None, :] <= q_pos[:, None]
    p = masked_softmax(s + fq_ - fk_, mask)
    o = jnp.einsum('bgrqk,bkgd->bqgrd', p.astype(v.dtype), v)
    return o.reshape(B, Tq, FOX_Q_W)


def pad_to_block(rows):
    pad = (-rows.shape[1]) % BLOCK
    return jnp.pad(rows, ((0, 0), (0, pad), (0, 0), (0, 0)))


def to_blocks(rows):
    r = pad_to_block(rows)
    B, T = r.shape[:2]
    return r.reshape(B, T // BLOCK, BLOCK, NSA_GROUPS, HEAD_DIM).transpose(0, 3, 1, 2, 4)


def compress(rows, pe, w1, w2):
    r = pad_to_block(rows)
    B, T = r.shape[:2]
    nb = T // BLOCK
    blk = r.reshape(B, nb, BLOCK, NSA_GROUPS, HEAD_DIM) + pe[None, None, :, None, :]
    flat = blk.transpose(0, 1, 3, 2, 4).reshape(B, nb, NSA_GROUPS, BLOCK * HEAD_DIM)
    return jax.nn.silu(flat @ w1) @ w2


def nsa_attend(q, gates, q_pos, k_cmp, v_cmp, ks_blk, vs_blk, kw, vw, w_pos):
    B, Tq = q.shape[:2]
    nb = k_cmp.shape[1]
    qg = q.reshape(B, Tq, NSA_GROUPS, NSA_REP, HEAD_DIM)
    blk = jnp.arange(nb)
    s_c = jnp.einsum('bqgrd,bngd->bgrqn', qg, k_cmp, preferred_element_type=jnp.float32) * ATTN_SCALE
    c_mask = (blk[None, :] + 1) * BLOCK - 1 <= q_pos[:, None]
    p_c = masked_softmax(s_c, c_mask) * jnp.any(c_mask, axis=-1)[:, None].astype(jnp.float32)
    o_cmp = jnp.einsum('bgrqn,bngd->bqgrd', p_c.astype(v_cmp.dtype), v_cmp)
    importance = p_c.sum(axis=2)
    cur = q_pos // BLOCK
    forced = (blk[None, :] == 0) | (blk[None, :] == cur[:, None]) | (blk[None, :] == cur[:, None] - 1)
    avail = blk[None, :] * BLOCK <= q_pos[:, None]
    score = jnp.where(avail, importance + jnp.where(forced, FORCE_BONUS, 0.0), NEG_INF)
    n_sel = min(SEL_TOP_N, nb)
    _, idx = lax.top_k(score, n_sel)
    bi = jnp.arange(B)[:, None, None, None]
    gi = jnp.arange(NSA_GROUPS)[None, :, None, None]
    k_sel = ks_blk[bi, gi, idx]
    v_sel = vs_blk[bi, gi, idx]
    s_pos = idx[..., None] * BLOCK + jnp.arange(BLOCK)
    s_mask = (s_pos <= q_pos[None, None, :, None, None]).reshape(B, NSA_GROUPS, 1, Tq, n_sel * BLOCK)
    s_s = jnp.einsum('bqgrd,bgqnld->bgrqnl', qg, k_sel, preferred_element_type=jnp.float32) * ATTN_SCALE
    p_s = masked_softmax(s_s.reshape(B, NSA_GROUPS, NSA_REP, Tq, n_sel * BLOCK), s_mask).reshape(s_s.shape)
    o_slc = jnp.einsum('bgrqnl,bgqnld->bqgrd', p_s.astype(v_sel.dtype), v_sel)
    s_w = jnp.einsum('bqgrd,bkgd->bgrqk', qg, kw, preferred_element_type=jnp.float32) * ATTN_SCALE
    dist = q_pos[:, None] - w_pos[None, :]
    w_mask = (dist >= 0) & (dist < WINDOW) & (w_pos[None, :] >= 0)
    p_w = masked_softmax(s_w, w_mask)
    o_win = jnp.einsum('bgrqk,bkgd->bqgrd', p_w.astype(vw.dtype), vw)
    g = gates.reshape(B, Tq, NSA_GROUPS, NSA_REP, 3)
    o = g[..., 0:1] * o_cmp + g[..., 1:2] * o_slc + g[..., 2:3] * o_win
    return o.reshape(B, Tq, NSA_Q_W)


def prompt_mixer(xn, w_in, b_fgate, cmp):
    pe_k, w1_k, w2_k, pe_v, w1_v, w2_v = cmp
    B, T, _ = xn.shape
    pos = jnp.arange(T)
    fq, fk, fv, logf, nq, kc, vc, ks, vs, kw, vw, gates = project_inputs(xn, pos, w_in, b_fgate)
    f_cum = jnp.cumsum(logf, axis=1)
    k_cmp = compress(kc, pe_k, w1_k, w2_k)
    v_cmp = compress(vc, pe_v, w1_v, w2_v)
    ks_blk, vs_blk = to_blocks(ks), to_blocks(vs)
    pad_w = ((0, 0), (WINDOW, 0), (0, 0), (0, 0))
    kw_pad, vw_pad = jnp.pad(kw, pad_w), jnp.pad(vw, pad_w)

    def sweep(c):
        start = c * Q_BLOCK
        q_pos = start + jnp.arange(Q_BLOCK)
        sl = lambda t: lax.dynamic_slice_in_dim(t, start, Q_BLOCK, axis=1)
        o_fox = fox_attend(sl(fq), sl(f_cum), q_pos, fk, fv, f_cum, pos)
        wk = lax.dynamic_slice_in_dim(kw_pad, start, WINDOW + Q_BLOCK, axis=1)
        wv = lax.dynamic_slice_in_dim(vw_pad, start, WINDOW + Q_BLOCK, axis=1)
        w_pos = start - WINDOW + jnp.arange(WINDOW + Q_BLOCK)
        o_nsa = nsa_attend(sl(nq), sl(gates), q_pos, k_cmp, v_cmp, ks_blk, vs_blk, wk, wv, w_pos)
        return jnp.concatenate([o_fox, o_nsa], axis=-1)

    o = lax.map(sweep, jnp.arange(T // Q_BLOCK))
    o = o.transpose(1, 0, 2, 3).reshape(B, T, MIX_WIDTH)
    n_win = min(WINDOW, T)
    rows = (jnp.stack([fk, fv], axis=2), logf,
            jnp.stack([kc, vc, ks, vs], axis=2),
            jnp.stack([kw, vw], axis=2)[:, T - n_win:])
    return o, rows


def sample_mixer(xn, cache_fox_kv, cache_fox_logf, cache_nsa_kv, state_win, page_table, w_in, b_fgate, cmp):
    pe_k, w1_k, w2_k, pe_v, w1_v, w2_v = cmp
    B, T, _ = xn.shape
    P = page_table.shape[1] * PAGE_SIZE
    pos = P + jnp.arange(T)
    fq, fk, fv, logf, nq, kc, vc, ks, vs, kw, vw, gates = project_inputs(xn, pos, w_in, b_fgate)

    def one_seq(args):
        pt, fq_b, fk_b, fv_b, lf_b, nq_b, kc_b, vc_b, ks_b, vs_b, kw_b, vw_b, g_b, win_b = args
        past_fox = cache_fox_kv[pt].reshape((P,) + cache_fox_kv.shape[2:])
        past_lf = cache_fox_logf[pt].reshape((P,) + cache_fox_logf.shape[2:])
        past_nsa = cache_nsa_kv[pt].reshape((P,) + cache_nsa_kv.shape[2:])
        cat = lambda a, b: jnp.concatenate([a, b], axis=0)[None]
        f_all = jnp.cumsum(jnp.concatenate([past_lf.astype(jnp.float32), lf_b], axis=0), axis=0)[None]
        o_fox = fox_attend(fq_b[None], f_all[:, P:], pos, cat(past_fox[:, 0], fk_b), cat(past_fox[:, 1], fv_b),
                           f_all, jnp.arange(P + T))
        k_cmp = compress(cat(past_nsa[:, 0], kc_b), pe_k, w1_k, w2_k)
        v_cmp = compress(cat(past_nsa[:, 1], vc_b), pe_v, w1_v, w2_v)
        ks_blk = to_blocks(cat(past_nsa[:, 2], ks_b))
        vs_blk = to_blocks(cat(past_nsa[:, 3], vs_b))
        n_buf = win_b.shape[0]
        wk = jnp.concatenate([win_b[:, 0], kw_b], axis=0)
        wv = jnp.concatenate([win_b[:, 1], vw_b], axis=0)
        w_pos = P - n_buf + jnp.arange(n_buf + T)
        o_nsa = nsa_attend(nq_b[None], g_b[None], pos, k_cmp, v_cmp, ks_blk, vs_blk, wk[None], wv[None], w_pos)
        o = jnp.concatenate([o_fox, o_nsa], axis=-1)[0]
        return o, jnp.stack([wk, wv], axis=1)[T:]

    o, new_win = lax.map(one_seq, (page_table, fq, fk, fv, logf, nq, kc, vc, ks, vs, kw, vw, gates, state_win))
    rows = (jnp.stack([fk, fv], axis=2), logf, jnp.stack([kc, vc, ks, vs], axis=2), new_win)
    return o, rows


def layer_tail(h, o, p, w_out, g_ffn, w_gate, w_up, w_down, g_ple, w_ple_gate, w_ple_proj):
    h = h + o @ w_out
    xf = rmsnorm(h, g_ffn)
    h = h + (jax.nn.silu(xf @ w_gate) * (xf @ w_up)) @ w_down
    gate = jax.nn.sigmoid(rmsnorm(h, g_ple) @ w_ple_gate)
    return h + gate * (p @ w_ple_proj)


def setup_inputs(seed: int = 0) -> dict:
    key = jax.random.key(seed)
    ks = jax.random.split(key, 32)
    f32 = jnp.float32
    nrm = lambda k, shape, scale: jax.random.normal(k, shape, f32) * scale
    gain = lambda k, shape: 1.0 + nrm(k, shape, 0.02)
    n_pages = PAST_LEN // PAGE_SIZE
    n_used = DEC_BATCH * n_pages
    n_pool = n_used + max(1, n_used // 4)
    win_buf = min(WINDOW, PAST_LEN)
    page_table = jax.random.permutation(ks[0], n_pool)[:n_used].reshape(DEC_BATCH, n_pages).astype(jnp.int32)
    return {
        'x_prompt': nrm(ks[1], (BATCH, SEQ, D_MODEL), 1.0),
        'x_sample': nrm(ks[2], (DEC_BATCH, DEC_SEQ, D_MODEL), 1.0),
        'cache_fox_kv': nrm(ks[3], (DEPTH, n_pool, PAGE_SIZE, 2, FOX_KV_HEADS, HEAD_DIM), 1.0),
        'cache_fox_logf': jax.nn.log_sigmoid(FGATE_BIAS + nrm(ks[4], (DEPTH, n_pool, PAGE_SIZE, FOX_HEADS), 1.0)),
        'cache_nsa_kv': nrm(ks[5], (DEPTH, n_pool, PAGE_SIZE, 4, NSA_GROUPS, HEAD_DIM), 1.0),
        'state_nsa_win': nrm(ks[6], (DEPTH, DEC_BATCH, win_buf, 2, NSA_GROUPS, HEAD_DIM), 1.0),
        'page_table': page_table,
        'p_prompt': nrm(ks[7], (DEPTH, BATCH, SEQ, PLE_DIM), 1.0),
        'p_sample': nrm(ks[8], (DEPTH, DEC_BATCH, DEC_SEQ, PLE_DIM), 1.0),
        'g_mix': gain(ks[9], (DEPTH, D_MODEL)),
        'w_in': nrm(ks[10], (DEPTH, D_MODEL, N_IN), D_MODEL ** -0.5),
        'b_fgate': FGATE_BIAS + nrm(ks[11], (DEPTH, FOX_HEADS), 0.1),
        'cmp_pe_k': nrm(ks[12], (DEPTH, BLOCK, HEAD_DIM), 0.1),
        'cmp_w1_k': nrm(ks[13], (DEPTH, BLOCK * HEAD_DIM, CMP_HIDDEN), (BLOCK * HEAD_DIM) ** -0.5),
        'cmp_w2_k': nrm(ks[14], (DEPTH, CMP_HIDDEN, HEAD_DIM), CMP_HIDDEN ** -0.5),
        'cmp_pe_v': nrm(ks[15], (DEPTH, BLOCK, HEAD_DIM), 0.1),
        'cmp_w1_v': nrm(ks[16], (DEPTH, BLOCK * HEAD_DIM, CMP_HIDDEN), (BLOCK * HEAD_DIM) ** -0.5),
        'cmp_w2_v': nrm(ks[17], (DEPTH, CMP_HIDDEN, HEAD_DIM), CMP_HIDDEN ** -0.5),
        'w_out': nrm(ks[18], (DEPTH, MIX_WIDTH, D_MODEL), MIX_WIDTH ** -0.5),
        'g_ffn': gain(ks[19], (DEPTH, D_MODEL)),
        'w_gate': nrm(ks[20], (DEPTH, D_MODEL, D_FF), D_MODEL ** -0.5),
        'w_up': nrm(ks[21], (DEPTH, D_MODEL, D_FF), D_MODEL ** -0.5),
        'w_down': nrm(ks[22], (DEPTH, D_FF, D_MODEL), D_FF ** -0.5),
        'g_ple': gain(ks[23], (DEPTH, D_MODEL)),
        'w_ple_gate': nrm(ks[24], (DEPTH, D_MODEL, D_MODEL), D_MODEL ** -0.5),
        'w_ple_proj': nrm(ks[25], (DEPTH, PLE_DIM, D_MODEL), PLE_DIM ** -0.5),
        'g_final': gain(ks[26], (D_MODEL,)),
    }


def reference(x_prompt, x_sample, cache_fox_kv, cache_fox_logf, cache_nsa_kv, state_nsa_win, page_table,
              p_prompt, p_sample, g_mix, w_in, b_fgate, cmp_pe_k, cmp_w1_k, cmp_w2_k, cmp_pe_v, cmp_w1_v,
              cmp_w2_v, w_out, g_ffn, w_gate, w_up, w_down, g_ple, w_ple_gate, w_ple_proj, g_final):
    hp, hs = x_prompt, x_sample
    outs_p, outs_s = [], []
    for i in range(DEPTH):
        cmp = (cmp_pe_k[i], cmp_w1_k[i], cmp_w2_k[i], cmp_pe_v[i], cmp_w1_v[i], cmp_w2_v[i])
        tail = (w_out[i], g_ffn[i], w_gate[i], w_up[i], w_down[i], g_ple[i], w_ple_gate[i], w_ple_proj[i])
        o_p, rows_p = prompt_mixer(rmsnorm(hp, g_mix[i]), w_in[i], b_fgate[i], cmp)
        hp = layer_tail(hp, o_p, p_prompt[i], *tail)
        o_s, rows_s = sample_mixer(rmsnorm(hs, g_mix[i]), cache_fox_kv[i], cache_fox_logf[i], cache_nsa_kv[i],
                                   state_nsa_win[i], page_table, w_in[i], b_fgate[i], cmp)
        hs = layer_tail(hs, o_s, p_sample[i], *tail)
        outs_p.append(rows_p)
        outs_s.append(rows_s)
    y_prompt = rmsnorm(hp, g_final)
    y_sample = rmsnorm(hs, g_final)
    stack = lambda outs, j: jnp.stack([r[j] for r in outs], axis=0)
    fox_kv_prompt, fox_logf_prompt = stack(outs_p, 0), stack(outs_p, 1)
    nsa_kv_prompt, nsa_win_prompt = stack(outs_p, 2), stack(outs_p, 3)
    fox_kv_sample, fox_logf_sample = stack(outs_s, 0), stack(outs_s, 1)
    nsa_kv_sample, nsa_win_sample = stack(outs_s, 2), stack(outs_s, 3)
    return (y_prompt, y_sample, fox_kv_prompt, fox_logf_prompt, nsa_kv_prompt, nsa_win_prompt,
            fox_kv_sample, fox_logf_sample, nsa_kv_sample, nsa_win_sample)
```

```python
import functools

import numpy as np
import jax
import jax.numpy as jnp
from jax import lax
from jax.experimental import pallas as pl
from jax.experimental.pallas import tpu as pltpu

HEAD_DIM = 128
REP = 4
BLOCK = 64
SEL_TOP_N = 16
WINDOW = 512
Q_TILE = 128
K_TILE = 512
ROPE_THETA = 10000.0
RMS_EPS = 1e-6
ATTN_SCALE = HEAD_DIM ** -0.5
NEG_INF = -1e30
REMOVED = -3e38
FORCE_BONUS = 1e4
LANE = 128
MIB = 1024 * 1024

F32 = jnp.float32
BF16 = jnp.bfloat16


def _params(sem, vmem_mib):
    return pltpu.CompilerParams(dimension_semantics=sem, vmem_limit_bytes=vmem_mib * MIB)


def _dot(a, b):
    return jnp.dot(a, b, preferred_element_type=F32)


def _dot_nt(a, b):
    return lax.dot_general(a, b, (((1,), (1,)), ((), ())), preferred_element_type=F32)


def _row_tile(m, cap):
    t = min(m, cap)
    assert m % t == 0
    return t


def _rmsnorm_kernel(x_ref, g_ref, o_ref):
    x = x_ref[...]
    y = x * lax.rsqrt(jnp.mean(x * x, axis=-1, keepdims=True) + RMS_EPS)
    o_ref[...] = (y * g_ref[...]).astype(o_ref.dtype)


def rmsnorm(x, g, out_dtype):
    m, d = x.shape
    tm = _row_tile(m, 256)
    return pl.pallas_call(
        _rmsnorm_kernel,
        grid=(m // tm,),
        in_specs=[pl.BlockSpec((tm, d), lambda i: (i, 0)), pl.BlockSpec((1, d), lambda i: (0, 0))],
        out_specs=pl.BlockSpec((tm, d), lambda i: (i, 0)),
        out_shape=jax.ShapeDtypeStruct((m, d), out_dtype),
        compiler_params=_params(("parallel",), 32),
        name="rmsnorm",
    )(x, g.reshape(1, d))


def _inproj_kernel(x_ref, w_ref, cos_ref, sin_ref, of_ref, ob_ref, *, tn, rope_ranges):
    acc = _dot(x_ref[...], w_ref[...])
    col0 = pl.program_id(1) * tn
    is_rope = None
    for lo, hi in rope_ranges:
        hit = (col0 >= lo) & (col0 < hi)
        is_rope = hit if is_rope is None else (is_rope | hit)

    @pl.when(is_rope)
    def _():
        cos = cos_ref[...]
        sin = sin_ref[...]
        for h in range(tn // HEAD_DIM):
            sl = slice(h * HEAD_DIM, (h + 1) * HEAD_DIM)
            xh = acc[:, sl]
            r = xh * cos + pltpu.roll(xh, HEAD_DIM // 2, 1) * sin
            of_ref[:, sl] = r
            ob_ref[:, sl] = r.astype(BF16)

    @pl.when(jnp.logical_not(is_rope))
    def _():
        of_ref[...] = acc
        ob_ref[...] = acc.astype(BF16)


def inproj(xn, w_main, cos, sin, tn, rope_ranges):
    m, k = xn.shape
    n = w_main.shape[1]
    tm = _row_tile(m, 1024)
    return pl.pallas_call(
        functools.partial(_inproj_kernel, tn=tn, rope_ranges=rope_ranges),
        grid=(m // tm, n // tn),
        in_specs=[
            pl.BlockSpec((tm, k), lambda i, j: (i, 0)),
            pl.BlockSpec((k, tn), lambda i, j: (0, j)),
            pl.BlockSpec((tm, HEAD_DIM), lambda i, j: (i, 0)),
            pl.BlockSpec((tm, HEAD_DIM), lambda i, j: (i, 0)),
        ],
        out_specs=[pl.BlockSpec((tm, tn), lambda i, j: (i, j)), pl.BlockSpec((tm, tn), lambda i, j: (i, j))],
        out_shape=[jax.ShapeDtypeStruct((m, n), F32), jax.ShapeDtypeStruct((m, n), BF16)],
        compiler_params=_params(("parallel", "arbitrary"), 48),
        name="inproj",
    )(xn, w_main, cos, sin)


def _smallproj_kernel(x_ref, w_ref, b_ref, o_ref, *, n_logf):
    v = _dot(x_ref[...], w_ref[...])
    z = v + b_ref[...]
    logf = -(jnp.maximum(-z, 0.0) + jnp.log1p(jnp.exp(-jnp.abs(z))))
    gate = jax.nn.sigmoid(v)
    lane = lax.broadcasted_iota(jnp.int32, v.shape, 1)
    o_ref[...] = jnp.where(lane < n_logf, logf, gate)


def smallproj(xn, w_small, b_small, n_logf):
    m, k = xn.shape
    tm = _row_tile(m, 1024)
    return pl.pallas_call(
        functools.partial(_smallproj_kernel, n_logf=n_logf),
        grid=(m // tm,),
        in_specs=[
            pl.BlockSpec((tm, k), lambda i: (i, 0)),
            pl.BlockSpec((k, LANE), lambda i: (0, 0)),
            pl.BlockSpec((1, LANE), lambda i: (0, 0)),
        ],
        out_specs=pl.BlockSpec((tm, LANE), lambda i: (i, 0)),
        out_shape=jax.ShapeDtypeStruct((m, LANE), F32),
        compiler_params=_params(("parallel",), 32),
        name="smallproj",
    )(xn, w_small, b_small)


def _outproj_kernel(a_ref, b_ref, wa_ref, wb_ref, h_ref, o_ref):
    o_ref[...] = h_ref[...] + (_dot(a_ref[...], wa_ref[...]) + _dot(b_ref[...], wb_ref[...]))


def outproj(o_fox, o_nsa, w_a, w_b, h):
    m, ka = o_fox.shape
    kb = o_nsa.shape[1]
    n = w_a.shape[1]
    tm, tn = _row_tile(m, 1024), 512
    return pl.pallas_call(
        _outproj_kernel,
        grid=(m // tm, n // tn),
        in_specs=[
            pl.BlockSpec((tm, ka), lambda i, j: (i, 0)),
            pl.BlockSpec((tm, kb), lambda i, j: (i, 0)),
            pl.BlockSpec((ka, tn), lambda i, j: (0, j)),
            pl.BlockSpec((kb, tn), lambda i, j: (0, j)),
            pl.BlockSpec((tm, tn), lambda i, j: (i, j)),
        ],
        out_specs=pl.BlockSpec((tm, tn), lambda i, j: (i, j)),
        out_shape=jax.ShapeDtypeStruct((m, n), F32),
        compiler_params=_params(("parallel", "arbitrary"), 48),
        name="outproj",
    )(o_fox, o_nsa, w_a, w_b, h)


def _gateup_kernel(x_ref, wg_ref, wu_ref, o_ref):
    x = x_ref[...]
    g = _dot(x, wg_ref[...])
    u = _dot(x, wu_ref[...])
    o_ref[...] = (g * jax.nn.sigmoid(g) * u).astype(o_ref.dtype)


def gateup(xn, w_gate, w_up):
    m, k = xn.shape
    n = w_gate.shape[1]
    tm, tn = _row_tile(m, 1024), 256
    return pl.pallas_call(
        _gateup_kernel,
        grid=(m // tm, n // tn),
        in_specs=[
            pl.BlockSpec((tm, k), lambda i, j: (i, 0)),
            pl.BlockSpec((k, tn), lambda i, j: (0, j)),
            pl.BlockSpec((k, tn), lambda i, j: (0, j)),
        ],
        out_specs=pl.BlockSpec((tm, tn), lambda i, j: (i, j)),
        out_shape=jax.ShapeDtypeStruct((m, n), BF16),
        compiler_params=_params(("parallel", "arbitrary"), 48),
        name="gateup",
    )(xn, w_gate, w_up)


def _down_kernel(x_ref, w_ref, h_ref, o_ref):
    o_ref[...] = h_ref[...] + _dot(x_ref[...], w_ref[...])


def downproj(act, w_down, h):
    m, k = act.shape
    n = w_down.shape[1]
    tm, tn = _row_tile(m, 512), 256
    return pl.pallas_call(
        _down_kernel,
        grid=(m // tm, n // tn),
        in_specs=[
            pl.BlockSpec((tm, k), lambda i, j: (i, 0)),
            pl.BlockSpec((k, tn), lambda i, j: (0, j)),
            pl.BlockSpec((tm, tn), lambda i, j: (i, j)),
        ],
        out_specs=pl.BlockSpec((tm, tn), lambda i, j: (i, j)),
        out_shape=jax.ShapeDtypeStruct((m, n), F32),
        compiler_params=_params(("parallel", "arbitrary"), 56),
        name="downproj",
    )(act, w_down, h)


def _ple_kernel(x_ref, wg_ref, p_ref, wp_ref, h_ref, o_ref):
    gate = jax.nn.sigmoid(_dot(x_ref[...], wg_ref[...]))
    o_ref[...] = h_ref[...] + gate * _dot(p_ref[...], wp_ref[...])


def ple(xn, w_gate, p, w_proj, h):
    m, k = xn.shape
    kp = p.shape[1]
    n = w_gate.shape[1]
    tm, tn = _row_tile(m, 1024), 512
    return pl.pallas_call(
        _ple_kernel,
        grid=(m // tm, n // tn),
        in_specs=[
            pl.BlockSpec((tm, k), lambda i, j: (i, 0)),
            pl.BlockSpec((k, tn), lambda i, j: (0, j)),
            pl.BlockSpec((tm, kp), lambda i, j: (i, 0)),
            pl.BlockSpec((kp, tn), lambda i, j: (0, j)),
            pl.BlockSpec((tm, tn), lambda i, j: (i, j)),
        ],
        out_specs=pl.BlockSpec((tm, tn), lambda i, j: (i, j)),
        out_shape=jax.ShapeDtypeStruct((m, n), F32),
        compiler_params=_params(("parallel", "arbitrary"), 48),
        name="ple",
    )(xn, w_gate, p, w_proj, h)


def _split3(x):
    x1 = x.astype(BF16)
    r1 = x - x1.astype(F32)
    x2 = r1.astype(BF16)
    x3 = (r1 - x2.astype(F32)).astype(BF16)
    return x1, x2, x3


def _cumsum_rows_kernel(x_ref, tri_ref, o_ref, carry_ref):
    @pl.when(pl.program_id(0) == 0)
    def _():
        carry_ref[...] = jnp.zeros_like(carry_ref)

    x1, x2, x3 = _split3(x_ref[...])
    tri = tri_ref[...]
    out = (_dot(tri, x1) + _dot(tri, x2) + _dot(tri, x3)) + carry_ref[...]
    o_ref[...] = out
    carry_ref[...] = out[out.shape[0] - 1:, :]


def cumsum_rows(x):
    t, h = x.shape
    c = _row_tile(t, 512)
    tri = jnp.asarray(np.tril(np.ones((c, c), np.float32)), BF16)
    return pl.pallas_call(
        _cumsum_rows_kernel,
        grid=(t // c,),
        in_specs=[pl.BlockSpec((c, h), lambda i: (i, 0)), pl.BlockSpec((c, c), lambda i: (0, 0))],
        out_specs=pl.BlockSpec((c, h), lambda i: (i, 0)),
        out_shape=jax.ShapeDtypeStruct((t, h), F32),
        scratch_shapes=[pltpu.VMEM((1, h), F32)],
        compiler_params=_params(("arbitrary",), 32),
        name="cumsum_rows",
    )(x, tri)


def _cumsum_pages_kernel(pt_ref, *refs, n_pages):
    page_refs = refs[:n_pages]
    new_ref, tri_ref, o_ref = refs[n_pages:]
    tri = tri_ref[...]
    h = new_ref.shape[0]
    carry = jnp.zeros((h, 1), F32)
    for p in range(n_pages + 1):
        x = page_refs[p][...] if p < n_pages else new_ref[...]
        x1, x2, x3 = _split3(x)
        out = (_dot(x1, tri) + _dot(x2, tri) + _dot(x3, tri)) + carry
        o_ref[:, p * LANE:(p + 1) * LANE] = out
        carry = out[:, LANE - 1:]


def cumsum_pages(pt_flat, logf_pool_t, new_t, n_seq, n_pages):
    h = logf_pool_t.shape[1]
    tri = jnp.asarray(np.triu(np.ones((LANE, LANE), np.float32)), BF16)
    page_spec = lambda p: pl.BlockSpec((None, h, LANE), lambda b, pt: (pt[b * n_pages + p], 0, 0))
    grid_spec = pltpu.PrefetchScalarGridSpec(
        num_scalar_prefetch=1,
        grid=(n_seq,),
        in_specs=[page_spec(p) for p in range(n_pages)] + [
            pl.BlockSpec((None, h, LANE), lambda b, pt: (b, 0, 0)),
            pl.BlockSpec((LANE, LANE), lambda b, pt: (0, 0)),
        ],
        out_specs=pl.BlockSpec((None, h, (n_pages + 1) * LANE), lambda b, pt: (b, 0, 0)),
    )
    return pl.pallas_call(
        functools.partial(_cumsum_pages_kernel, n_pages=n_pages),
        grid_spec=grid_spec,
        out_shape=jax.ShapeDtypeStruct((n_seq, h, (n_pages + 1) * LANE), F32),
        compiler_params=_params(("arbitrary",), 32),
        name="cumsum_pages",
    )(pt_flat, *([logf_pool_t] * n_pages), new_t, tri)


def _silu(x):
    return x * jax.nn.sigmoid(x)


def _compress_prompt_kernel(cb_ref, x_ref, pe_ref, w1_ref, w2_ref, o_ref, *, nb):
    hidden = w1_ref.shape[2]

    def body(l, acc):
        xl = x_ref[pl.ds(l, nb, stride=BLOCK), :] + pe_ref[pl.ds(l, 1), :]
        return acc + _dot(xl.astype(BF16), w1_ref[l])

    acc = lax.fori_loop(0, BLOCK, body, jnp.zeros((nb, hidden), F32))
    out = _dot(_silu(acc).astype(BF16), w2_ref[...])
    o_ref[...] = jnp.zeros_like(o_ref)
    o_ref[0:nb, :] = out


def compress_prompt(proj_f32, col_blocks, pe, w1, w2, n_groups, nbp):
    t = proj_f32.shape[0]
    nb = t // BLOCK
    hidden = w1.shape[-1]
    cb = jnp.asarray(col_blocks, jnp.int32)
    grid_spec = pltpu.PrefetchScalarGridSpec(
        num_scalar_prefetch=1,
        grid=(2, n_groups),
        in_specs=[
            pl.BlockSpec((t, HEAD_DIM), lambda kv, g, cb: (0, cb[kv] + g)),
            pl.BlockSpec((None, BLOCK, HEAD_DIM), lambda kv, g, cb: (kv, 0, 0)),
            pl.BlockSpec((None, BLOCK, HEAD_DIM, hidden), lambda kv, g, cb: (kv, 0, 0, 0)),
            pl.BlockSpec((None, hidden, HEAD_DIM), lambda kv, g, cb: (kv, 0, 0)),
        ],
        out_specs=pl.BlockSpec((None, None, nbp, HEAD_DIM), lambda kv, g, cb: (kv, g, 0, 0)),
    )
    return pl.pallas_call(
        functools.partial(_compress_prompt_kernel, nb=nb),
        grid_spec=grid_spec,
        out_shape=jax.ShapeDtypeStruct((2, n_groups, nbp, HEAD_DIM), F32),
        compiler_params=_params(("arbitrary", "arbitrary"), 40),
        name="compress_prompt",
    )(cb, proj_f32, pe, w1, w2)


def _compress_sample_kernel(pt_ref, *refs, n_pages, n_groups, nb, rows_pad):
    kpages = refs[:n_pages]
    vpages = refs[n_pages:2 * n_pages]
    new_ref, pe_ref, w1_ref, w2_ref, o_ref = refs[2 * n_pages:]
    hidden = w1_ref.shape[3]
    gw = n_groups * HEAD_DIM
    t_new = new_ref.shape[0]
    n_rows = 2 * n_pages + 1
    pad_rows = rows_pad - n_rows
    o_ref[...] = jnp.zeros_like(o_ref)
    for kv, pages in enumerate((kpages, vpages)):
        new_rows = new_ref[:, kv * gw:(kv + 1) * gw]

        def body(l, acc):
            rows = []
            for pg in pages:
                rows.append(pg[pl.ds(l, 1), :])
                rows.append(pg[pl.ds(l + BLOCK, 1), :])
            lane_row = lax.broadcasted_iota(jnp.int32, (t_new, gw), 0)
            new_l = jnp.sum(jnp.where(lane_row == l, new_rows, 0.0), axis=0, keepdims=True)
            rows.append(new_l)
            rows.append(jnp.zeros((pad_rows, gw), F32))
            x = jnp.concatenate(rows, axis=0)
            xg = jnp.concatenate([x[:, g * HEAD_DIM:(g + 1) * HEAD_DIM] for g in range(n_groups)], axis=0)
            xg = xg + pe_ref[kv, pl.ds(l, 1), :]
            return acc + _dot(xg.astype(BF16), w1_ref[kv, l])

        acc = lax.fori_loop(0, BLOCK, body, jnp.zeros((n_groups * rows_pad, hidden), F32))
        out = _dot(_silu(acc).astype(BF16), w2_ref[kv])
        valid = lax.broadcasted_iota(jnp.int32, (rows_pad, HEAD_DIM), 0) < nb
        for g in range(n_groups):
            o_ref[kv, g, 0:rows_pad, :] = jnp.where(valid, out[g * rows_pad:(g + 1) * rows_pad, :], 0.0)


def compress_sample(pt_flat, nsa_pool, new_rows, new_col_block, pe, w1, w2, n_seq, n_pages, n_groups, nb, nbp):
    page = nsa_pool.shape[1]
    assert page == 2 * BLOCK
    gw = n_groups * HEAD_DIM
    hidden = w1.shape[-1]
    t_new = new_rows.shape[0] // n_seq
    rows_pad = -(-(2 * n_pages + 1) // 8) * 8
    spec = lambda p, slot: pl.BlockSpec((None, page, gw), lambda b, pt: (pt[b * n_pages + p], 0, slot))
    grid_spec = pltpu.PrefetchScalarGridSpec(
        num_scalar_prefetch=1,
        grid=(n_seq,),
        in_specs=[spec(p, 0) for p in range(n_pages)] + [spec(p, 1) for p in range(n_pages)] + [
            pl.BlockSpec((t_new, 2 * gw), lambda b, pt: (b, new_col_block)),
            pl.BlockSpec((2, BLOCK, HEAD_DIM), lambda b, pt: (0, 0, 0)),
            pl.BlockSpec((2, BLOCK, HEAD_DIM, hidden), lambda b, pt: (0, 0, 0, 0)),
            pl.BlockSpec((2, hidden, HEAD_DIM), lambda b, pt: (0, 0, 0)),
        ],
        out_specs=pl.BlockSpec((None, 2, n_groups, nbp, HEAD_DIM), lambda b, pt: (b, 0, 0, 0, 0)),
    )
    return pl.pallas_call(
        functools.partial(_compress_sample_kernel, n_pages=n_pages, n_groups=n_groups, nb=nb, rows_pad=rows_pad),
        grid_spec=grid_spec,
        out_shape=jax.ShapeDtypeStruct((n_seq, 2, n_groups, nbp, HEAD_DIM), F32),
        compiler_params=_params(("arbitrary",), 56),
        name="compress_sample",
    )(pt_flat, *([nsa_pool] * (2 * n_pages)), new_rows, pe, w1, w2)


def _stack_heads(q):
    return jnp.concatenate([q[:, r * HEAD_DIM:(r + 1) * HEAD_DIM] for r in range(REP)], axis=0)


def _softmax_rows(s):
    m = jnp.max(s, axis=-1, keepdims=True)
    e = jnp.exp(s - m)
    return e / jnp.sum(e, axis=-1, keepdims=True)


def _top_n_mask_t(score_t, blk_t, n_sel, n_blocks_pad):
    sel = jnp.zeros(score_t.shape, F32)
    for _ in range(n_sel):
        mx = jnp.max(score_t, axis=0, keepdims=True)
        idx = jnp.min(jnp.where(score_t == mx, blk_t, n_blocks_pad), axis=0, keepdims=True)
        hit = blk_t == idx
        sel = jnp.where(hit, 1.0, sel)
        score_t = jnp.where(hit, REMOVED, score_t)
    return sel


def _selection_scores_t(imp_t, blk_t, qpos_t):
    cur = qpos_t // BLOCK
    forced = (blk_t == 0) | (blk_t == cur) | (blk_t == cur - 1)
    avail = blk_t * BLOCK <= qpos_t
    return jnp.where(avail, imp_t + jnp.where(forced, FORCE_BONUS, 0.0), NEG_INF)


def _online_update(s, v_tile, m_ref, l_ref, acc_ref):
    m_old = m_ref[...]
    m_new = jnp.maximum(m_old, jnp.max(s, axis=-1, keepdims=True))
    alpha = jnp.exp(m_old - m_new)
    p = jnp.exp(s - m_new)
    l_ref[...] = alpha * l_ref[...] + jnp.sum(p, axis=-1, keepdims=True)
    acc_ref[...] = alpha * acc_ref[...] + _dot(p.astype(BF16), v_tile)
    m_ref[...] = m_new


def _fox_prompt_kernel(q_ref, k_ref, v_ref, fq_ref, fk_ref, o_ref, m_ref, l_ref, acc_ref):
    i = pl.program_id(1)
    rows = REP * Q_TILE
    q4 = _stack_heads(q_ref[...])
    fq = jnp.concatenate([fq_ref[:, r:r + 1] for r in range(REP)], axis=0)
    m_ref[...] = jnp.full((rows, 1), NEG_INF, F32)
    l_ref[...] = jnp.zeros((rows, 1), F32)
    acc_ref[...] = jnp.zeros((rows, HEAD_DIM), F32)
    n_full = (i * Q_TILE) // K_TILE

    def tile(j, masked):
        k0 = pl.multiple_of(j * K_TILE, K_TILE)
        s = _dot_nt(q4, k_ref[pl.ds(k0, K_TILE), :]) * ATTN_SCALE
        fk = fk_ref[:, pl.ds(k0, K_TILE)]
        s = s.reshape(REP, Q_TILE, K_TILE) - fk[:, None, :]
        if masked:
            kpos = k0 + lax.broadcasted_iota(jnp.int32, (Q_TILE, K_TILE), 1)
            qpos = i * Q_TILE + lax.broadcasted_iota(jnp.int32, (Q_TILE, K_TILE), 0)
            s = jnp.where((kpos <= qpos)[None], s, NEG_INF)
        s = s.reshape(rows, K_TILE) + fq
        _online_update(s, v_ref[pl.ds(k0, K_TILE), :], m_ref, l_ref, acc_ref)

    def body(j, c):
        tile(j, False)
        return c

    lax.fori_loop(0, n_full, body, 0)
    tile(n_full, True)
    o = acc_ref[...] / l_ref[...]
    for r in range(REP):
        o_ref[:, r * HEAD_DIM:(r + 1) * HEAD_DIM] = o[r * Q_TILE:(r + 1) * Q_TILE, :].astype(o_ref.dtype)


def fox_prompt(proj_bf16, q_blk0, k_blk0, v_blk0, fq_g, fk_g, n_kv):
    t = proj_bf16.shape[0]
    qw = REP * HEAD_DIM
    rows = REP * Q_TILE
    return pl.pallas_call(
        _fox_prompt_kernel,
        grid=(n_kv, t // Q_TILE),
        in_specs=[
            pl.BlockSpec((Q_TILE, qw), lambda g, i: (i, q_blk0 + g)),
            pl.BlockSpec((t, HEAD_DIM), lambda g, i: (0, k_blk0 + g)),
            pl.BlockSpec((t, HEAD_DIM), lambda g, i: (0, v_blk0 + g)),
            pl.BlockSpec((None, Q_TILE, REP), lambda g, i: (g, i, 0)),
            pl.BlockSpec((None, REP, t), lambda g, i: (g, 0, 0)),
        ],
        out_specs=pl.BlockSpec((Q_TILE, qw), lambda g, i: (i, g)),
        out_shape=jax.ShapeDtypeStruct((t, n_kv * qw), BF16),
        scratch_shapes=[pltpu.VMEM((rows, 1), F32), pltpu.VMEM((rows, 1), F32), pltpu.VMEM((rows, HEAD_DIM), F32)],
        compiler_params=_params(("arbitrary", "arbitrary"), 40),
        name="fox_prompt",
    )(proj_bf16, proj_bf16, proj_bf16, fq_g, fk_g)


def _nsa_prompt_kernel(q_ref, ks_ref, vs_ref, kw_ref, vw_ref, cmp_k_ref, cmp_v_ref, e_ref, gate_ref, o_ref,
                       m_ref, l_ref, acc_ref, *, n_sel):
    i = pl.program_id(1)
    t = ks_ref.shape[0]
    nbp = cmp_k_ref.shape[0]
    rows = REP * Q_TILE
    q0 = i * Q_TILE
    q4 = _stack_heads(q_ref[...])

    s_c = _dot_nt(q4, cmp_k_ref[...].astype(BF16)) * ATTN_SCALE
    blk = lax.broadcasted_iota(jnp.int32, (Q_TILE, nbp), 1)
    qpos = q0 + lax.broadcasted_iota(jnp.int32, (Q_TILE, nbp), 0)
    c_mask = (blk + 1) * BLOCK - 1 <= qpos
    s_c = jnp.where(c_mask[None], s_c.reshape(REP, Q_TILE, nbp), NEG_INF)
    any_vis = (qpos >= BLOCK - 1).astype(F32)
    p_c = _softmax_rows(s_c) * any_vis[None]
    o_cmp = _dot(p_c.reshape(rows, nbp).astype(BF16), cmp_v_ref[...].astype(BF16))

    imp = p_c[0]
    for r in range(1, REP):
        imp = imp + p_c[r]
    blk_t = lax.broadcasted_iota(jnp.int32, (nbp, Q_TILE), 0)
    qpos_t = q0 + lax.broadcasted_iota(jnp.int32, (nbp, Q_TILE), 1)
    sel_t = _top_n_mask_t(_selection_scores_t(imp.T, blk_t, qpos_t), blk_t, n_sel, nbp)
    sel = sel_t.T.astype(BF16)

    m_ref[...] = jnp.full((rows, 1), NEG_INF, F32)
    l_ref[...] = jnp.zeros((rows, 1), F32)
    acc_ref[...] = jnp.zeros((rows, HEAD_DIM), F32)
    n_full = q0 // K_TILE

    def tile(j, masked):
        k0 = pl.multiple_of(j * K_TILE, K_TILE)
        s = _dot_nt(q4, ks_ref[pl.ds(k0, K_TILE), :]) * ATTN_SCALE
        picked = _dot(sel, e_ref[:, pl.ds(k0, K_TILE)]) > 0.5
        if masked:
            kpos = k0 + lax.broadcasted_iota(jnp.int32, (Q_TILE, K_TILE), 1)
            qp = q0 + lax.broadcasted_iota(jnp.int32, (Q_TILE, K_TILE), 0)
            picked = picked & (kpos <= qp)
        s = jnp.where(picked[None], s.reshape(REP, Q_TILE, K_TILE), NEG_INF).reshape(rows, K_TILE)
        _online_update(s, vs_ref[pl.ds(k0, K_TILE), :], m_ref, l_ref, acc_ref)

    def body(j, c):
        tile(j, False)
        return c

    lax.fori_loop(0, n_full, body, 0)
    tile(n_full, True)
    o_slc = acc_ref[...] / l_ref[...]

    wlen = WINDOW + Q_TILE
    w0 = pl.multiple_of(jnp.maximum(q0 - WINDOW, 0), Q_TILE)
    s_w = _dot_nt(q4, kw_ref[pl.ds(w0, wlen), :]) * ATTN_SCALE
    wpos = w0 + lax.broadcasted_iota(jnp.int32, (Q_TILE, wlen), 1)
    dist = q0 + lax.broadcasted_iota(jnp.int32, (Q_TILE, wlen), 0) - wpos
    w_mask = (dist >= 0) & (dist < WINDOW)
    s_w = jnp.where(w_mask[None], s_w.reshape(REP, Q_TILE, wlen), NEG_INF).reshape(rows, wlen)
    o_win = _dot(_softmax_rows(s_w).astype(BF16), vw_ref[pl.ds(w0, wlen), :])

    gates = gate_ref[...]
    for r in range(REP):
        rs = slice(r * Q_TILE, (r + 1) * Q_TILE)
        o = (gates[:, 3 * r:3 * r + 1] * o_cmp[rs] + gates[:, 3 * r + 1:3 * r + 2] * o_slc[rs]
             + gates[:, 3 * r + 2:3 * r + 3] * o_win[rs])
        o_ref[:, r * HEAD_DIM:(r + 1) * HEAD_DIM] = o.astype(o_ref.dtype)


def nsa_prompt(proj_bf16, q_blk0, ks_blk0, vs_blk0, kw_blk0, vw_blk0, cmp, expand, gates_g, n_groups, n_sel):
    t = proj_bf16.shape[0]
    nbp = cmp.shape[2]
    qw = REP * HEAD_DIM
    rows = REP * Q_TILE
    col = lambda blk0: pl.BlockSpec((t, HEAD_DIM), lambda g, i: (0, blk0 + g))
    return pl.pallas_call(
        functools.partial(_nsa_prompt_kernel, n_sel=n_sel),
        grid=(n_groups, t // Q_TILE),
        in_specs=[
            pl.BlockSpec((Q_TILE, qw), lambda g, i: (i, q_blk0 + g)),
            col(ks_blk0), col(vs_blk0), col(kw_blk0), col(vw_blk0),
            pl.BlockSpec((None, None, nbp, HEAD_DIM), lambda g, i: (0, g, 0, 0)),
            pl.BlockSpec((None, None, nbp, HEAD_DIM), lambda g, i: (1, g, 0, 0)),
            pl.BlockSpec((nbp, t), lambda g, i: (0, 0)),
            pl.BlockSpec((None, Q_TILE, 16), lambda g, i: (g, i, 0)),
        ],
        out_specs=pl.BlockSpec((Q_TILE, qw), lambda g, i: (i, g)),
        out_shape=jax.ShapeDtypeStruct((t, n_groups * qw), BF16),
        scratch_shapes=[pltpu.VMEM((rows, 1), F32), pltpu.VMEM((rows, 1), F32), pltpu.VMEM((rows, HEAD_DIM), F32)],
        compiler_params=_params(("arbitrary", "arbitrary"), 48),
        name="nsa_prompt",
    )(proj_bf16, proj_bf16, proj_bf16, proj_bf16, proj_bf16, cmp, cmp, expand, gates_g)


def _pad_rows(x, n):
    return jnp.concatenate([x, jnp.zeros((n - x.shape[0], x.shape[1]), x.dtype)], axis=0)


def _fox_sample_kernel(pt_ref, *refs, n_pages, n_kv):
    pages = refs[:n_pages]
    q_ref, new_ref, fq_ref, fk_ref, o_ref = refs[n_pages:]
    page = pages[0].shape[0]
    t_new = new_ref.shape[0]
    rows = REP * t_new
    kvw = n_kv * HEAD_DIM
    row_t = lax.broadcasted_iota(jnp.int32, (t_new, page), 0)
    lane = lax.broadcasted_iota(jnp.int32, (t_new, page), 1)
    new_mask = lane <= row_t
    for g in range(n_kv):
        q4 = q_ref[g]
        ks = slice(g * HEAD_DIM, (g + 1) * HEAD_DIM)
        vs = slice(kvw + g * HEAD_DIM, kvw + (g + 1) * HEAD_DIM)
        chunks = []
        for p in range(n_pages + 1):
            if p < n_pages:
                k_p = pages[p][:, ks].astype(BF16)
            else:
                k_p = _pad_rows(new_ref[:, ks], page).astype(BF16)
            s = _dot_nt(q4, k_p) * ATTN_SCALE
            fk = fk_ref[g * REP:(g + 1) * REP, p * page:(p + 1) * page]
            s = s.reshape(REP, t_new, page) - fk[:, None, :]
            if p == n_pages:
                s = jnp.where(new_mask[None], s, NEG_INF)
            chunks.append(s.reshape(rows, page))
        fq = jnp.concatenate([fq_ref[:, g * REP + r:g * REP + r + 1] for r in range(REP)], axis=0)
        prob = _softmax_rows(jnp.concatenate(chunks, axis=1) + fq).astype(BF16)
        o = jnp.zeros((rows, HEAD_DIM), F32)
        for p in range(n_pages + 1):
            if p < n_pages:
                v_p = pages[p][:, vs].astype(BF16)
            else:
                v_p = _pad_rows(new_ref[:, vs], page).astype(BF16)
            o = o + _dot(prob[:, p * page:(p + 1) * page], v_p)
        o_ref[g] = o.astype(o_ref.dtype)


def fox_sample(pt_flat, fox_pool, q_s, new_rows, new_col_block, fq_s, fk_s, n_seq, n_pages, n_kv):
    page, width = fox_pool.shape[1:]
    t_new = new_rows.shape[0] // n_seq
    rows = REP * t_new
    n_heads = fq_s.shape[2]
    assert page == LANE
    grid_spec = pltpu.PrefetchScalarGridSpec(
        num_scalar_prefetch=1,
        grid=(n_seq,),
        in_specs=[pl.BlockSpec((None, page, width), lambda b, pt, p=p: (pt[b * n_pages + p], 0, 0))
                  for p in range(n_pages)] + [
            pl.BlockSpec((None, n_kv, rows, HEAD_DIM), lambda b, pt: (b, 0, 0, 0)),
            pl.BlockSpec((t_new, width), lambda b, pt: (b, new_col_block)),
            pl.BlockSpec((None, t_new, n_heads), lambda b, pt: (b, 0, 0)),
            pl.BlockSpec((None, n_heads, (n_pages + 1) * page), lambda b, pt: (b, 0, 0)),
        ],
        out_specs=pl.BlockSpec((None, n_kv, rows, HEAD_DIM), lambda b, pt: (b, 0, 0, 0)),
    )
    return pl.pallas_call(
        functools.partial(_fox_sample_kernel, n_pages=n_pages, n_kv=n_kv),
        grid_spec=grid_spec,
        out_shape=jax.ShapeDtypeStruct((n_seq, n_kv, rows, HEAD_DIM), BF16),
        compiler_params=_params(("arbitrary",), 48),
        name="fox_sample",
    )(pt_flat, *([fox_pool] * n_pages), q_s, new_rows, fq_s, fk_s)


def _nsa_sample_kernel(pt_ref, *refs, n_pages, n_groups, n_sel, past_len):
    kpages = refs[:n_pages]
    vpages = refs[n_pages:2 * n_pages]
    q_ref, cmp_ref, new_slc_ref, new_win_ref, win_ref, gate_ref, o_ref, win_out_ref = refs[2 * n_pages:]
    page = kpages[0].shape[0]
    t_new = new_slc_ref.shape[0]
    rows = REP * t_new
    gw = n_groups * HEAD_DIM
    nbp = cmp_ref.shape[2]
    n_buf = win_ref.shape[0]
    blocks_per_page = page // BLOCK

    row_t = lax.broadcasted_iota(jnp.int32, (t_new, nbp), 0)
    blk = lax.broadcasted_iota(jnp.int32, (t_new, nbp), 1)
    qpos = past_len + row_t
    c_mask = (blk + 1) * BLOCK - 1 <= qpos
    any_vis = (qpos >= BLOCK - 1).astype(F32)
    o_cmp, imps = [], []
    for g in range(n_groups):
        s_c = _dot_nt(q_ref[g], cmp_ref[0, g].astype(BF16)) * ATTN_SCALE
        s_c = jnp.where(c_mask[None], s_c.reshape(REP, t_new, nbp), NEG_INF)
        p_c = _softmax_rows(s_c) * any_vis[None]
        o_cmp.append(_dot(p_c.reshape(rows, nbp).astype(BF16), cmp_ref[1, g].astype(BF16)))
        imp = p_c[0]
        for r in range(1, REP):
            imp = imp + p_c[r]
        imps.append(imp)

    imp_all = _pad_rows(jnp.concatenate(imps, axis=0), LANE)
    blk_t = lax.broadcasted_iota(jnp.int32, (nbp, LANE), 0)
    qpos_t = past_len + lax.broadcasted_iota(jnp.int32, (nbp, LANE), 1) % t_new
    sel_t = _top_n_mask_t(_selection_scores_t(imp_all.T, blk_t, qpos_t), blk_t, n_sel, nbp)
    sel_all = sel_t.T

    lane_p = lax.broadcasted_iota(jnp.int32, (t_new, page), 1)
    row_p = lax.broadcasted_iota(jnp.int32, (t_new, page), 0)
    lane_blk = lane_p // BLOCK
    widx = lax.broadcasted_iota(jnp.int32, (t_new, n_buf + page), 1)
    wdist = n_buf + lax.broadcasted_iota(jnp.int32, (t_new, n_buf + page), 0) - widx
    w_mask = (wdist >= 0) & (wdist < WINDOW) & (widx < n_buf + t_new)
    gates = gate_ref[...]

    for g in range(n_groups):
        q4 = q_ref[g]
        gs = slice(g * HEAD_DIM, (g + 1) * HEAD_DIM)
        vs = slice(gw + g * HEAD_DIM, gw + (g + 1) * HEAD_DIM)
        sel_g = sel_all[g * t_new:(g + 1) * t_new, :]

        chunks = []
        for p in range(n_pages + 1):
            if p < n_pages:
                k_p = kpages[p][:, gs].astype(BF16)
            else:
                k_p = _pad_rows(new_slc_ref[:, gs], page).astype(BF16)
            s = _dot_nt(q4, k_p) * ATTN_SCALE
            picked = jnp.zeros((t_new, page), F32)
            for c in range(blocks_per_page):
                b_idx = p * blocks_per_page + c
                picked = jnp.where(lane_blk == c, sel_g[:, b_idx:b_idx + 1], picked)
            ok = picked > 0.5
            if p == n_pages:
                ok = ok & (lane_p <= row_p)
            chunks.append(jnp.where(ok[None], s.reshape(REP, t_new, page), NEG_INF).reshape(rows, page))
        prob = _softmax_rows(jnp.concatenate(chunks, axis=1)).astype(BF16)
        o_slc = jnp.zeros((rows, HEAD_DIM), F32)
        for p in range(n_pages + 1):
            if p < n_pages:
                v_p = vpages[p][:, gs].astype(BF16)
            else:
                v_p = _pad_rows(new_slc_ref[:, vs], page).astype(BF16)
            o_slc = o_slc + _dot(prob[:, p * page:(p + 1) * page], v_p)

        kw = jnp.concatenate([win_ref[:, gs], _pad_rows(new_win_ref[:, gs], page)], axis=0).astype(BF16)
        vw = jnp.concatenate([win_ref[:, vs], _pad_rows(new_win_ref[:, vs], page)], axis=0).astype(BF16)
        s_w = _dot_nt(q4, kw) * ATTN_SCALE
        s_w = jnp.where(w_mask[None], s_w.reshape(REP, t_new, n_buf + page), NEG_INF).reshape(rows, n_buf + page)
        o_win = _dot(_softmax_rows(s_w).astype(BF16), vw)

        gt = gates[g]
        o_ref[g] = (gt[:, 0:1] * o_cmp[g] + gt[:, 1:2] * o_slc + gt[:, 2:3] * o_win).astype(o_ref.dtype)

    win_out_ref[0:n_buf - t_new, :] = win_ref[t_new:n_buf, :]
    win_out_ref[n_buf - t_new:n_buf, :] = new_win_ref[...]


def nsa_sample(pt_flat, nsa_pool, q_s, cmp_s, new_rows, slc_col_block, win_col_block, win_state, gates_s,
               n_seq, n_pages, n_groups, n_sel, past_len):
    page = nsa_pool.shape[1]
    gw = n_groups * HEAD_DIM
    t_new = new_rows.shape[0] // n_seq
    rows = REP * t_new
    nbp = cmp_s.shape[3]
    n_buf = win_state.shape[1]
    assert n_buf > t_new and n_buf % 8 == 0 and t_new % 8 == 0
    spec = lambda p, slot: pl.BlockSpec((None, page, gw), lambda b, pt: (pt[b * n_pages + p], 0, slot))
    grid_spec = pltpu.PrefetchScalarGridSpec(
        num_scalar_prefetch=1,
        grid=(n_seq,),
        in_specs=[spec(p, 2) for p in range(n_pages)] + [spec(p, 3) for p in range(n_pages)] + [
            pl.BlockSpec((None, n_groups, rows, HEAD_DIM), lambda b, pt: (b, 0, 0, 0)),
            pl.BlockSpec((None, 2, n_groups, nbp, HEAD_DIM), lambda b, pt: (b, 0, 0, 0, 0)),
            pl.BlockSpec((t_new, 2 * gw), lambda b, pt: (b, slc_col_block)),
            pl.BlockSpec((t_new, 2 * gw), lambda b, pt: (b, win_col_block)),
            pl.BlockSpec((None, n_buf, 2 * gw), lambda b, pt: (b, 0, 0)),
            pl.BlockSpec((None, n_groups, rows, 16), lambda b, pt: (b, 0, 0, 0)),
        ],
        out_specs=[
            pl.BlockSpec((None, n_groups, rows, HEAD_DIM), lambda b, pt: (b, 0, 0, 0)),
            pl.BlockSpec((None, n_buf, 2 * gw), lambda b, pt: (b, 0, 0)),
        ],
    )
    return pl.pallas_call(
        functools.partial(_nsa_sample_kernel, n_pages=n_pages, n_groups=n_groups, n_sel=n_sel, past_len=past_len),
        grid_spec=grid_spec,
        out_shape=[jax.ShapeDtypeStruct((n_seq, n_groups, rows, HEAD_DIM), BF16),
                   jax.ShapeDtypeStruct(win_state.shape, F32)],
        compiler_params=_params(("arbitrary",), 56),
        name="nsa_sample",
    )(pt_flat, *([nsa_pool] * (2 * n_pages)), q_s, cmp_s, new_rows, new_rows, win_state, gates_s)


def _rope_tables(pos):
    half = HEAD_DIM // 2
    inv_freq = ROPE_THETA ** (-jnp.arange(half, dtype=F32) / half)
    ang = pos.astype(F32)[:, None] * inv_freq[None, :]
    cos, sin = jnp.cos(ang), jnp.sin(ang)
    return jnp.concatenate([cos, cos], axis=-1), jnp.concatenate([-sin, sin], axis=-1)


def _heads_major(x, n_seq, t_new, n_groups):
    w = x.shape[1] // (n_groups * REP)
    return x.reshape(n_seq, t_new, n_groups, REP, w).transpose(0, 2, 3, 1, 4).reshape(n_seq, n_groups, REP * t_new, w)


def _tokens_major(x, n_seq, t_new, n_groups):
    w = x.shape[-1]
    return x.reshape(n_seq, n_groups, REP, t_new, w).transpose(0, 3, 1, 2, 4).reshape(n_seq * t_new, n_groups * REP * w)


def kernel(x_prompt, x_sample, cache_fox_kv, cache_fox_logf, cache_nsa_kv, state_nsa_win, page_table, p_prompt, p_sample, g_mix, w_in, b_fgate, cmp_pe_k, cmp_w1_k, cmp_w2_k, cmp_pe_v, cmp_w1_v, cmp_w2_v, w_out, g_ffn, w_gate, w_up, w_down, g_ple, w_ple_gate, w_ple_proj, g_final):
    depth = w_in.shape[0]
    _, seq, d_model = x_prompt.shape
    n_seq, t_new, _ = x_sample.shape
    n_pool, page = cache_fox_kv.shape[1:3]
    n_pages = page_table.shape[1]
    past_len = n_pages * page
    fox_heads = b_fgate.shape[1]
    n_kv = cache_fox_kv.shape[4]
    n_groups = cache_nsa_kv.shape[4]
    nsa_heads = n_groups * REP
    assert fox_heads == n_kv * REP and w_out.shape[1] == (fox_heads + nsa_heads) * HEAD_DIM
    fq_w, nq_w = fox_heads * HEAD_DIM, nsa_heads * HEAD_DIM
    fkv_w, nkv_w = n_kv * HEAD_DIM, n_groups * HEAD_DIM
    n_gate = 3 * nsa_heads
    assert fox_heads + n_gate <= LANE and seq % K_TILE == 0 and seq >= WINDOW + Q_TILE

    sizes = [fq_w, fkv_w, fkv_w, fox_heads, nq_w] + [nkv_w] * 6 + [n_gate]
    off = np.concatenate([[0], np.cumsum(sizes)]).astype(int)
    o_fq, o_fk, o_fv, o_fl, o_nq, o_kc, o_vc, o_ks, o_vs, o_kw, o_vw, o_g = [int(v) for v in off[:-1]]
    c_fq, c_nq = 0, fq_w
    c_fk = c_nq + nq_w
    c_fv = c_fk + fkv_w
    c_kc = c_fv + fkv_w
    c_vc, c_ks, c_vs, c_kw, c_vw = (c_kc + nkv_w * k for k in range(1, 6))
    n_main = c_vw + nkv_w
    tn = min(512, fkv_w, nkv_w)
    rope_ranges = ((c_nq, c_nq + nq_w), (c_kc, c_kc + nkv_w), (c_ks, c_ks + nkv_w), (c_kw, c_kw + nkv_w))

    nb_p = seq // BLOCK
    nbp_p = max(LANE, nb_p)
    nb_s = -(-(past_len + t_new) // BLOCK)
    nbp_s = LANE
    assert nb_s <= LANE and nbp_p % LANE == 0
    n_win_p = min(WINDOW, seq)

    cos_p, sin_p = _rope_tables(jnp.arange(seq))
    cos_s, sin_s = _rope_tables(jnp.tile(past_len + jnp.arange(t_new), n_seq))
    expand = jnp.asarray(np.arange(nbp_p)[:, None] == (np.arange(seq)[None, :] // BLOCK), BF16)
    pt_flat = page_table.reshape(-1).astype(jnp.int32)

    hp = x_prompt.reshape(seq, d_model)
    hs = x_sample.reshape(n_seq * t_new, d_model)
    outs_p, outs_s = [], []
    for i in range(depth):
        wi = w_in[i]
        w_main = jnp.concatenate(
            [wi[:, o_fq:o_fq + fq_w], wi[:, o_nq:o_nq + nq_w], wi[:, o_fk:o_fl], wi[:, o_kc:o_g]], axis=1).astype(BF16)
        w_small = jnp.concatenate(
            [wi[:, o_fl:o_fl + fox_heads], wi[:, o_g:o_g + n_gate],
             jnp.zeros((d_model, LANE - fox_heads - n_gate), F32)], axis=1).astype(BF16)
        b_small = jnp.concatenate([b_fgate[i], jnp.zeros((LANE - fox_heads,), F32)]).reshape(1, LANE)
        w_o = w_out[i].astype(BF16)
        w_oa, w_ob = w_o[:fq_w], w_o[fq_w:]
        w_g, w_u, w_d = w_gate[i].astype(BF16), w_up[i].astype(BF16), w_down[i].astype(BF16)
        w_pg, w_pp = w_ple_gate[i].astype(BF16), w_ple_proj[i].astype(BF16)
        pe = jnp.stack([cmp_pe_k[i], cmp_pe_v[i]])
        hidden = cmp_w1_k.shape[-1]
        w1 = jnp.stack([cmp_w1_k[i], cmp_w1_v[i]]).reshape(2, BLOCK, HEAD_DIM, hidden).astype(BF16)
        w2 = jnp.stack([cmp_w2_k[i], cmp_w2_v[i]]).astype(BF16)

        def tail(h, o_fox, o_nsa, p):
            h1 = outproj(o_fox, o_nsa, w_oa, w_ob, h)
            act = gateup(rmsnorm(h1, g_ffn[i], BF16), w_g, w_u)
            h2 = downproj(act, w_d, h1)
            return ple(rmsnorm(h2, g_ple[i], BF16), w_pg, p.astype(BF16), w_pp, h2)

        xn = rmsnorm(hp, g_mix[i], BF16)
        pf, pb = inproj(xn, w_main, cos_p, sin_p, tn, rope_ranges)
        small = smallproj(xn, w_small, b_small, fox_heads)
        logf = small[:, :fox_heads]
        f_cum = cumsum_rows(logf)
        fq_g = f_cum.reshape(seq, n_kv, REP).transpose(1, 0, 2)
        fk_g = f_cum.T.reshape(n_kv, REP, seq)
        o_fox = fox_prompt(pb, c_fq // (REP * HEAD_DIM), c_fk // HEAD_DIM, c_fv // HEAD_DIM, fq_g, fk_g, n_kv)
        cmp_p = compress_prompt(pf, (c_kc // HEAD_DIM, c_vc // HEAD_DIM), pe, w1, w2, n_groups, nbp_p)
        gates = small[:, fox_heads:fox_heads + n_gate].reshape(seq, n_groups, REP * 3).transpose(1, 0, 2)
        gates_g = jnp.pad(gates, ((0, 0), (0, 0), (0, 16 - REP * 3)))
        o_nsa = nsa_prompt(pb, c_nq // (REP * HEAD_DIM), c_ks // HEAD_DIM, c_vs // HEAD_DIM, c_kw // HEAD_DIM,
                           c_vw // HEAD_DIM, cmp_p, expand, gates_g, n_groups, min(SEL_TOP_N, nb_p))
        hp = tail(hp, o_fox, o_nsa, p_prompt[i].reshape(seq, -1))
        outs_p.append((
            pf[:, c_fk:c_fk + 2 * fkv_w].reshape(1, seq, 2, n_kv, HEAD_DIM),
            logf.reshape(1, seq, fox_heads),
            pf[:, c_kc:c_kc + 4 * nkv_w].reshape(1, seq, 4, n_groups, HEAD_DIM),
            pf[seq - n_win_p:, c_kw:c_kw + 2 * nkv_w].reshape(1, n_win_p, 2, n_groups, HEAD_DIM),
        ))

        m_s = n_seq * t_new
        xs = rmsnorm(hs, g_mix[i], BF16)
        sf, sb = inproj(xs, w_main, cos_s, sin_s, tn, rope_ranges)
        small_s = smallproj(xs, w_small, b_small, fox_heads)
        logf_s = small_s[:, :fox_heads]
        logf_pool_t = cache_fox_logf[i].transpose(0, 2, 1)
        new_t = jnp.pad(logf_s.reshape(n_seq, t_new, fox_heads).transpose(0, 2, 1),
                        ((0, 0), (0, 0), (0, LANE - t_new)))
        f_all_t = cumsum_pages(pt_flat, logf_pool_t, new_t, n_seq, n_pages)
        fq_s = f_all_t[:, :, past_len:past_len + t_new].transpose(0, 2, 1)
        assert (2 * fkv_w) and c_fk % (2 * fkv_w) == 0 and c_kc % (2 * nkv_w) == 0
        q_fox_s = _heads_major(sb[:, c_fq:c_fq + fq_w], n_seq, t_new, n_kv)
        o_fox_s = fox_sample(pt_flat, cache_fox_kv[i].reshape(n_pool, page, 2 * fkv_w), q_fox_s, sf,
                             c_fk // (2 * fkv_w), fq_s, f_all_t, n_seq, n_pages, n_kv)
        nsa_pool = cache_nsa_kv[i].reshape(n_pool, page, 4 * nkv_w)
        cmp_s = compress_sample(pt_flat, nsa_pool, sf, c_kc // (2 * nkv_w), pe, w1, w2,
                                n_seq, n_pages, n_groups, nb_s, nbp_s)
        q_nsa_s = _heads_major(sb[:, c_nq:c_nq + nq_w], n_seq, t_new, n_groups)
        gates_s = _heads_major(small_s[:, fox_heads:fox_heads + n_gate], n_seq, t_new, n_groups)
        gates_s = jnp.pad(gates_s, ((0, 0), (0, 0), (0, 0), (0, 13)))
        win_state = state_nsa_win[i].reshape(n_seq, -1, 2 * nkv_w)
        o_nsa_s, new_win = nsa_sample(pt_flat, nsa_pool, q_nsa_s, cmp_s, sf, c_ks // (2 * nkv_w),
                                      c_kw // (2 * nkv_w), win_state, gates_s,
                                      n_seq, n_pages, n_groups, min(SEL_TOP_N, nb_s), past_len)
        hs = tail(hs, _tokens_major(o_fox_s, n_seq, t_new, n_kv), _tokens_major(o_nsa_s, n_seq, t_new, n_groups),
                  p_sample[i].reshape(m_s, -1))
        outs_s.append((
            sf[:, c_fk:c_fk + 2 * fkv_w].reshape(n_seq, t_new, 2, n_kv, HEAD_DIM),
            logf_s.reshape(n_seq, t_new, fox_heads),
            sf[:, c_kc:c_kc + 4 * nkv_w].reshape(n_seq, t_new, 4, n_groups, HEAD_DIM),
            new_win.reshape(state_nsa_win.shape[1:]),
        ))

    y_prompt = rmsnorm(hp, g_final, F32).reshape(x_prompt.shape)
    y_sample = rmsnorm(hs, g_final, F32).reshape(x_sample.shape)
    stack = lambda outs, j: jnp.stack([r[j] for r in outs], axis=0)
    return (y_prompt, y_sample, stack(outs_p, 0), stack(outs_p, 1), stack(outs_p, 2), stack(outs_p, 3),
            stack(outs_s, 0), stack(outs_s, 1), stack(outs_s, 2), stack(outs_s, 3))
```

```python
import functools

import numpy as np
import jax
import jax.numpy as jnp
from jax import lax
from jax.experimental import pallas as pl
from jax.experimental.pallas import tpu as pltpu

HEAD_DIM = 128
REP = 4
BLOCK = 64
SEL_TOP_N = 16
WINDOW = 512
Q_TILE = 128
K_TILE = 512
V_ROWS = 144
ROPE_THETA = 10000.0
RMS_EPS = 1e-6
ATTN_SCALE = HEAD_DIM ** -0.5
LOG2E = 1.4426950408889634
SCALE2 = ATTN_SCALE * LOG2E
NEG_INF = -1e30
REMOVED = -3e38
FORCE_BONUS = 1e4
LANE = 128
MIB = 1024 * 1024

F32 = jnp.float32
BF16 = jnp.bfloat16


def _params(sem, vmem_mib):
    return pltpu.CompilerParams(dimension_semantics=sem, vmem_limit_bytes=vmem_mib * MIB)


def _dot(a, b):
    return jnp.dot(a, b, preferred_element_type=F32)


def _dot_nt(a, b):
    return lax.dot_general(a, b, (((1,), (1,)), ((), ())), preferred_element_type=F32)


def _row_tile(m, cap):
    t = min(m, cap)
    assert m % t == 0
    return t


def _iota(shape, dim):
    return lax.broadcasted_iota(jnp.int32, shape, dim)


def _rmsnorm_kernel(x_ref, g_ref, o_ref):
    x = x_ref[...]
    y = x * lax.rsqrt(jnp.mean(x * x, axis=-1, keepdims=True) + RMS_EPS)
    o_ref[...] = (y * g_ref[...]).astype(o_ref.dtype)


def rmsnorm(x, g, out_dtype):
    m, d = x.shape
    tm = _row_tile(m, 256)
    return pl.pallas_call(
        _rmsnorm_kernel,
        grid=(m // tm,),
        in_specs=[pl.BlockSpec((tm, d), lambda i: (i, 0)), pl.BlockSpec((1, d), lambda i: (0, 0))],
        out_specs=pl.BlockSpec((tm, d), lambda i: (i, 0)),
        out_shape=jax.ShapeDtypeStruct((m, d), out_dtype),
        compiler_params=_params(("parallel",), 32),
        name="rmsnorm",
    )(x, g.reshape(1, d))


def _inproj_kernel(x_ref, w_ref, cos_ref, sin_ref, of_ref, ob_ref, *, tn, rope_ranges):
    acc = _dot(x_ref[...], w_ref[...])
    col0 = pl.program_id(1) * tn
    is_rope = None
    for lo, hi in rope_ranges:
        hit = (col0 >= lo) & (col0 < hi)
        is_rope = hit if is_rope is None else (is_rope | hit)

    @pl.when(is_rope)
    def _():
        cos = cos_ref[...]
        sin = sin_ref[...]
        for h in range(tn // HEAD_DIM):
            sl = slice(h * HEAD_DIM, (h + 1) * HEAD_DIM)
            xh = acc[:, sl]
            r = xh * cos + pltpu.roll(xh, HEAD_DIM // 2, 1) * sin
            of_ref[:, sl] = r
            ob_ref[:, sl] = r.astype(BF16)

    @pl.when(jnp.logical_not(is_rope))
    def _():
        of_ref[...] = acc
        ob_ref[...] = acc.astype(BF16)


def inproj(xn, w_main, cos, sin, tn, rope_ranges):
    m, k = xn.shape
    n = w_main.shape[1]
    tm = _row_tile(m, 1024)
    return pl.pallas_call(
        functools.partial(_inproj_kernel, tn=tn, rope_ranges=rope_ranges),
        grid=(m // tm, n // tn),
        in_specs=[
            pl.BlockSpec((tm, k), lambda i, j: (i, 0)),
            pl.BlockSpec((k, tn), lambda i, j: (0, j)),
            pl.BlockSpec((tm, HEAD_DIM), lambda i, j: (i, 0)),
            pl.BlockSpec((tm, HEAD_DIM), lambda i, j: (i, 0)),
        ],
        out_specs=[pl.BlockSpec((tm, tn), lambda i, j: (i, j)), pl.BlockSpec((tm, tn), lambda i, j: (i, j))],
        out_shape=[jax.ShapeDtypeStruct((m, n), F32), jax.ShapeDtypeStruct((m, n), BF16)],
        compiler_params=_params(("parallel", "arbitrary"), 48),
        name="inproj",
    )(xn, w_main, cos, sin)


def _smallproj_kernel(x_ref, w_ref, b_ref, o_ref, *, n_logf):
    v = _dot(x_ref[...], w_ref[...])
    z = v + b_ref[...]
    logf = -(jnp.maximum(-z, 0.0) + jnp.log1p(jnp.exp(-jnp.abs(z))))
    gate = jax.nn.sigmoid(v)
    lane = _iota(v.shape, 1)
    o_ref[...] = jnp.where(lane < n_logf, logf, gate)


def smallproj(xn, w_small, b_small, n_logf):
    m, k = xn.shape
    tm = _row_tile(m, 1024)
    return pl.pallas_call(
        functools.partial(_smallproj_kernel, n_logf=n_logf),
        grid=(m // tm,),
        in_specs=[
            pl.BlockSpec((tm, k), lambda i: (i, 0)),
            pl.BlockSpec((k, LANE), lambda i: (0, 0)),
            pl.BlockSpec((1, LANE), lambda i: (0, 0)),
        ],
        out_specs=pl.BlockSpec((tm, LANE), lambda i: (i, 0)),
        out_shape=jax.ShapeDtypeStruct((m, LANE), F32),
        compiler_params=_params(("parallel",), 32),
        name="smallproj",
    )(xn, w_small, b_small)


def _outproj_kernel(a_ref, b_ref, wa_ref, wb_ref, h_ref, o_ref):
    o_ref[...] = h_ref[...] + (_dot(a_ref[...], wa_ref[...]) + _dot(b_ref[...], wb_ref[...]))


def outproj(o_fox, o_nsa, w_a, w_b, h):
    m, ka = o_fox.shape
    kb = o_nsa.shape[1]
    n = w_a.shape[1]
    tm, tn = _row_tile(m, 1024), 512
    return pl.pallas_call(
        _outproj_kernel,
        grid=(m // tm, n // tn),
        in_specs=[
            pl.BlockSpec((tm, ka), lambda i, j: (i, 0)),
            pl.BlockSpec((tm, kb), lambda i, j: (i, 0)),
            pl.BlockSpec((ka, tn), lambda i, j: (0, j)),
            pl.BlockSpec((kb, tn), lambda i, j: (0, j)),
            pl.BlockSpec((tm, tn), lambda i, j: (i, j)),
        ],
        out_specs=pl.BlockSpec((tm, tn), lambda i, j: (i, j)),
        out_shape=jax.ShapeDtypeStruct((m, n), F32),
        compiler_params=_params(("parallel", "arbitrary"), 48),
        name="outproj",
    )(o_fox, o_nsa, w_a, w_b, h)


def _gateup_kernel(x_ref, wg_ref, wu_ref, o_ref):
    x = x_ref[...]
    g = _dot(x, wg_ref[...])
    u = _dot(x, wu_ref[...])
    o_ref[...] = (g * jax.nn.sigmoid(g) * u).astype(o_ref.dtype)


def gateup(xn, w_gate, w_up):
    m, k = xn.shape
    n = w_gate.shape[1]
    tm, tn = _row_tile(m, 1024), 256
    return pl.pallas_call(
        _gateup_kernel,
        grid=(m // tm, n // tn),
        in_specs=[
            pl.BlockSpec((tm, k), lambda i, j: (i, 0)),
            pl.BlockSpec((k, tn), lambda i, j: (0, j)),
            pl.BlockSpec((k, tn), lambda i, j: (0, j)),
        ],
        out_specs=pl.BlockSpec((tm, tn), lambda i, j: (i, j)),
        out_shape=jax.ShapeDtypeStruct((m, n), BF16),
        compiler_params=_params(("parallel", "arbitrary"), 48),
        name="gateup",
    )(xn, w_gate, w_up)


def _down_kernel(x_ref, w_ref, h_ref, o_ref):
    o_ref[...] = h_ref[...] + _dot(x_ref[...], w_ref[...])


def downproj(act, w_down, h):
    m, k = act.shape
    n = w_down.shape[1]
    tm, tn = _row_tile(m, 512), 256
    return pl.pallas_call(
        _down_kernel,
        grid=(m // tm, n // tn),
        in_specs=[
            pl.BlockSpec((tm, k), lambda i, j: (i, 0)),
            pl.BlockSpec((k, tn), lambda i, j: (0, j)),
            pl.BlockSpec((tm, tn), lambda i, j: (i, j)),
        ],
        out_specs=pl.BlockSpec((tm, tn), lambda i, j: (i, j)),
        out_shape=jax.ShapeDtypeStruct((m, n), F32),
        compiler_params=_params(("parallel", "arbitrary"), 56),
        name="downproj",
    )(act, w_down, h)


def _ple_kernel(x_ref, wg_ref, p_ref, wp_ref, h_ref, o_ref):
    gate = jax.nn.sigmoid(_dot(x_ref[...], wg_ref[...]))
    o_ref[...] = h_ref[...] + gate * _dot(p_ref[...], wp_ref[...])


def ple(xn, w_gate, p, w_proj, h):
    m, k = xn.shape
    kp = p.shape[1]
    n = w_gate.shape[1]
    tm, tn = _row_tile(m, 1024), 512
    return pl.pallas_call(
        _ple_kernel,
        grid=(m // tm, n // tn),
        in_specs=[
            pl.BlockSpec((tm, k), lambda i, j: (i, 0)),
            pl.BlockSpec((k, tn), lambda i, j: (0, j)),
            pl.BlockSpec((tm, kp), lambda i, j: (i, 0)),
            pl.BlockSpec((kp, tn), lambda i, j: (0, j)),
            pl.BlockSpec((tm, tn), lambda i, j: (i, j)),
        ],
        out_specs=pl.BlockSpec((tm, tn), lambda i, j: (i, j)),
        out_shape=jax.ShapeDtypeStruct((m, n), F32),
        compiler_params=_params(("parallel", "arbitrary"), 48),
        name="ple",
    )(xn, w_gate, p, w_proj, h)


def _split3(x):
    x1 = x.astype(BF16)
    r1 = x - x1.astype(F32)
    x2 = r1.astype(BF16)
    x3 = (r1 - x2.astype(F32)).astype(BF16)
    return x1, x2, x3


def _cumsum_rows_kernel(x_ref, tri_ref, o_ref, p1_ref, p2_ref, p3_ref, carry_ref):
    @pl.when(pl.program_id(0) == 0)
    def _():
        carry_ref[...] = jnp.zeros_like(carry_ref)

    x1, x2, x3 = _split3(x_ref[...])
    tri = tri_ref[...]
    out = (_dot(tri, x1) + _dot(tri, x2) + _dot(tri, x3)) + carry_ref[...]
    o_ref[...] = out
    carry_ref[...] = out[out.shape[0] - 1:, :]
    p1_ref[...], p2_ref[...], p3_ref[...] = _split3(out * (-1.0 / ATTN_SCALE))


def cumsum_rows(x):
    t, h = x.shape
    c = _row_tile(t, 512)
    tri = jnp.asarray(np.tril(np.ones((c, c), np.float32)), BF16)
    row_spec = pl.BlockSpec((c, h), lambda i: (i, 0))
    return pl.pallas_call(
        _cumsum_rows_kernel,
        grid=(t // c,),
        in_specs=[row_spec, pl.BlockSpec((c, c), lambda i: (0, 0))],
        out_specs=[row_spec] * 4,
        out_shape=[jax.ShapeDtypeStruct((t, h), F32)] + [jax.ShapeDtypeStruct((t, h), BF16)] * 3,
        scratch_shapes=[pltpu.VMEM((1, h), F32)],
        compiler_params=_params(("arbitrary",), 32),
        name="cumsum_rows",
    )(x, tri)


def _cumsum_pages_kernel(pt_ref, *refs, n_pages):
    page_refs = refs[:n_pages]
    new_ref, tri_ref, o_ref = refs[n_pages:]
    tri = tri_ref[...]
    h = new_ref.shape[0]
    carry = jnp.zeros((h, 1), F32)
    for p in range(n_pages + 1):
        x = page_refs[p][...] if p < n_pages else new_ref[...]
        x1, x2, x3 = _split3(x)
        out = (_dot(x1, tri) + _dot(x2, tri) + _dot(x3, tri)) + carry
        o_ref[:, p * LANE:(p + 1) * LANE] = out
        carry = out[:, LANE - 1:]


def cumsum_pages(pt_flat, logf_pool_t, new_t, n_seq, n_pages):
    h = logf_pool_t.shape[1]
    tri = jnp.asarray(np.triu(np.ones((LANE, LANE), np.float32)), BF16)
    page_spec = lambda p: pl.BlockSpec((None, h, LANE), lambda b, pt: (pt[b * n_pages + p], 0, 0))
    grid_spec = pltpu.PrefetchScalarGridSpec(
        num_scalar_prefetch=1,
        grid=(n_seq,),
        in_specs=[page_spec(p) for p in range(n_pages)] + [
            pl.BlockSpec((None, h, LANE), lambda b, pt: (b, 0, 0)),
            pl.BlockSpec((LANE, LANE), lambda b, pt: (0, 0)),
        ],
        out_specs=pl.BlockSpec((None, h, (n_pages + 1) * LANE), lambda b, pt: (b, 0, 0)),
    )
    return pl.pallas_call(
        functools.partial(_cumsum_pages_kernel, n_pages=n_pages),
        grid_spec=grid_spec,
        out_shape=jax.ShapeDtypeStruct((n_seq, h, (n_pages + 1) * LANE), F32),
        compiler_params=_params(("arbitrary",), 32),
        name="cumsum_pages",
    )(pt_flat, *([logf_pool_t] * n_pages), new_t, tri)


def _silu(x):
    return x * jax.nn.sigmoid(x)


def _compress_prompt_kernel(cb_ref, x_ref, pe_ref, w1_ref, w2_ref, o_ref, *, nb):
    hidden = w1_ref.shape[2]

    def body(l, acc):
        xl = x_ref[pl.ds(l, nb, stride=BLOCK), :] + pe_ref[pl.ds(l, 1), :]
        return acc + _dot(xl.astype(BF16), w1_ref[l])

    acc = lax.fori_loop(0, BLOCK, body, jnp.zeros((nb, hidden), F32))
    out = _dot(_silu(acc).astype(BF16), w2_ref[...])
    o_ref[...] = jnp.zeros_like(o_ref)
    o_ref[0:nb, :] = out


def compress_prompt(proj_f32, col_blocks, pe, w1, w2, n_groups, nbp):
    t = proj_f32.shape[0]
    nb = t // BLOCK
    hidden = w1.shape[-1]
    cb = jnp.asarray(col_blocks, jnp.int32)
    grid_spec = pltpu.PrefetchScalarGridSpec(
        num_scalar_prefetch=1,
        grid=(2, n_groups),
        in_specs=[
            pl.BlockSpec((t, HEAD_DIM), lambda kv, g, cb: (0, cb[kv] + g)),
            pl.BlockSpec((None, BLOCK, HEAD_DIM), lambda kv, g, cb: (kv, 0, 0)),
            pl.BlockSpec((None, BLOCK, HEAD_DIM, hidden), lambda kv, g, cb: (kv, 0, 0, 0)),
            pl.BlockSpec((None, hidden, HEAD_DIM), lambda kv, g, cb: (kv, 0, 0)),
        ],
        out_specs=pl.BlockSpec((None, None, nbp, HEAD_DIM), lambda kv, g, cb: (kv, g, 0, 0)),
    )
    return pl.pallas_call(
        functools.partial(_compress_prompt_kernel, nb=nb),
        grid_spec=grid_spec,
        out_shape=jax.ShapeDtypeStruct((2, n_groups, nbp, HEAD_DIM), F32),
        compiler_params=_params(("arbitrary", "arbitrary"), 40),
        name="compress_prompt",
    )(cb, proj_f32, pe, w1, w2)


def _compress_sample_kernel(pt_ref, *refs, n_pages, n_groups, nb):
    pages = refs[:n_pages]
    new_ref, pe_ref, w1_ref, w2_ref, o_ref, newblk_ref = refs[n_pages:]
    t_new = new_ref.shape[0]
    hidden = w1_ref.shape[2]
    rows_pad = o_ref.shape[0]
    n_rows = (2 * n_pages + 1) * n_groups

    newblk_ref[...] = jnp.zeros_like(newblk_ref)
    for l in range(t_new):
        for g in range(n_groups):
            newblk_ref[l, g:g + 1, :] = new_ref[l:l + 1, g * HEAD_DIM:(g + 1) * HEAD_DIM]

    def gather(l):
        rr = []
        for pg in pages:
            rr.append(pg[l])
            rr.append(pg[l + BLOCK])
        rr.append(newblk_ref[l])
        if rows_pad > n_rows:
            rr.append(jnp.zeros((rows_pad - n_rows, HEAD_DIM), F32))
        return jnp.concatenate(rr, axis=0) + pe_ref[l:l + 1, :]

    acc = jnp.zeros((rows_pad, hidden), F32)
    for l2 in range(BLOCK // 2):
        x = jnp.concatenate([gather(2 * l2), gather(2 * l2 + 1)], axis=1).astype(BF16)
        acc = acc + _dot(x, w1_ref[l2])
    out = _dot(_silu(acc).astype(BF16), w2_ref[...])
    o_ref[...] = jnp.where(_iota(out.shape, 0) < nb * n_groups, out, 0.0)


def compress_sample(pt_flat, nsa_cache, layer, new_rows, new_col_block, pe, w1_pairs, w2, n_seq, n_pages, nb):
    page, _, n_groups, _ = nsa_cache.shape[2:]
    assert page == 2 * BLOCK
    gw = n_groups * HEAD_DIM
    hidden = w1_pairs.shape[-1]
    t_new = new_rows.shape[0] // n_seq
    rows_pad = -(-((2 * n_pages + 1) * n_groups) // 8) * 8
    spec = lambda p: pl.BlockSpec((None, None, page, None, n_groups, HEAD_DIM),
                                  lambda kv, b, pt: (layer, pt[b * n_pages + p], 0, kv, 0, 0))
    grid_spec = pltpu.PrefetchScalarGridSpec(
        num_scalar_prefetch=1,
        grid=(2, n_seq),
        in_specs=[spec(p) for p in range(n_pages)] + [
            pl.BlockSpec((t_new, gw), lambda kv, b, pt: (b, new_col_block + kv)),
            pl.BlockSpec((None, BLOCK, HEAD_DIM), lambda kv, b, pt: (kv, 0, 0)),
            pl.BlockSpec((None, BLOCK // 2, 2 * HEAD_DIM, hidden), lambda kv, b, pt: (kv, 0, 0, 0)),
            pl.BlockSpec((None, hidden, HEAD_DIM), lambda kv, b, pt: (kv, 0, 0)),
        ],
        out_specs=pl.BlockSpec((None, None, rows_pad, HEAD_DIM), lambda kv, b, pt: (b, kv, 0, 0)),
        scratch_shapes=[pltpu.VMEM((BLOCK, n_groups, HEAD_DIM), F32)],
    )
    return pl.pallas_call(
        functools.partial(_compress_sample_kernel, n_pages=n_pages, n_groups=n_groups, nb=nb),
        grid_spec=grid_spec,
        out_shape=jax.ShapeDtypeStruct((n_seq, 2, rows_pad, HEAD_DIM), F32),
        compiler_params=_params(("arbitrary", "arbitrary"), 40),
        name="compress_sample",
    )(pt_flat, *([nsa_cache] * n_pages), new_rows, pe, w1_pairs, w2)


def _softmax_rows(s):
    m = jnp.max(s, axis=-1, keepdims=True)
    e = jnp.exp(s - m)
    return e / jnp.sum(e, axis=-1, keepdims=True)


def _softmax_cols(s):
    m = jnp.max(s, axis=0, keepdims=True)
    e = jnp.exp(s - m)
    return e / jnp.sum(e, axis=0, keepdims=True)


def _top_n_mask_t(score_t, blk_t, n_sel, n_blocks_pad):
    sel = jnp.zeros(score_t.shape, F32)
    for _ in range(n_sel):
        mx = jnp.max(score_t, axis=0, keepdims=True)
        idx = jnp.min(jnp.where(score_t == mx, blk_t, n_blocks_pad), axis=0, keepdims=True)
        hit = blk_t == idx
        sel = jnp.where(hit, 1.0, sel)
        score_t = jnp.where(hit, REMOVED, score_t)
    return sel


def _selection_scores_t(imp_t, blk_t, qpos_t):
    cur = qpos_t // BLOCK
    forced = (blk_t == 0) | (blk_t == cur) | (blk_t == cur - 1)
    avail = blk_t * BLOCK <= qpos_t
    return jnp.where(avail, imp_t + jnp.where(forced, FORCE_BONUS, 0.0), NEG_INF)


def _online_tile_t(x, vt_tile, m, acc_ref, shift):
    mx = jnp.max(x, axis=0, keepdims=True) * SCALE2
    if shift is not None:
        mx = mx + shift
    m_new = jnp.maximum(m, mx)
    off = m_new if shift is None else m_new - shift
    p = jnp.exp2(x * SCALE2 - off)
    acc_ref[...] = jnp.exp2(m - m_new) * acc_ref[...] + _dot(vt_tile, p.astype(BF16))
    return m_new


def _store_heads(o_ref, o_t):
    for r in range(REP):
        o_ref[:, r * HEAD_DIM:(r + 1) * HEAD_DIM] = o_t[:, r * Q_TILE:(r + 1) * Q_TILE].T.astype(o_ref.dtype)


def _fox_prompt_kernel(qt_ref, aug_ref, ka_ref, vt_ref, fq_ref, o_ref, acc_ref):
    q0 = pl.program_id(1) * Q_TILE
    cols = REP * Q_TILE
    qa = jnp.concatenate([qt_ref[...], aug_ref[...]], axis=0)
    fq2 = fq_ref[...] * LOG2E
    acc_ref[...] = jnp.zeros_like(acc_ref)
    n_full = q0 // K_TILE

    def scores(j):
        k0 = pl.multiple_of(j * K_TILE, K_TILE)
        return _dot(ka_ref[pl.ds(k0, K_TILE), :], qa)

    def update(j, x, m):
        k0 = pl.multiple_of(j * K_TILE, K_TILE)
        return _online_tile_t(x, vt_ref[:, pl.ds(k0, K_TILE)], m, acc_ref, fq2)

    def body(j, carry):
        m, x = carry
        x_next = scores(j)
        return update(j - 1, x, m), x_next

    m, x = lax.fori_loop(1, n_full + 1, body, (jnp.full((1, cols), NEG_INF, F32), scores(0)))
    kpos = n_full * K_TILE + _iota((K_TILE, cols), 0)
    qpos = q0 + (_iota((K_TILE, cols), 1) & (Q_TILE - 1))
    update(n_full, jnp.where(kpos <= qpos, x, NEG_INF), m)
    acc = acc_ref[...]
    _store_heads(o_ref, acc[0:HEAD_DIM] / acc[HEAD_DIM:HEAD_DIM + 1])


def fox_prompt(q_t, aug, k_aug, v_t, fq_lanes):
    n_kv, _, t = v_t.shape
    cols = REP * Q_TILE
    return pl.pallas_call(
        _fox_prompt_kernel,
        grid=(n_kv, t // Q_TILE),
        in_specs=[
            pl.BlockSpec((None, HEAD_DIM, cols), lambda g, i: (g, 0, i)),
            pl.BlockSpec((HEAD_DIM, cols), lambda g, i: (0, 0)),
            pl.BlockSpec((None, t, 2 * HEAD_DIM), lambda g, i: (g, 0, 0)),
            pl.BlockSpec((None, V_ROWS, t), lambda g, i: (g, 0, 0)),
            pl.BlockSpec((None, None, 1, cols), lambda g, i: (g, i, 0, 0)),
        ],
        out_specs=pl.BlockSpec((Q_TILE, REP * HEAD_DIM), lambda g, i: (i, g)),
        out_shape=jax.ShapeDtypeStruct((t, n_kv * REP * HEAD_DIM), BF16),
        scratch_shapes=[pltpu.VMEM((V_ROWS, cols), F32)],
        compiler_params=_params(("arbitrary", "arbitrary"), 40),
        name="fox_prompt",
    )(q_t, aug, k_aug, v_t, fq_lanes)


def _nsa_prompt_kernel(qt_ref, ks_ref, vst_ref, kw_ref, vwt_ref, ck_ref, cvt_ref, et_ref, gate_ref, o_ref,
                       acc_ref, *, n_sel):
    q0 = pl.program_id(1) * Q_TILE
    cols = REP * Q_TILE
    nbp = ck_ref.shape[0]
    qt = qt_ref[...]

    x_c = _dot(ck_ref[...].astype(BF16), qt) * ATTN_SCALE
    blk = _iota((nbp, cols), 0)
    qpos = q0 + (_iota((nbp, cols), 1) & (Q_TILE - 1))
    x_c = jnp.where((blk + 1) * BLOCK - 1 <= qpos, x_c, NEG_INF)
    p_c = _softmax_cols(x_c) * (qpos >= BLOCK - 1).astype(F32)
    o_cmp = _dot(cvt_ref[...], p_c.astype(BF16))

    imp_t = p_c[:, 0:Q_TILE]
    for r in range(1, REP):
        imp_t = imp_t + p_c[:, r * Q_TILE:(r + 1) * Q_TILE]
    blk_t = _iota((nbp, Q_TILE), 0)
    qpos_t = q0 + _iota((nbp, Q_TILE), 1)
    sel_t = _top_n_mask_t(_selection_scores_t(imp_t, blk_t, qpos_t), blk_t, n_sel, nbp).astype(BF16)

    acc_ref[...] = jnp.zeros_like(acc_ref)
    n_full = q0 // K_TILE

    def tile(j, m, masked):
        k0 = pl.multiple_of(j * K_TILE, K_TILE)
        x = _dot(ks_ref[pl.ds(k0, K_TILE), :], qt)
        picked = _dot(et_ref[pl.ds(k0, K_TILE), :], sel_t)
        bias = (picked - 1.0) * -NEG_INF
        if masked:
            kpos = k0 + _iota((K_TILE, Q_TILE), 0)
            bias = jnp.where(kpos <= q0 + _iota((K_TILE, Q_TILE), 1), bias, NEG_INF)
        x = x + jnp.concatenate([bias] * REP, axis=1)
        return _online_tile_t(x, vst_ref[:, pl.ds(k0, K_TILE)], m, acc_ref, None)

    m = lax.fori_loop(0, n_full, lambda j, m: tile(j, m, False), jnp.full((1, cols), NEG_INF, F32))
    tile(n_full, m, True)
    acc = acc_ref[...]
    o_slc = acc[0:HEAD_DIM] / acc[HEAD_DIM:HEAD_DIM + 1]

    wlen = WINDOW + Q_TILE
    w0 = pl.multiple_of(jnp.maximum(q0 - WINDOW, 0), Q_TILE)
    x_w = _dot(kw_ref[pl.ds(w0, wlen), :], qt) * ATTN_SCALE
    dist = q0 + (_iota((wlen, cols), 1) & (Q_TILE - 1)) - (w0 + _iota((wlen, cols), 0))
    x_w = jnp.where((dist >= 0) & (dist < WINDOW), x_w, NEG_INF)
    o_win = _dot(vwt_ref[0:HEAD_DIM, pl.ds(w0, wlen)], _softmax_cols(x_w).astype(BF16))

    gates = gate_ref[...]
    _store_heads(o_ref, gates[0:1] * o_cmp + gates[1:2] * o_slc + gates[2:3] * o_win)


def nsa_prompt(q_t, proj_bf16, ks_blk0, kw_blk0, vs_t, vw_t, cmp_k, cmp_vt, expand_t, gate_lanes, n_sel):
    n_groups, _, t = vs_t.shape
    nbp = cmp_k.shape[1]
    cols = REP * Q_TILE
    return pl.pallas_call(
        functools.partial(_nsa_prompt_kernel, n_sel=n_sel),
        grid=(n_groups, t // Q_TILE),
        in_specs=[
            pl.BlockSpec((None, HEAD_DIM, cols), lambda g, i: (g, 0, i)),
            pl.BlockSpec((t, HEAD_DIM), lambda g, i: (0, ks_blk0 + g)),
            pl.BlockSpec((None, V_ROWS, t), lambda g, i: (g, 0, 0)),
            pl.BlockSpec((t, HEAD_DIM), lambda g, i: (0, kw_blk0 + g)),
            pl.BlockSpec((None, V_ROWS, t), lambda g, i: (g, 0, 0)),
            pl.BlockSpec((None, nbp, HEAD_DIM), lambda g, i: (g, 0, 0)),
            pl.BlockSpec((None, HEAD_DIM, nbp), lambda g, i: (g, 0, 0)),
            pl.BlockSpec((t, nbp), lambda g, i: (0, 0)),
            pl.BlockSpec((None, None, 8, cols), lambda g, i: (g, i, 0, 0)),
        ],
        out_specs=pl.BlockSpec((Q_TILE, REP * HEAD_DIM), lambda g, i: (i, g)),
        out_shape=jax.ShapeDtypeStruct((t, n_groups * REP * HEAD_DIM), BF16),
        scratch_shapes=[pltpu.VMEM((V_ROWS, cols), F32)],
        compiler_params=_params(("arbitrary", "arbitrary"), 48),
        name="nsa_prompt",
    )(q_t, proj_bf16, vs_t, proj_bf16, vw_t, cmp_k, cmp_vt, expand_t, gate_lanes)


def _pad_rows(x, n):
    return jnp.concatenate([x, jnp.zeros((n - x.shape[0], x.shape[1]), x.dtype)], axis=0)


def _copy_rows(block_refs, rows_ref):
    views = []
    for p, r in enumerate(block_refs):
        rows_ref[p] = r[...].reshape(rows_ref.shape[1:])
        views.append(rows_ref.at[p])
    return views


def _fox_sample_kernel(pt_ref, *refs, n_pages, n_kv):
    q_ref, new_ref, fq_ref, fk_ref, o_ref, rows_ref = refs[n_pages:]
    page = refs[0].shape[0]
    stride = 2 * n_kv
    pages = _copy_rows(refs[:n_pages], rows_ref)
    t_new = new_ref.shape[0]
    rows = REP * t_new
    kvw = n_kv * HEAD_DIM
    new_mask = _iota((t_new, page), 1) <= _iota((t_new, page), 0)
    for g in range(n_kv):
        q4 = q_ref[g]
        ks = slice(g * HEAD_DIM, (g + 1) * HEAD_DIM)
        vs = slice(kvw + g * HEAD_DIM, kvw + (g + 1) * HEAD_DIM)
        chunks = []
        for p in range(n_pages + 1):
            if p < n_pages:
                k_p = pages[p][pl.ds(g, page, stride=stride), :].astype(BF16)
            else:
                k_p = _pad_rows(new_ref[:, ks], page).astype(BF16)
            s = _dot_nt(q4, k_p) * ATTN_SCALE
            fk = fk_ref[g * REP:(g + 1) * REP, p * page:(p + 1) * page]
            s = s.reshape(REP, t_new, page) - fk[:, None, :]
            if p == n_pages:
                s = jnp.where(new_mask[None], s, NEG_INF)
            chunks.append(s.reshape(rows, page))
        fq = jnp.concatenate([fq_ref[:, g * REP + r:g * REP + r + 1] for r in range(REP)], axis=0)
        prob = _softmax_rows(jnp.concatenate(chunks, axis=1) + fq).astype(BF16)
        o = jnp.zeros((rows, HEAD_DIM), F32)
        for p in range(n_pages + 1):
            if p < n_pages:
                v_p = pages[p][pl.ds(n_kv + g, page, stride=stride), :].astype(BF16)
            else:
                v_p = _pad_rows(new_ref[:, vs], page).astype(BF16)
            o = o + _dot(prob[:, p * page:(p + 1) * page], v_p)
        o_ref[g] = o.astype(o_ref.dtype)


def fox_sample(pt_flat, fox_cache, layer, q_s, new_rows, new_col_block, fq_s, fk_s, n_seq, n_pages):
    page, _, n_kv, _ = fox_cache.shape[2:]
    t_new = new_rows.shape[0] // n_seq
    rows = REP * t_new
    n_heads = fq_s.shape[2]
    assert page == LANE
    spec = lambda p: pl.BlockSpec((None, None, page, 2, n_kv, HEAD_DIM),
                                  lambda b, pt: (layer, pt[b * n_pages + p], 0, 0, 0, 0))
    grid_spec = pltpu.PrefetchScalarGridSpec(
        num_scalar_prefetch=1,
        grid=(n_seq,),
        in_specs=[spec(p) for p in range(n_pages)] + [
            pl.BlockSpec((None, n_kv, rows, HEAD_DIM), lambda b, pt: (b, 0, 0, 0)),
            pl.BlockSpec((t_new, 2 * n_kv * HEAD_DIM), lambda b, pt: (b, new_col_block)),
            pl.BlockSpec((None, t_new, n_heads), lambda b, pt: (b, 0, 0)),
            pl.BlockSpec((None, n_heads, (n_pages + 1) * page), lambda b, pt: (b, 0, 0)),
        ],
        out_specs=pl.BlockSpec((None, n_kv, rows, HEAD_DIM), lambda b, pt: (b, 0, 0, 0)),
        scratch_shapes=[pltpu.VMEM((n_pages, page * 2 * n_kv, HEAD_DIM), F32)],
    )
    return pl.pallas_call(
        functools.partial(_fox_sample_kernel, n_pages=n_pages, n_kv=n_kv),
        grid_spec=grid_spec,
        out_shape=jax.ShapeDtypeStruct((n_seq, n_kv, rows, HEAD_DIM), BF16),
        compiler_params=_params(("arbitrary",), 52),
        name="fox_sample",
    )(pt_flat, *([fox_cache] * n_pages), q_s, new_rows, fq_s, fk_s)


def _nsa_sample_kernel(pt_ref, *refs, n_pages, n_groups, n_sel, past_len):
    (q_ref, cmp_ref, new_slc_ref, new_win_ref, win_ref, gate_ref, o_ref,
     krows_ref, vrows_ref, wrows_ref) = refs[2 * n_pages:]
    page = refs[0].shape[0]
    kpages = _copy_rows(refs[:n_pages], krows_ref)
    vpages = _copy_rows(refs[n_pages:2 * n_pages], vrows_ref)
    t_new = new_slc_ref.shape[0]
    rows = REP * t_new
    gw = n_groups * HEAD_DIM
    nbp = cmp_ref.shape[2]
    n_buf = win_ref.shape[0]
    win_rows = _copy_rows([win_ref], wrows_ref)[0]
    blocks_per_page = page // BLOCK

    qpos = past_len + _iota((t_new, nbp), 0)
    c_mask = (_iota((t_new, nbp), 1) + 1) * BLOCK - 1 <= qpos
    any_vis = (qpos >= BLOCK - 1).astype(F32)
    o_cmp, imps = [], []
    for g in range(n_groups):
        s_c = _dot_nt(q_ref[g], cmp_ref[0, g].astype(BF16)) * ATTN_SCALE
        s_c = jnp.where(c_mask[None], s_c.reshape(REP, t_new, nbp), NEG_INF)
        p_c = _softmax_rows(s_c) * any_vis[None]
        o_cmp.append(_dot(p_c.reshape(rows, nbp).astype(BF16), cmp_ref[1, g].astype(BF16)))
        imp = p_c[0]
        for r in range(1, REP):
            imp = imp + p_c[r]
        imps.append(imp)

    imp_all = _pad_rows(jnp.concatenate(imps, axis=0), LANE)
    blk_t = _iota((nbp, LANE), 0)
    qpos_t = past_len + _iota((nbp, LANE), 1) % t_new
    sel_t = _top_n_mask_t(_selection_scores_t(imp_all.T, blk_t, qpos_t), blk_t, n_sel, nbp)
    sel_all = sel_t.T

    lane_p = _iota((t_new, page), 1)
    row_p = _iota((t_new, page), 0)
    lane_blk = lane_p // BLOCK
    widx = _iota((t_new, n_buf + page), 1)
    wdist = n_buf + _iota((t_new, n_buf + page), 0) - widx
    w_mask = (wdist >= 0) & (wdist < WINDOW) & (widx < n_buf + t_new)
    gates = gate_ref[...]

    for g in range(n_groups):
        q4 = q_ref[g]
        gs = slice(g * HEAD_DIM, (g + 1) * HEAD_DIM)
        vs = slice(gw + g * HEAD_DIM, gw + (g + 1) * HEAD_DIM)
        sel_g = sel_all[g * t_new:(g + 1) * t_new, :]

        chunks = []
        for p in range(n_pages + 1):
            if p < n_pages:
                k_p = kpages[p][pl.ds(g, page, stride=n_groups), :].astype(BF16)
            else:
                k_p = _pad_rows(new_slc_ref[:, gs], page).astype(BF16)
            s = _dot_nt(q4, k_p) * ATTN_SCALE
            picked = jnp.zeros((t_new, page), F32)
            for c in range(blocks_per_page):
                b_idx = p * blocks_per_page + c
                picked = jnp.where(lane_blk == c, sel_g[:, b_idx:b_idx + 1], picked)
            ok = picked > 0.5
            if p == n_pages:
                ok = ok & (lane_p <= row_p)
            chunks.append(jnp.where(ok[None], s.reshape(REP, t_new, page), NEG_INF).reshape(rows, page))
        prob = _softmax_rows(jnp.concatenate(chunks, axis=1)).astype(BF16)
        o_slc = jnp.zeros((rows, HEAD_DIM), F32)
        for p in range(n_pages + 1):
            if p < n_pages:
                v_p = vpages[p][pl.ds(g, page, stride=n_groups), :].astype(BF16)
            else:
                v_p = _pad_rows(new_slc_ref[:, vs], page).astype(BF16)
            o_slc = o_slc + _dot(prob[:, p * page:(p + 1) * page], v_p)

        kw = jnp.concatenate([win_rows[pl.ds(g, n_buf, stride=2 * n_groups), :],
                              _pad_rows(new_win_ref[:, gs], page)], axis=0).astype(BF16)
        vw = jnp.concatenate([win_rows[pl.ds(n_groups + g, n_buf, stride=2 * n_groups), :],
                              _pad_rows(new_win_ref[:, vs], page)], axis=0).astype(BF16)
        s_w = _dot_nt(q4, kw) * ATTN_SCALE
        s_w = jnp.where(w_mask[None], s_w.reshape(REP, t_new, n_buf + page), NEG_INF).reshape(rows, n_buf + page)
        o_win = _dot(_softmax_rows(s_w).astype(BF16), vw)

        gt = gates[g]
        o_ref[g] = (gt[:, 0:1] * o_cmp[g] + gt[:, 1:2] * o_slc + gt[:, 2:3] * o_win).astype(o_ref.dtype)


def nsa_sample(pt_flat, nsa_cache, win_state, layer, q_s, cmp_s, new_rows, slc_col_block, win_col_block, gates_s,
               n_seq, n_pages, n_sel, past_len):
    page, _, n_groups, _ = nsa_cache.shape[2:]
    gw = n_groups * HEAD_DIM
    t_new = new_rows.shape[0] // n_seq
    rows = REP * t_new
    nbp = cmp_s.shape[3]
    n_buf = win_state.shape[2]
    spec = lambda p, slot: pl.BlockSpec((None, None, page, None, n_groups, HEAD_DIM),
                                        lambda b, pt: (layer, pt[b * n_pages + p], 0, slot, 0, 0))
    grid_spec = pltpu.PrefetchScalarGridSpec(
        num_scalar_prefetch=1,
        grid=(n_seq,),
        in_specs=[spec(p, 2) for p in range(n_pages)] + [spec(p, 3) for p in range(n_pages)] + [
            pl.BlockSpec((None, n_groups, rows, HEAD_DIM), lambda b, pt: (b, 0, 0, 0)),
            pl.BlockSpec((None, 2, n_groups, nbp, HEAD_DIM), lambda b, pt: (b, 0, 0, 0, 0)),
            pl.BlockSpec((t_new, 2 * gw), lambda b, pt: (b, slc_col_block)),
            pl.BlockSpec((t_new, 2 * gw), lambda b, pt: (b, win_col_block)),
            pl.BlockSpec((None, None, n_buf, 2, n_groups, HEAD_DIM), lambda b, pt: (layer, b, 0, 0, 0, 0)),
            pl.BlockSpec((None, n_groups, rows, 16), lambda b, pt: (b, 0, 0, 0)),
        ],
        out_specs=pl.BlockSpec((None, n_groups, rows, HEAD_DIM), lambda b, pt: (b, 0, 0, 0)),
        scratch_shapes=[pltpu.VMEM((n_pages, page * n_groups, HEAD_DIM), F32),
                        pltpu.VMEM((n_pages, page * n_groups, HEAD_DIM), F32),
                        pltpu.VMEM((1, n_buf * 2 * n_groups, HEAD_DIM), F32)],
    )
    return pl.pallas_call(
        functools.partial(_nsa_sample_kernel, n_pages=n_pages, n_groups=n_groups, n_sel=n_sel, past_len=past_len),
        grid_spec=grid_spec,
        out_shape=jax.ShapeDtypeStruct((n_seq, n_groups, rows, HEAD_DIM), BF16),
        compiler_params=_params(("arbitrary",), 56),
        name="nsa_sample",
    )(pt_flat, *([nsa_cache] * (2 * n_pages)), q_s, cmp_s, new_rows, new_rows, win_state, gates_s)


def _win_shift_kernel(win_ref, new_ref, o_ref):
    n_buf, _, n_groups, _ = win_ref.shape
    t_new = new_ref.shape[0]
    gw = n_groups * HEAD_DIM
    o_ref[0:n_buf - t_new] = win_ref[t_new:n_buf]
    for s in range(2):
        for g in range(n_groups):
            c0 = s * gw + g * HEAD_DIM
            o_ref[n_buf - t_new:n_buf, s, g, :] = new_ref[:, c0:c0 + HEAD_DIM]


def win_shift(win_state, layer, new_rows, win_col_block):
    _, n_seq, n_buf, _, n_groups, _ = win_state.shape
    t_new = new_rows.shape[0] // n_seq
    assert n_buf > t_new
    blk = (None, n_buf, 2, n_groups, HEAD_DIM)
    return pl.pallas_call(
        _win_shift_kernel,
        grid=(n_seq,),
        in_specs=[
            pl.BlockSpec((None,) + blk, lambda b: (layer, b, 0, 0, 0, 0)),
            pl.BlockSpec((t_new, 2 * n_groups * HEAD_DIM), lambda b: (b, win_col_block)),
        ],
        out_specs=pl.BlockSpec(blk, lambda b: (b, 0, 0, 0, 0)),
        out_shape=jax.ShapeDtypeStruct(win_state.shape[1:], F32),
        compiler_params=_params(("parallel",), 32),
        name="win_shift",
    )(win_state, new_rows)


def _rope_tables(pos):
    half = HEAD_DIM // 2
    inv_freq = ROPE_THETA ** (-jnp.arange(half, dtype=F32) / half)
    ang = pos.astype(F32)[:, None] * inv_freq[None, :]
    cos, sin = jnp.cos(ang), jnp.sin(ang)
    return jnp.concatenate([cos, cos], axis=-1), jnp.concatenate([-sin, sin], axis=-1)


def _heads_major(x, n_seq, t_new, n_groups):
    w = x.shape[1] // (n_groups * REP)
    return x.reshape(n_seq, t_new, n_groups, REP, w).transpose(0, 2, 3, 1, 4).reshape(n_seq, n_groups, REP * t_new, w)


def _tokens_major(x, n_seq, t_new, n_groups):
    w = x.shape[-1]
    return x.reshape(n_seq, n_groups, REP, t_new, w).transpose(0, 3, 1, 2, 4).reshape(n_seq * t_new, n_groups * REP * w)


def _queries_t(q, n_groups):
    t = q.shape[0]
    nq = t // Q_TILE
    q = q.reshape(nq, Q_TILE, n_groups, REP, HEAD_DIM).transpose(2, 4, 0, 3, 1)
    return q.reshape(n_groups, HEAD_DIM, nq * REP * Q_TILE)


def _values_t(v, n_groups):
    t = v.shape[0]
    vt = v.T.reshape(n_groups, HEAD_DIM, t)
    return jnp.concatenate([vt, jnp.ones((n_groups, V_ROWS - HEAD_DIM, t), v.dtype)], axis=1)


def _head_lanes(x, n_groups, w):
    t = x.shape[0]
    nq = t // Q_TILE
    x = x.reshape(nq, Q_TILE, n_groups, REP, w).transpose(2, 0, 4, 3, 1)
    return x.reshape(n_groups, nq, w, REP * Q_TILE)


def kernel(x_prompt, x_sample, cache_fox_kv, cache_fox_logf, cache_nsa_kv, state_nsa_win, page_table, p_prompt, p_sample, g_mix, w_in, b_fgate, cmp_pe_k, cmp_w1_k, cmp_w2_k, cmp_pe_v, cmp_w1_v, cmp_w2_v, w_out, g_ffn, w_gate, w_up, w_down, g_ple, w_ple_gate, w_ple_proj, g_final):
    depth = w_in.shape[0]
    _, seq, d_model = x_prompt.shape
    n_seq, t_new, _ = x_sample.shape
    page = cache_fox_kv.shape[2]
    n_pages = page_table.shape[1]
    past_len = n_pages * page
    fox_heads = b_fgate.shape[1]
    n_kv = cache_fox_kv.shape[4]
    n_groups = cache_nsa_kv.shape[4]
    nsa_heads = n_groups * REP
    assert fox_heads == n_kv * REP and w_out.shape[1] == (fox_heads + nsa_heads) * HEAD_DIM
    fq_w, nq_w = fox_heads * HEAD_DIM, nsa_heads * HEAD_DIM
    fkv_w, nkv_w = n_kv * HEAD_DIM, n_groups * HEAD_DIM
    n_gate = 3 * nsa_heads
    assert fox_heads + n_gate <= LANE and seq % K_TILE == 0 and seq >= WINDOW + Q_TILE and 3 * REP <= HEAD_DIM

    sizes = [fq_w, fkv_w, fkv_w, fox_heads, nq_w] + [nkv_w] * 6 + [n_gate]
    off = np.concatenate([[0], np.cumsum(sizes)]).astype(int)
    o_fq, o_fk, o_fv, o_fl, o_nq, o_kc, o_vc, o_ks, o_vs, o_kw, o_vw, o_g = [int(v) for v in off[:-1]]
    c_fq, c_nq = 0, fq_w
    c_fk = c_nq + nq_w
    c_fv = c_fk + fkv_w
    c_kc = c_fv + fkv_w
    c_vc, c_ks, c_vs, c_kw, c_vw = (c_kc + nkv_w * k for k in range(1, 6))
    n_main = c_vw + nkv_w
    tn = min(512, fkv_w, nkv_w)
    rope_ranges = ((c_nq, c_nq + nq_w), (c_kc, c_kc + nkv_w), (c_ks, c_ks + nkv_w), (c_kw, c_kw + nkv_w))

    nb_p = seq // BLOCK
    nbp_p = max(LANE, nb_p)
    nb_s = -(-(past_len + t_new) // BLOCK)
    nbp_s = LANE
    assert nb_s == 2 * n_pages + 1 and nb_s <= LANE and nbp_p % LANE == 0
    n_win_p = min(WINDOW, seq)
    nq_tiles = seq // Q_TILE

    cos_p, sin_p = _rope_tables(jnp.arange(seq))
    cos_s, sin_s = _rope_tables(jnp.tile(past_len + jnp.arange(t_new), n_seq))
    expand_t = jnp.asarray((np.arange(seq)[:, None] // BLOCK) == np.arange(nbp_p)[None, :], BF16)
    aug_np = np.zeros((HEAD_DIM, REP * Q_TILE), np.float32)
    for r in range(REP):
        aug_np[3 * r:3 * r + 3, r * Q_TILE:(r + 1) * Q_TILE] = 1.0
    aug = jnp.asarray(aug_np, BF16)
    pt_flat = page_table.reshape(-1).astype(jnp.int32)

    hp = x_prompt.reshape(seq, d_model)
    hs = x_sample.reshape(n_seq * t_new, d_model)
    outs_p, outs_s = [], []
    for i in range(depth):
        wi = w_in[i]
        w_main = jnp.concatenate(
            [wi[:, o_fq:o_fq + fq_w], wi[:, o_nq:o_nq + nq_w], wi[:, o_fk:o_fl], wi[:, o_kc:o_g]], axis=1).astype(BF16)
        w_small = jnp.concatenate(
            [wi[:, o_fl:o_fl + fox_heads], wi[:, o_g:o_g + n_gate],
             jnp.zeros((d_model, LANE - fox_heads - n_gate), F32)], axis=1).astype(BF16)
        b_small = jnp.concatenate([b_fgate[i], jnp.zeros((LANE - fox_heads,), F32)]).reshape(1, LANE)
        w_o = w_out[i].astype(BF16)
        w_oa, w_ob = w_o[:fq_w], w_o[fq_w:]
        w_g, w_u, w_d = w_gate[i].astype(BF16), w_up[i].astype(BF16), w_down[i].astype(BF16)
        w_pg, w_pp = w_ple_gate[i].astype(BF16), w_ple_proj[i].astype(BF16)
        pe = jnp.stack([cmp_pe_k[i], cmp_pe_v[i]])
        hidden = cmp_w1_k.shape[-1]
        w1 = jnp.stack([cmp_w1_k[i], cmp_w1_v[i]]).reshape(2, BLOCK, HEAD_DIM, hidden).astype(BF16)
        w1_pairs = w1.reshape(2, BLOCK // 2, 2 * HEAD_DIM, hidden)
        w2 = jnp.stack([cmp_w2_k[i], cmp_w2_v[i]]).astype(BF16)

        def tail(h, o_fox, o_nsa, p):
            h1 = outproj(o_fox, o_nsa, w_oa, w_ob, h)
            act = gateup(rmsnorm(h1, g_ffn[i], BF16), w_g, w_u)
            h2 = downproj(act, w_d, h1)
            return ple(rmsnorm(h2, g_ple[i], BF16), w_pg, p.astype(BF16), w_pp, h2)

        xn = rmsnorm(hp, g_mix[i], BF16)
        pf, pb = inproj(xn, w_main, cos_p, sin_p, tn, rope_ranges)
        small = smallproj(xn, w_small, b_small, fox_heads)
        logf = small[:, :fox_heads]
        f_cum, fp1, fp2, fp3 = cumsum_rows(logf)
        pieces = jnp.stack([fp1, fp2, fp3], axis=-1)
        pieces = pieces.reshape(seq, n_kv, REP * 3).transpose(1, 0, 2)
        k_fox = pb[:, c_fk:c_fk + fkv_w].reshape(seq, n_kv, HEAD_DIM).transpose(1, 0, 2)
        k_aug = jnp.concatenate([k_fox, pieces, jnp.zeros((n_kv, seq, HEAD_DIM - REP * 3), BF16)], axis=-1)
        o_fox = fox_prompt(_queries_t(pb[:, c_fq:c_fq + fq_w], n_kv), aug, k_aug,
                           _values_t(pb[:, c_fv:c_fv + fkv_w], n_kv), _head_lanes(f_cum, n_kv, 1))
        cmp_p = compress_prompt(pf, (c_kc // HEAD_DIM, c_vc // HEAD_DIM), pe, w1, w2, n_groups, nbp_p)
        cmp_vt = cmp_p[1].transpose(0, 2, 1).astype(BF16)
        gate_lanes = _head_lanes(small[:, fox_heads:fox_heads + n_gate], n_groups, 3)
        gate_lanes = jnp.pad(gate_lanes, ((0, 0), (0, 0), (0, 5), (0, 0)))
        o_nsa = nsa_prompt(_queries_t(pb[:, c_nq:c_nq + nq_w], n_groups), pb, c_ks // HEAD_DIM, c_kw // HEAD_DIM,
                           _values_t(pb[:, c_vs:c_vs + nkv_w], n_groups), _values_t(pb[:, c_vw:c_vw + nkv_w], n_groups),
                           cmp_p[0], cmp_vt, expand_t, gate_lanes, min(SEL_TOP_N, nb_p))
        hp = tail(hp, o_fox, o_nsa, p_prompt[i].reshape(seq, -1))
        outs_p.append((
            pf[:, c_fk:c_fk + 2 * fkv_w].reshape(1, seq, 2, n_kv, HEAD_DIM),
            logf.reshape(1, seq, fox_heads),
            pf[:, c_kc:c_kc + 4 * nkv_w].reshape(1, seq, 4, n_groups, HEAD_DIM),
            pf[seq - n_win_p:, c_kw:c_kw + 2 * nkv_w].reshape(1, n_win_p, 2, n_groups, HEAD_DIM),
        ))

        m_s = n_seq * t_new
        xs = rmsnorm(hs, g_mix[i], BF16)
        sf, sb = inproj(xs, w_main, cos_s, sin_s, tn, rope_ranges)
        small_s = smallproj(xs, w_small, b_small, fox_heads)
        logf_s = small_s[:, :fox_heads]
        logf_pool_t = cache_fox_logf[i].transpose(0, 2, 1)
        new_t = jnp.pad(logf_s.reshape(n_seq, t_new, fox_heads).transpose(0, 2, 1),
                        ((0, 0), (0, 0), (0, LANE - t_new)))
        f_all_t = cumsum_pages(pt_flat, logf_pool_t, new_t, n_seq, n_pages)
        fq_s = f_all_t[:, :, past_len:past_len + t_new].transpose(0, 2, 1)
        assert c_fk % (2 * fkv_w) == 0 and c_kc % nkv_w == 0 and c_ks % (2 * nkv_w) == 0 and c_kw % (2 * nkv_w) == 0
        q_fox_s = _heads_major(sb[:, c_fq:c_fq + fq_w], n_seq, t_new, n_kv)
        o_fox_s = fox_sample(pt_flat, cache_fox_kv, i, q_fox_s, sf, c_fk // (2 * fkv_w), fq_s, f_all_t, n_seq, n_pages)
        cmp_raw = compress_sample(pt_flat, cache_nsa_kv, i, sf, c_kc // nkv_w, pe, w1_pairs, w2, n_seq, n_pages, nb_s)
        cmp_s = cmp_raw[:, :, :nb_s * n_groups].reshape(n_seq, 2, nb_s, n_groups, HEAD_DIM).transpose(0, 1, 3, 2, 4)
        cmp_s = jnp.pad(cmp_s, ((0, 0), (0, 0), (0, 0), (0, nbp_s - nb_s), (0, 0)))
        q_nsa_s = _heads_major(sb[:, c_nq:c_nq + nq_w], n_seq, t_new, n_groups)
        gates_s = _heads_major(small_s[:, fox_heads:fox_heads + n_gate], n_seq, t_new, n_groups)
        gates_s = jnp.pad(gates_s, ((0, 0), (0, 0), (0, 0), (0, 13)))
        o_nsa_s = nsa_sample(pt_flat, cache_nsa_kv, state_nsa_win, i, q_nsa_s, cmp_s, sf, c_ks // (2 * nkv_w),
                             c_kw // (2 * nkv_w), gates_s, n_seq, n_pages, min(SEL_TOP_N, nb_s), past_len)
        new_win = win_shift(state_nsa_win, i, sf, c_kw // (2 * nkv_w))
        hs = tail(hs, _tokens_major(o_fox_s, n_seq, t_new, n_kv), _tokens_major(o_nsa_s, n_seq, t_new, n_groups),
                  p_sample[i].reshape(m_s, -1))
        outs_s.append((
            sf[:, c_fk:c_fk + 2 * fkv_w].reshape(n_seq, t_new, 2, n_kv, HEAD_DIM),
            logf_s.reshape(n_seq, t_new, fox_heads),
            sf[:, c_kc:c_kc + 4 * nkv_w].reshape(n_seq, t_new, 4, n_groups, HEAD_DIM),
            new_win,
        ))

    y_prompt = rmsnorm(hp, g_final, F32).reshape(x_prompt.shape)
    y_sample = rmsnorm(hs, g_final, F32).reshape(x_sample.shape)
    stack = lambda outs, j: jnp.stack([r[j] for r in outs], axis=0)
    return (y_prompt, y_sample, stack(outs_p, 0), stack(outs_p, 1), stack(outs_p, 2), stack(outs_p, 3),
            stack(outs_s, 0), stack(outs_s, 1), stack(outs_s, 2), stack(outs_s, 3))
```

```python
import functools

import numpy as np
import jax
import jax.numpy as jnp
from jax import lax
from jax.experimental import pallas as pl
from jax.experimental.pallas import tpu as pltpu

HEAD_DIM = 128
REP = 4
BLOCK = 64
SEL_TOP_N = 16
WINDOW = 512
Q_TILE = 512
K_TILE = 512
V_ROWS = 144
ROPE_THETA = 10000.0
RMS_EPS = 1e-6
ATTN_SCALE = HEAD_DIM ** -0.5
LOG2E = 1.4426950408889634
SCALE2 = ATTN_SCALE * LOG2E
NEG_INF = -1e30
REMOVED = -3e38
FORCE_BONUS = 1e4
LANE = 128
MIB = 1024 * 1024

F32 = jnp.float32
BF16 = jnp.bfloat16


def _params(sem, vmem_mib):
    return pltpu.CompilerParams(dimension_semantics=sem, vmem_limit_bytes=vmem_mib * MIB)


def _dot(a, b):
    return jnp.dot(a, b, preferred_element_type=F32)


def _dot_nt(a, b):
    return lax.dot_general(a, b, (((1,), (1,)), ((), ())), preferred_element_type=F32)


def _row_tile(m, cap):
    t = min(m, cap)
    assert m % t == 0
    return t


def _iota(shape, dim):
    return lax.broadcasted_iota(jnp.int32, shape, dim)


def _rmsnorm_kernel(x_ref, g_ref, o_ref):
    x = x_ref[...]
    y = x * lax.rsqrt(jnp.mean(x * x, axis=-1, keepdims=True) + RMS_EPS)
    o_ref[...] = (y * g_ref[...]).astype(o_ref.dtype)


def rmsnorm(x, g, out_dtype):
    m, d = x.shape
    tm = _row_tile(m, 256)
    return pl.pallas_call(
        _rmsnorm_kernel,
        grid=(m // tm,),
        in_specs=[pl.BlockSpec((tm, d), lambda i: (i, 0)), pl.BlockSpec((1, d), lambda i: (0, 0))],
        out_specs=pl.BlockSpec((tm, d), lambda i: (i, 0)),
        out_shape=jax.ShapeDtypeStruct((m, d), out_dtype),
        compiler_params=_params(("parallel",), 32),
        name="rmsnorm",
    )(x, g.reshape(1, d))


def _inproj_kernel(x_ref, w_ref, cos_ref, sin_ref, of_ref, ob_ref, *, tn, rope_ranges):
    acc = _dot(x_ref[...], w_ref[...])
    col0 = pl.program_id(1) * tn
    is_rope = None
    for lo, hi in rope_ranges:
        hit = (col0 >= lo) & (col0 < hi)
        is_rope = hit if is_rope is None else (is_rope | hit)

    @pl.when(is_rope)
    def _():
        cos = cos_ref[...]
        sin = sin_ref[...]
        for h in range(tn // HEAD_DIM):
            sl = slice(h * HEAD_DIM, (h + 1) * HEAD_DIM)
            xh = acc[:, sl]
            r = xh * cos + pltpu.roll(xh, HEAD_DIM // 2, 1) * sin
            of_ref[:, sl] = r
            ob_ref[:, sl] = r.astype(BF16)

    @pl.when(jnp.logical_not(is_rope))
    def _():
        of_ref[...] = acc
        ob_ref[...] = acc.astype(BF16)


def inproj(xn, w_main, cos, sin, tn, rope_ranges):
    m, k = xn.shape
    n = w_main.shape[1]
    tm = _row_tile(m, 1024)
    return pl.pallas_call(
        functools.partial(_inproj_kernel, tn=tn, rope_ranges=rope_ranges),
        grid=(m // tm, n // tn),
        in_specs=[
            pl.BlockSpec((tm, k), lambda i, j: (i, 0)),
            pl.BlockSpec((k, tn), lambda i, j: (0, j)),
            pl.BlockSpec((tm, HEAD_DIM), lambda i, j: (i, 0)),
            pl.BlockSpec((tm, HEAD_DIM), lambda i, j: (i, 0)),
        ],
        out_specs=[pl.BlockSpec((tm, tn), lambda i, j: (i, j)), pl.BlockSpec((tm, tn), lambda i, j: (i, j))],
        out_shape=[jax.ShapeDtypeStruct((m, n), F32), jax.ShapeDtypeStruct((m, n), BF16)],
        compiler_params=_params(("parallel", "arbitrary"), 48),
        name="inproj",
    )(xn, w_main, cos, sin)


def _smallproj_kernel(x_ref, w_ref, b_ref, o_ref, *, n_logf):
    v = _dot(x_ref[...], w_ref[...])
    z = v + b_ref[...]
    logf = -(jnp.maximum(-z, 0.0) + jnp.log1p(jnp.exp(-jnp.abs(z))))
    gate = jax.nn.sigmoid(v)
    lane = _iota(v.shape, 1)
    o_ref[...] = jnp.where(lane < n_logf, logf, gate)


def smallproj(xn, w_small, b_small, n_logf):
    m, k = xn.shape
    tm = _row_tile(m, 1024)
    return pl.pallas_call(
        functools.partial(_smallproj_kernel, n_logf=n_logf),
        grid=(m // tm,),
        in_specs=[
            pl.BlockSpec((tm, k), lambda i: (i, 0)),
            pl.BlockSpec((k, LANE), lambda i: (0, 0)),
            pl.BlockSpec((1, LANE), lambda i: (0, 0)),
        ],
        out_specs=pl.BlockSpec((tm, LANE), lambda i: (i, 0)),
        out_shape=jax.ShapeDtypeStruct((m, LANE), F32),
        compiler_params=_params(("parallel",), 32),
        name="smallproj",
    )(xn, w_small, b_small)


def _outproj_kernel(a_ref, b_ref, wa_ref, wb_ref, h_ref, o_ref):
    o_ref[...] = h_ref[...] + (_dot(a_ref[...], wa_ref[...]) + _dot(b_ref[...], wb_ref[...]))


def outproj(o_fox, o_nsa, w_a, w_b, h):
    m, ka = o_fox.shape
    kb = o_nsa.shape[1]
    n = w_a.shape[1]
    tm, tn = _row_tile(m, 1024), 512
    return pl.pallas_call(
        _outproj_kernel,
        grid=(m // tm, n // tn),
        in_specs=[
            pl.BlockSpec((tm, ka), lambda i, j: (i, 0)),
            pl.BlockSpec((tm, kb), lambda i, j: (i, 0)),
            pl.BlockSpec((ka, tn), lambda i, j: (0, j)),
            pl.BlockSpec((kb, tn), lambda i, j: (0, j)),
            pl.BlockSpec((tm, tn), lambda i, j: (i, j)),
        ],
        out_specs=pl.BlockSpec((tm, tn), lambda i, j: (i, j)),
        out_shape=jax.ShapeDtypeStruct((m, n), F32),
        compiler_params=_params(("parallel", "arbitrary"), 48),
        name="outproj",
    )(o_fox, o_nsa, w_a, w_b, h)


def _gateup_kernel(x_ref, wg_ref, wu_ref, o_ref):
    x = x_ref[...]
    g = _dot(x, wg_ref[...])
    u = _dot(x, wu_ref[...])
    o_ref[...] = (g * jax.nn.sigmoid(g) * u).astype(o_ref.dtype)


def gateup(xn, w_gate, w_up):
    m, k = xn.shape
    n = w_gate.shape[1]
    tm, tn = _row_tile(m, 1024), 256
    return pl.pallas_call(
        _gateup_kernel,
        grid=(m // tm, n // tn),
        in_specs=[
            pl.BlockSpec((tm, k), lambda i, j: (i, 0)),
            pl.BlockSpec((k, tn), lambda i, j: (0, j)),
            pl.BlockSpec((k, tn), lambda i, j: (0, j)),
        ],
        out_specs=pl.BlockSpec((tm, tn), lambda i, j: (i, j)),
        out_shape=jax.ShapeDtypeStruct((m, n), BF16),
        compiler_params=_params(("parallel", "arbitrary"), 48),
        name="gateup",
    )(xn, w_gate, w_up)


def _down_kernel(x_ref, w_ref, h_ref, o_ref):
    o_ref[...] = h_ref[...] + _dot(x_ref[...], w_ref[...])


def downproj(act, w_down, h):
    m, k = act.shape
    n = w_down.shape[1]
    tm, tn = _row_tile(m, 512), 256
    return pl.pallas_call(
        _down_kernel,
        grid=(m // tm, n // tn),
        in_specs=[
            pl.BlockSpec((tm, k), lambda i, j: (i, 0)),
            pl.BlockSpec((k, tn), lambda i, j: (0, j)),
            pl.BlockSpec((tm, tn), lambda i, j: (i, j)),
        ],
        out_specs=pl.BlockSpec((tm, tn), lambda i, j: (i, j)),
        out_shape=jax.ShapeDtypeStruct((m, n), F32),
        compiler_params=_params(("parallel", "arbitrary"), 56),
        name="downproj",
    )(act, w_down, h)


def _ple_kernel(x_ref, wg_ref, p_ref, wp_ref, h_ref, o_ref):
    gate = jax.nn.sigmoid(_dot(x_ref[...], wg_ref[...]))
    o_ref[...] = h_ref[...] + gate * _dot(p_ref[...], wp_ref[...])


def ple(xn, w_gate, p, w_proj, h):
    m, k = xn.shape
    kp = p.shape[1]
    n = w_gate.shape[1]
    tm, tn = _row_tile(m, 1024), 512
    return pl.pallas_call(
        _ple_kernel,
        grid=(m // tm, n // tn),
        in_specs=[
            pl.BlockSpec((tm, k), lambda i, j: (i, 0)),
            pl.BlockSpec((k, tn), lambda i, j: (0, j)),
            pl.BlockSpec((tm, kp), lambda i, j: (i, 0)),
            pl.BlockSpec((kp, tn), lambda i, j: (0, j)),
            pl.BlockSpec((tm, tn), lambda i, j: (i, j)),
        ],
        out_specs=pl.BlockSpec((tm, tn), lambda i, j: (i, j)),
        out_shape=jax.ShapeDtypeStruct((m, n), F32),
        compiler_params=_params(("parallel", "arbitrary"), 48),
        name="ple",
    )(xn, w_gate, p, w_proj, h)


def _split3(x):
    x1 = x.astype(BF16)
    r1 = x - x1.astype(F32)
    x2 = r1.astype(BF16)
    x3 = (r1 - x2.astype(F32)).astype(BF16)
    return x1, x2, x3


def _cumsum_rows_kernel(x_ref, tri_ref, o_ref, p1_ref, p2_ref, p3_ref, carry_ref):
    @pl.when(pl.program_id(0) == 0)
    def _():
        carry_ref[...] = jnp.zeros_like(carry_ref)

    x1, x2, x3 = _split3(x_ref[...])
    tri = tri_ref[...]
    out = (_dot(tri, x1) + _dot(tri, x2) + _dot(tri, x3)) + carry_ref[...]
    o_ref[...] = out
    carry_ref[...] = out[out.shape[0] - 1:, :]
    p1_ref[...], p2_ref[...], p3_ref[...] = _split3(out * (-1.0 / ATTN_SCALE))


def cumsum_rows(x):
    t, h = x.shape
    c = _row_tile(t, 512)
    tri = jnp.asarray(np.tril(np.ones((c, c), np.float32)), BF16)
    row_spec = pl.BlockSpec((c, h), lambda i: (i, 0))
    return pl.pallas_call(
        _cumsum_rows_kernel,
        grid=(t // c,),
        in_specs=[row_spec, pl.BlockSpec((c, c), lambda i: (0, 0))],
        out_specs=[row_spec] * 4,
        out_shape=[jax.ShapeDtypeStruct((t, h), F32)] + [jax.ShapeDtypeStruct((t, h), BF16)] * 3,
        scratch_shapes=[pltpu.VMEM((1, h), F32)],
        compiler_params=_params(("arbitrary",), 32),
        name="cumsum_rows",
    )(x, tri)


def _cumsum_pages_kernel(pt_ref, *refs, n_pages):
    page_refs = refs[:n_pages]
    new_ref, tri_ref, o_ref = refs[n_pages:]
    tri = tri_ref[...]
    h = new_ref.shape[0]
    carry = jnp.zeros((h, 1), F32)
    for p in range(n_pages + 1):
        x = page_refs[p][...] if p < n_pages else new_ref[...]
        x1, x2, x3 = _split3(x)
        out = (_dot(x1, tri) + _dot(x2, tri) + _dot(x3, tri)) + carry
        o_ref[:, p * LANE:(p + 1) * LANE] = out
        carry = out[:, LANE - 1:]


def cumsum_pages(pt_flat, logf_pool_t, new_t, n_seq, n_pages):
    h = logf_pool_t.shape[1]
    tri = jnp.asarray(np.triu(np.ones((LANE, LANE), np.float32)), BF16)
    page_spec = lambda p: pl.BlockSpec((None, h, LANE), lambda b, pt: (pt[b * n_pages + p], 0, 0))
    grid_spec = pltpu.PrefetchScalarGridSpec(
        num_scalar_prefetch=1,
        grid=(n_seq,),
        in_specs=[page_spec(p) for p in range(n_pages)] + [
            pl.BlockSpec((None, h, LANE), lambda b, pt: (b, 0, 0)),
            pl.BlockSpec((LANE, LANE), lambda b, pt: (0, 0)),
        ],
        out_specs=pl.BlockSpec((None, h, (n_pages + 1) * LANE), lambda b, pt: (b, 0, 0)),
    )
    return pl.pallas_call(
        functools.partial(_cumsum_pages_kernel, n_pages=n_pages),
        grid_spec=grid_spec,
        out_shape=jax.ShapeDtypeStruct((n_seq, h, (n_pages + 1) * LANE), F32),
        compiler_params=_params(("arbitrary",), 32),
        name="cumsum_pages",
    )(pt_flat, *([logf_pool_t] * n_pages), new_t, tri)


def _silu(x):
    return x * jax.nn.sigmoid(x)


def _compress_prompt_kernel(cb_ref, x_ref, pe_ref, w1_ref, w2_ref, o_ref, *, nb):
    hidden = w1_ref.shape[2]

    def body(l, acc):
        xl = x_ref[pl.ds(l, nb, stride=BLOCK), :] + pe_ref[pl.ds(l, 1), :]
        return acc + _dot(xl.astype(BF16), w1_ref[l])

    acc = lax.fori_loop(0, BLOCK, body, jnp.zeros((nb, hidden), F32))
    out = _dot(_silu(acc).astype(BF16), w2_ref[...])
    o_ref[...] = jnp.zeros_like(o_ref)
    o_ref[0:nb, :] = out


def compress_prompt(proj_f32, col_blocks, pe, w1, w2, n_groups, nbp):
    t = proj_f32.shape[0]
    nb = t // BLOCK
    hidden = w1.shape[-1]
    cb = jnp.asarray(col_blocks, jnp.int32)
    grid_spec = pltpu.PrefetchScalarGridSpec(
        num_scalar_prefetch=1,
        grid=(2, n_groups),
        in_specs=[
            pl.BlockSpec((t, HEAD_DIM), lambda kv, g, cb: (0, cb[kv] + g)),
            pl.BlockSpec((None, BLOCK, HEAD_DIM), lambda kv, g, cb: (kv, 0, 0)),
            pl.BlockSpec((None, BLOCK, HEAD_DIM, hidden), lambda kv, g, cb: (kv, 0, 0, 0)),
            pl.BlockSpec((None, hidden, HEAD_DIM), lambda kv, g, cb: (kv, 0, 0)),
        ],
        out_specs=pl.BlockSpec((None, None, nbp, HEAD_DIM), lambda kv, g, cb: (kv, g, 0, 0)),
    )
    return pl.pallas_call(
        functools.partial(_compress_prompt_kernel, nb=nb),
        grid_spec=grid_spec,
        out_shape=jax.ShapeDtypeStruct((2, n_groups, nbp, HEAD_DIM), F32),
        compiler_params=_params(("arbitrary", "arbitrary"), 40),
        name="compress_prompt",
    )(cb, proj_f32, pe, w1, w2)


def _compress_sample_kernel(pt_ref, *refs, n_pages, n_groups, nb):
    pages = refs[:n_pages]
    new_ref, pe_ref, w1_ref, w2_ref, o_ref, newblk_ref = refs[n_pages:]
    t_new = new_ref.shape[0]
    hidden = w1_ref.shape[2]
    rows_pad = o_ref.shape[0]
    n_rows = (2 * n_pages + 1) * n_groups

    newblk_ref[...] = jnp.zeros_like(newblk_ref)
    for l in range(t_new):
        for g in range(n_groups):
            newblk_ref[l, g:g + 1, :] = new_ref[l:l + 1, g * HEAD_DIM:(g + 1) * HEAD_DIM]

    def gather(l):
        rr = []
        for pg in pages:
            rr.append(pg[l])
            rr.append(pg[l + BLOCK])
        rr.append(newblk_ref[l])
        if rows_pad > n_rows:
            rr.append(jnp.zeros((rows_pad - n_rows, HEAD_DIM), F32))
        return jnp.concatenate(rr, axis=0) + pe_ref[l:l + 1, :]

    acc = jnp.zeros((rows_pad, hidden), F32)
    for l2 in range(BLOCK // 2):
        x = jnp.concatenate([gather(2 * l2), gather(2 * l2 + 1)], axis=1).astype(BF16)
        acc = acc + _dot(x, w1_ref[l2])
    out = _dot(_silu(acc).astype(BF16), w2_ref[...])
    o_ref[...] = jnp.where(_iota(out.shape, 0) < nb * n_groups, out, 0.0)


def compress_sample(pt_flat, nsa_cache, layer, new_rows, new_col_block, pe, w1_pairs, w2, n_seq, n_pages, nb):
    page, _, n_groups, _ = nsa_cache.shape[2:]
    assert page == 2 * BLOCK
    gw = n_groups * HEAD_DIM
    hidden = w1_pairs.shape[-1]
    t_new = new_rows.shape[0] // n_seq
    rows_pad = -(-((2 * n_pages + 1) * n_groups) // 8) * 8
    spec = lambda p: pl.BlockSpec((None, None, page, None, n_groups, HEAD_DIM),
                                  lambda kv, b, pt: (layer, pt[b * n_pages + p], 0, kv, 0, 0))
    grid_spec = pltpu.PrefetchScalarGridSpec(
        num_scalar_prefetch=1,
        grid=(2, n_seq),
        in_specs=[spec(p) for p in range(n_pages)] + [
            pl.BlockSpec((t_new, gw), lambda kv, b, pt: (b, new_col_block + kv)),
            pl.BlockSpec((None, BLOCK, HEAD_DIM), lambda kv, b, pt: (kv, 0, 0)),
            pl.BlockSpec((None, BLOCK // 2, 2 * HEAD_DIM, hidden), lambda kv, b, pt: (kv, 0, 0, 0)),
            pl.BlockSpec((None, hidden, HEAD_DIM), lambda kv, b, pt: (kv, 0, 0)),
        ],
        out_specs=pl.BlockSpec((None, None, rows_pad, HEAD_DIM), lambda kv, b, pt: (b, kv, 0, 0)),
        scratch_shapes=[pltpu.VMEM((BLOCK, n_groups, HEAD_DIM), F32)],
    )
    return pl.pallas_call(
        functools.partial(_compress_sample_kernel, n_pages=n_pages, n_groups=n_groups, nb=nb),
        grid_spec=grid_spec,
        out_shape=jax.ShapeDtypeStruct((n_seq, 2, rows_pad, HEAD_DIM), F32),
        compiler_params=_params(("arbitrary", "arbitrary"), 40),
        name="compress_sample",
    )(pt_flat, *([nsa_cache] * n_pages), new_rows, pe, w1_pairs, w2)


def _softmax_rows(s):
    m = jnp.max(s, axis=-1, keepdims=True)
    e = jnp.exp(s - m)
    return e / jnp.sum(e, axis=-1, keepdims=True)


def _softmax_cols(s):
    m = jnp.max(s, axis=0, keepdims=True)
    e = jnp.exp(s - m)
    return e / jnp.sum(e, axis=0, keepdims=True)


def _top_n_mask_t(score_t, blk_t, n_sel, n_blocks_pad):
    sel = jnp.zeros(score_t.shape, F32)
    for _ in range(n_sel):
        mx = jnp.max(score_t, axis=0, keepdims=True)
        idx = jnp.min(jnp.where(score_t == mx, blk_t, n_blocks_pad), axis=0, keepdims=True)
        hit = blk_t == idx
        sel = jnp.where(hit, 1.0, sel)
        score_t = jnp.where(hit, REMOVED, score_t)
    return sel


def _selection_scores_t(imp_t, blk_t, qpos_t):
    cur = qpos_t // BLOCK
    forced = (blk_t == 0) | (blk_t == cur) | (blk_t == cur - 1)
    avail = blk_t * BLOCK <= qpos_t
    return jnp.where(avail, imp_t + jnp.where(forced, FORCE_BONUS, 0.0), NEG_INF)


def _online_tile_t(x, vt_tile, m, acc_ref, shift):
    mx = jnp.max(x, axis=0, keepdims=True) * SCALE2
    if shift is not None:
        mx = mx + shift
    m_new = jnp.maximum(m, mx)
    off = m_new if shift is None else m_new - shift
    p = jnp.exp2(x * SCALE2 - off)
    acc_ref[...] = jnp.exp2(m - m_new) * acc_ref[...] + _dot(vt_tile, p.astype(BF16))
    return m_new


def _store_heads(o_ref, o_t):
    for r in range(REP):
        o_ref[:, r * HEAD_DIM:(r + 1) * HEAD_DIM] = o_t[:, r * Q_TILE:(r + 1) * Q_TILE].T.astype(o_ref.dtype)


def _fox_prompt_kernel(qt_ref, aug_ref, ka_ref, vt_ref, fq_ref, o_ref, acc_ref):
    q0 = pl.program_id(1) * Q_TILE
    cols = REP * Q_TILE
    qa = jnp.concatenate([qt_ref[...], aug_ref[...]], axis=0)
    fq2 = fq_ref[...] * LOG2E
    acc_ref[...] = jnp.zeros_like(acc_ref)
    n_full = q0 // K_TILE

    def scores(j):
        k0 = pl.multiple_of(j * K_TILE, K_TILE)
        return _dot(ka_ref[pl.ds(k0, K_TILE), :], qa)

    def update(j, x, m):
        k0 = pl.multiple_of(j * K_TILE, K_TILE)
        return _online_tile_t(x, vt_ref[:, pl.ds(k0, K_TILE)], m, acc_ref, fq2)

    m = lax.fori_loop(0, n_full, lambda j, m: update(j, scores(j), m), jnp.full((1, cols), NEG_INF, F32))
    kpos = n_full * K_TILE + _iota((K_TILE, cols), 0)
    qpos = q0 + (_iota((K_TILE, cols), 1) & (Q_TILE - 1))
    update(n_full, jnp.where(kpos <= qpos, scores(n_full), NEG_INF), m)
    acc = acc_ref[...]
    _store_heads(o_ref, acc[0:HEAD_DIM] / acc[HEAD_DIM:HEAD_DIM + 1])


def fox_prompt(q_t, aug, k_aug, v_t, fq_lanes):
    n_kv, _, t = v_t.shape
    cols = REP * Q_TILE
    return pl.pallas_call(
        _fox_prompt_kernel,
        grid=(n_kv, t // Q_TILE),
        in_specs=[
            pl.BlockSpec((None, HEAD_DIM, cols), lambda g, i: (g, 0, i)),
            pl.BlockSpec((HEAD_DIM, cols), lambda g, i: (0, 0)),
            pl.BlockSpec((None, t, 2 * HEAD_DIM), lambda g, i: (g, 0, 0)),
            pl.BlockSpec((None, V_ROWS, t), lambda g, i: (g, 0, 0)),
            pl.BlockSpec((None, None, 1, cols), lambda g, i: (g, i, 0, 0)),
        ],
        out_specs=pl.BlockSpec((Q_TILE, REP * HEAD_DIM), lambda g, i: (i, g)),
        out_shape=jax.ShapeDtypeStruct((t, n_kv * REP * HEAD_DIM), BF16),
        scratch_shapes=[pltpu.VMEM((V_ROWS, cols), F32)],
        compiler_params=_params(("arbitrary", "arbitrary"), 40),
        name="fox_prompt",
    )(q_t, aug, k_aug, v_t, fq_lanes)


def _nsa_prompt_kernel(qt_ref, ks_ref, vst_ref, kw_ref, vwt_ref, ck_ref, cvt_ref, et_ref, gate_ref, wmask_ref, o_ref,
                       acc_ref, *, n_sel):
    q0 = pl.program_id(1) * Q_TILE
    cols = REP * Q_TILE
    nbp = ck_ref.shape[0]
    qt = qt_ref[...]

    x_c = _dot(ck_ref[...].astype(BF16), qt) * ATTN_SCALE
    blk = _iota((nbp, cols), 0)
    qpos = q0 + (_iota((nbp, cols), 1) & (Q_TILE - 1))
    x_c = jnp.where((blk + 1) * BLOCK - 1 <= qpos, x_c, NEG_INF)
    p_c = _softmax_cols(x_c) * (qpos >= BLOCK - 1).astype(F32)
    o_cmp = _dot(cvt_ref[...], p_c.astype(BF16))

    imp_t = p_c[:, 0:Q_TILE]
    for r in range(1, REP):
        imp_t = imp_t + p_c[:, r * Q_TILE:(r + 1) * Q_TILE]
    blk_t = _iota((nbp, Q_TILE), 0)
    qpos_t = q0 + _iota((nbp, Q_TILE), 1)
    sel_t = _top_n_mask_t(_selection_scores_t(imp_t, blk_t, qpos_t), blk_t, n_sel, nbp).astype(BF16)

    acc_ref[...] = jnp.zeros_like(acc_ref)
    n_full = q0 // K_TILE

    def tile(j, m, masked):
        k0 = pl.multiple_of(j * K_TILE, K_TILE)
        x = _dot(ks_ref[pl.ds(k0, K_TILE), :], qt)
        picked = _dot(et_ref[pl.ds(k0, K_TILE), :], sel_t)
        bias = (picked - 1.0) * -NEG_INF
        if masked:
            kpos = k0 + _iota((K_TILE, Q_TILE), 0)
            bias = jnp.where(kpos <= q0 + _iota((K_TILE, Q_TILE), 1), bias, NEG_INF)
        x = x + jnp.concatenate([bias] * REP, axis=1)
        return _online_tile_t(x, vst_ref[:, pl.ds(k0, K_TILE)], m, acc_ref, None)

    m = lax.fori_loop(0, n_full, lambda j, m: tile(j, m, False), jnp.full((1, cols), NEG_INF, F32))
    tile(n_full, m, True)
    acc = acc_ref[...]
    o_slc = acc[0:HEAD_DIM] / acc[HEAD_DIM:HEAD_DIM + 1]

    wlen = WINDOW + Q_TILE
    w0 = pl.multiple_of(jnp.maximum(q0 - WINDOW, 0), Q_TILE)
    x_w = _dot(kw_ref[pl.ds(w0, wlen), :], qt) * SCALE2
    x_w = x_w + jnp.concatenate([wmask_ref[...]] * REP, axis=1)
    p_w = jnp.exp2(x_w - jnp.max(x_w, axis=0, keepdims=True))
    a_w = _dot(vwt_ref[:, pl.ds(w0, wlen)], p_w.astype(BF16))
    o_win = a_w[0:HEAD_DIM] / a_w[HEAD_DIM:HEAD_DIM + 1]

    gates = gate_ref[...]
    _store_heads(o_ref, gates[0:1] * o_cmp + gates[1:2] * o_slc + gates[2:3] * o_win)


def _window_masks(t):
    n_var = WINDOW // Q_TILE + 1
    k = np.arange(WINDOW + Q_TILE)[None, :, None]
    q = np.arange(Q_TILE)[None, None, :]
    q0 = (np.arange(n_var) * Q_TILE)[:, None, None]
    dist = q0 + q - (np.maximum(q0 - WINDOW, 0) + k)
    return jnp.asarray(np.where((dist >= 0) & (dist < WINDOW), 0.0, NEG_INF), F32)


def nsa_prompt(q_t, proj_bf16, ks_blk0, kw_blk0, vs_t, vw_t, cmp_k, cmp_vt, expand_t, gate_lanes, n_sel):
    assert WINDOW % Q_TILE == 0
    wmask = _window_masks(vs_t.shape[2])
    last_var = wmask.shape[0] - 1
    n_groups, _, t = vs_t.shape
    nbp = cmp_k.shape[1]
    cols = REP * Q_TILE
    return pl.pallas_call(
        functools.partial(_nsa_prompt_kernel, n_sel=n_sel),
        grid=(n_groups, t // Q_TILE),
        in_specs=[
            pl.BlockSpec((None, HEAD_DIM, cols), lambda g, i: (g, 0, i)),
            pl.BlockSpec((t, HEAD_DIM), lambda g, i: (0, ks_blk0 + g)),
            pl.BlockSpec((None, V_ROWS, t), lambda g, i: (g, 0, 0)),
            pl.BlockSpec((t, HEAD_DIM), lambda g, i: (0, kw_blk0 + g)),
            pl.BlockSpec((None, V_ROWS, t), lambda g, i: (g, 0, 0)),
            pl.BlockSpec((None, nbp, HEAD_DIM), lambda g, i: (g, 0, 0)),
            pl.BlockSpec((None, HEAD_DIM, nbp), lambda g, i: (g, 0, 0)),
            pl.BlockSpec((t, nbp), lambda g, i: (0, 0)),
            pl.BlockSpec((None, None, 8, cols), lambda g, i: (g, i, 0, 0)),
            pl.BlockSpec((None, WINDOW + Q_TILE, Q_TILE), lambda g, i: (jnp.minimum(i, last_var), 0, 0)),
        ],
        out_specs=pl.BlockSpec((Q_TILE, REP * HEAD_DIM), lambda g, i: (i, g)),
        out_shape=jax.ShapeDtypeStruct((t, n_groups * REP * HEAD_DIM), BF16),
        scratch_shapes=[pltpu.VMEM((V_ROWS, cols), F32)],
        compiler_params=_params(("arbitrary", "arbitrary"), 56),
        name="nsa_prompt",
    )(q_t, proj_bf16, vs_t, proj_bf16, vw_t, cmp_k, cmp_vt, expand_t, gate_lanes, wmask)


def _pad_rows(x, n):
    return jnp.concatenate([x, jnp.zeros((n - x.shape[0], x.shape[1]), x.dtype)], axis=0)


def _copy_rows(block_refs, rows_ref):
    views = []
    for p, r in enumerate(block_refs):
        rows_ref[p] = r[...].reshape(rows_ref.shape[1:])
        views.append(rows_ref.at[p])
    return views


def _fox_sample_kernel(pt_ref, *refs, n_pages, n_kv):
    q_ref, new_ref, fq_ref, fk_ref, o_ref, kv_ref = refs[n_pages:]
    pages = refs[:n_pages]
    stride = 2 * n_kv
    page = pages[0].shape[0] // stride
    t_new = new_ref.shape[0]
    rows = REP * t_new
    past = n_pages * page
    n_keys = kv_ref.shape[1]
    for c in range(stride):
        for p in range(n_pages):
            kv_ref[c, p * page:(p + 1) * page, :] = pages[p][pl.ds(c, page, stride=stride), :].astype(BF16)
        kv_ref[c, past:n_keys, :] = _pad_rows(new_ref[:, c * HEAD_DIM:(c + 1) * HEAD_DIM], n_keys - past).astype(BF16)
    visible = _iota((t_new, n_keys), 1) <= past + _iota((t_new, n_keys), 0)
    for g in range(n_kv):
        s = _dot_nt(q_ref[g], kv_ref[g]) * ATTN_SCALE
        s = s.reshape(REP, t_new, n_keys) - fk_ref[g * REP:(g + 1) * REP, :][:, None, :]
        s = jnp.where(visible[None], s, NEG_INF).reshape(rows, n_keys)
        fq = jnp.concatenate([fq_ref[:, g * REP + r:g * REP + r + 1] for r in range(REP)], axis=0)
        prob = _softmax_rows(s + fq).astype(BF16)
        o_ref[g] = _dot(prob, kv_ref[n_kv + g]).astype(o_ref.dtype)


def fox_sample(pt_flat, fox_cache, layer, q_s, new_rows, new_col_block, fq_s, fk_s, n_seq, n_pages):
    depth, n_pool, page, _, n_kv, _ = fox_cache.shape
    fox_cache = fox_cache.reshape(depth, n_pool, page * 2 * n_kv, HEAD_DIM)
    t_new = new_rows.shape[0] // n_seq
    rows = REP * t_new
    n_heads = fq_s.shape[2]
    assert page == LANE
    spec = lambda p: pl.BlockSpec((None, None, page * 2 * n_kv, HEAD_DIM),
                                  lambda b, pt: (layer, pt[b * n_pages + p], 0, 0))
    grid_spec = pltpu.PrefetchScalarGridSpec(
        num_scalar_prefetch=1,
        grid=(n_seq,),
        in_specs=[spec(p) for p in range(n_pages)] + [
            pl.BlockSpec((None, n_kv, rows, HEAD_DIM), lambda b, pt: (b, 0, 0, 0)),
            pl.BlockSpec((t_new, 2 * n_kv * HEAD_DIM), lambda b, pt: (b, new_col_block)),
            pl.BlockSpec((None, t_new, n_heads), lambda b, pt: (b, 0, 0)),
            pl.BlockSpec((None, n_heads, (n_pages + 1) * page), lambda b, pt: (b, 0, 0)),
        ],
        out_specs=pl.BlockSpec((None, n_kv, rows, HEAD_DIM), lambda b, pt: (b, 0, 0, 0)),
        scratch_shapes=[pltpu.VMEM((2 * n_kv, (n_pages + 1) * page, HEAD_DIM), BF16)],
    )
    return pl.pallas_call(
        functools.partial(_fox_sample_kernel, n_pages=n_pages, n_kv=n_kv),
        grid_spec=grid_spec,
        out_shape=jax.ShapeDtypeStruct((n_seq, n_kv, rows, HEAD_DIM), BF16),
        compiler_params=_params(("arbitrary",), 52),
        name="fox_sample",
    )(pt_flat, *([fox_cache] * n_pages), q_s, new_rows, fq_s, fk_s)


def _nsa_sample_kernel(pt_ref, *refs, n_pages, n_groups, n_sel, past_len):
    (q_ref, cmp_ref, new_slc_ref, new_win_ref, win_ref, gate_ref, o_ref,
     krows_ref, vrows_ref, wrows_ref) = refs[2 * n_pages:]
    page = refs[0].shape[0]
    kpages = _copy_rows(refs[:n_pages], krows_ref)
    vpages = _copy_rows(refs[n_pages:2 * n_pages], vrows_ref)
    t_new = new_slc_ref.shape[0]
    rows = REP * t_new
    gw = n_groups * HEAD_DIM
    nbp = cmp_ref.shape[2]
    n_buf = win_ref.shape[0]
    win_rows = _copy_rows([win_ref], wrows_ref)[0]
    blocks_per_page = page // BLOCK

    qpos = past_len + _iota((t_new, nbp), 0)
    c_mask = (_iota((t_new, nbp), 1) + 1) * BLOCK - 1 <= qpos
    any_vis = (qpos >= BLOCK - 1).astype(F32)
    o_cmp, imps = [], []
    for g in range(n_groups):
        s_c = _dot_nt(q_ref[g], cmp_ref[0, g].astype(BF16)) * ATTN_SCALE
        s_c = jnp.where(c_mask[None], s_c.reshape(REP, t_new, nbp), NEG_INF)
        p_c = _softmax_rows(s_c) * any_vis[None]
        o_cmp.append(_dot(p_c.reshape(rows, nbp).astype(BF16), cmp_ref[1, g].astype(BF16)))
        imp = p_c[0]
        for r in range(1, REP):
            imp = imp + p_c[r]
        imps.append(imp)

    imp_all = _pad_rows(jnp.concatenate(imps, axis=0), LANE)
    blk_t = _iota((nbp, LANE), 0)
    qpos_t = past_len + _iota((nbp, LANE), 1) % t_new
    sel_t = _top_n_mask_t(_selection_scores_t(imp_all.T, blk_t, qpos_t), blk_t, n_sel, nbp)
    sel_all = sel_t.T

    lane_p = _iota((t_new, page), 1)
    row_p = _iota((t_new, page), 0)
    lane_blk = lane_p // BLOCK
    widx = _iota((t_new, n_buf + page), 1)
    wdist = n_buf + _iota((t_new, n_buf + page), 0) - widx
    w_mask = (wdist >= 0) & (wdist < WINDOW) & (widx < n_buf + t_new)
    gates = gate_ref[...]

    for g in range(n_groups):
        q4 = q_ref[g]
        gs = slice(g * HEAD_DIM, (g + 1) * HEAD_DIM)
        vs = slice(gw + g * HEAD_DIM, gw + (g + 1) * HEAD_DIM)
        sel_g = sel_all[g * t_new:(g + 1) * t_new, :]

        chunks = []
        for p in range(n_pages + 1):
            if p < n_pages:
                k_p = kpages[p][pl.ds(g, page, stride=n_groups), :].astype(BF16)
            else:
                k_p = _pad_rows(new_slc_ref[:, gs], page).astype(BF16)
            s = _dot_nt(q4, k_p) * ATTN_SCALE
            picked = jnp.zeros((t_new, page), F32)
            for c in range(blocks_per_page):
                b_idx = p * blocks_per_page + c
                picked = jnp.where(lane_blk == c, sel_g[:, b_idx:b_idx + 1], picked)
            ok = picked > 0.5
            if p == n_pages:
                ok = ok & (lane_p <= row_p)
            chunks.append(jnp.where(ok[None], s.reshape(REP, t_new, page), NEG_INF).reshape(rows, page))
        prob = _softmax_rows(jnp.concatenate(chunks, axis=1)).astype(BF16)
        o_slc = jnp.zeros((rows, HEAD_DIM), F32)
        for p in range(n_pages + 1):
            if p < n_pages:
                v_p = vpages[p][pl.ds(g, page, stride=n_groups), :].astype(BF16)
            else:
                v_p = _pad_rows(new_slc_ref[:, vs], page).astype(BF16)
            o_slc = o_slc + _dot(prob[:, p * page:(p + 1) * page], v_p)

        kw = jnp.concatenate([win_rows[pl.ds(g, n_buf, stride=2 * n_groups), :],
                              _pad_rows(new_win_ref[:, gs], page)], axis=0).astype(BF16)
        vw = jnp.concatenate([win_rows[pl.ds(n_groups + g, n_buf, stride=2 * n_groups), :],
                              _pad_rows(new_win_ref[:, vs], page)], axis=0).astype(BF16)
        s_w = _dot_nt(q4, kw) * ATTN_SCALE
        s_w = jnp.where(w_mask[None], s_w.reshape(REP, t_new, n_buf + page), NEG_INF).reshape(rows, n_buf + page)
        o_win = _dot(_softmax_rows(s_w).astype(BF16), vw)

        gt = gates[g]
        o_ref[g] = (gt[:, 0:1] * o_cmp[g] + gt[:, 1:2] * o_slc + gt[:, 2:3] * o_win).astype(o_ref.dtype)


def nsa_sample(pt_flat, nsa_cache, win_state, layer, q_s, cmp_s, new_rows, slc_col_block, win_col_block, gates_s,
               n_seq, n_pages, n_sel, past_len):
    page, _, n_groups, _ = nsa_cache.shape[2:]
    gw = n_groups * HEAD_DIM
    t_new = new_rows.shape[0] // n_seq
    rows = REP * t_new
    nbp = cmp_s.shape[3]
    n_buf = win_state.shape[2]
    spec = lambda p, slot: pl.BlockSpec((None, None, page, None, n_groups, HEAD_DIM),
                                        lambda b, pt: (layer, pt[b * n_pages + p], 0, slot, 0, 0))
    grid_spec = pltpu.PrefetchScalarGridSpec(
        num_scalar_prefetch=1,
        grid=(n_seq,),
        in_specs=[spec(p, 2) for p in range(n_pages)] + [spec(p, 3) for p in range(n_pages)] + [
            pl.BlockSpec((None, n_groups, rows, HEAD_DIM), lambda b, pt: (b, 0, 0, 0)),
            pl.BlockSpec((None, 2, n_groups, nbp, HEAD_DIM), lambda b, pt: (b, 0, 0, 0, 0)),
            pl.BlockSpec((t_new, 2 * gw), lambda b, pt: (b, slc_col_block)),
            pl.BlockSpec((t_new, 2 * gw), lambda b, pt: (b, win_col_block)),
            pl.BlockSpec((None, None, n_buf, 2, n_groups, HEAD_DIM), lambda b, pt: (layer, b, 0, 0, 0, 0)),
            pl.BlockSpec((None, n_groups, rows, 16), lambda b, pt: (b, 0, 0, 0)),
        ],
        out_specs=pl.BlockSpec((None, n_groups, rows, HEAD_DIM), lambda b, pt: (b, 0, 0, 0)),
        scratch_shapes=[pltpu.VMEM((n_pages, page * n_groups, HEAD_DIM), F32),
                        pltpu.VMEM((n_pages, page * n_groups, HEAD_DIM), F32),
                        pltpu.VMEM((1, n_buf * 2 * n_groups, HEAD_DIM), F32)],
    )
    return pl.pallas_call(
        functools.partial(_nsa_sample_kernel, n_pages=n_pages, n_groups=n_groups, n_sel=n_sel, past_len=past_len),
        grid_spec=grid_spec,
        out_shape=jax.ShapeDtypeStruct((n_seq, n_groups, rows, HEAD_DIM), BF16),
        compiler_params=_params(("arbitrary",), 56),
        name="nsa_sample",
    )(pt_flat, *([nsa_cache] * (2 * n_pages)), q_s, cmp_s, new_rows, new_rows, win_state, gates_s)


def _win_shift_kernel(win_ref, new_ref, o_ref):
    n_buf, _, n_groups, _ = win_ref.shape
    t_new = new_ref.shape[0]
    gw = n_groups * HEAD_DIM
    o_ref[0:n_buf - t_new] = win_ref[t_new:n_buf]
    for s in range(2):
        for g in range(n_groups):
            c0 = s * gw + g * HEAD_DIM
            o_ref[n_buf - t_new:n_buf, s, g, :] = new_ref[:, c0:c0 + HEAD_DIM]


def win_shift(win_state, layer, new_rows, win_col_block):
    _, n_seq, n_buf, _, n_groups, _ = win_state.shape
    t_new = new_rows.shape[0] // n_seq
    assert n_buf > t_new
    blk = (None, n_buf, 2, n_groups, HEAD_DIM)
    return pl.pallas_call(
        _win_shift_kernel,
        grid=(n_seq,),
        in_specs=[
            pl.BlockSpec((None,) + blk, lambda b: (layer, b, 0, 0, 0, 0)),
            pl.BlockSpec((t_new, 2 * n_groups * HEAD_DIM), lambda b: (b, win_col_block)),
        ],
        out_specs=pl.BlockSpec(blk, lambda b: (b, 0, 0, 0, 0)),
        out_shape=jax.ShapeDtypeStruct(win_state.shape[1:], F32),
        compiler_params=_params(("parallel",), 32),
        name="win_shift",
    )(win_state, new_rows)


def _rope_tables(pos):
    half = HEAD_DIM // 2
    inv_freq = ROPE_THETA ** (-jnp.arange(half, dtype=F32) / half)
    ang = pos.astype(F32)[:, None] * inv_freq[None, :]
    cos, sin = jnp.cos(ang), jnp.sin(ang)
    return jnp.concatenate([cos, cos], axis=-1), jnp.concatenate([-sin, sin], axis=-1)


def _heads_major(x, n_seq, t_new, n_groups):
    w = x.shape[1] // (n_groups * REP)
    return x.reshape(n_seq, t_new, n_groups, REP, w).transpose(0, 2, 3, 1, 4).reshape(n_seq, n_groups, REP * t_new, w)


def _tokens_major(x, n_seq, t_new, n_groups):
    w = x.shape[-1]
    return x.reshape(n_seq, n_groups, REP, t_new, w).transpose(0, 3, 1, 2, 4).reshape(n_seq * t_new, n_groups * REP * w)


def _queries_t(q, n_groups):
    t = q.shape[0]
    nq = t // Q_TILE
    q = q.reshape(nq, Q_TILE, n_groups, REP, HEAD_DIM).transpose(2, 4, 0, 3, 1)
    return q.reshape(n_groups, HEAD_DIM, nq * REP * Q_TILE)


def _values_t(v, n_groups):
    t = v.shape[0]
    vt = v.T.reshape(n_groups, HEAD_DIM, t)
    return jnp.concatenate([vt, jnp.ones((n_groups, V_ROWS - HEAD_DIM, t), v.dtype)], axis=1)


def _head_lanes(x, n_groups, w):
    t = x.shape[0]
    nq = t // Q_TILE
    x = x.reshape(nq, Q_TILE, n_groups, REP, w).transpose(2, 0, 4, 3, 1)
    return x.reshape(n_groups, nq, w, REP * Q_TILE)


def kernel(x_prompt, x_sample, cache_fox_kv, cache_fox_logf, cache_nsa_kv, state_nsa_win, page_table, p_prompt, p_sample, g_mix, w_in, b_fgate, cmp_pe_k, cmp_w1_k, cmp_w2_k, cmp_pe_v, cmp_w1_v, cmp_w2_v, w_out, g_ffn, w_gate, w_up, w_down, g_ple, w_ple_gate, w_ple_proj, g_final):
    depth = w_in.shape[0]
    _, seq, d_model = x_prompt.shape
    n_seq, t_new, _ = x_sample.shape
    page = cache_fox_kv.shape[2]
    n_pages = page_table.shape[1]
    past_len = n_pages * page
    fox_heads = b_fgate.shape[1]
    n_kv = cache_fox_kv.shape[4]
    n_groups = cache_nsa_kv.shape[4]
    nsa_heads = n_groups * REP
    assert fox_heads == n_kv * REP and w_out.shape[1] == (fox_heads + nsa_heads) * HEAD_DIM
    fq_w, nq_w = fox_heads * HEAD_DIM, nsa_heads * HEAD_DIM
    fkv_w, nkv_w = n_kv * HEAD_DIM, n_groups * HEAD_DIM
    n_gate = 3 * nsa_heads
    assert fox_heads + n_gate <= LANE and seq % K_TILE == 0 and seq >= WINDOW + Q_TILE and 3 * REP <= HEAD_DIM

    sizes = [fq_w, fkv_w, fkv_w, fox_heads, nq_w] + [nkv_w] * 6 + [n_gate]
    off = np.concatenate([[0], np.cumsum(sizes)]).astype(int)
    o_fq, o_fk, o_fv, o_fl, o_nq, o_kc, o_vc, o_ks, o_vs, o_kw, o_vw, o_g = [int(v) for v in off[:-1]]
    c_fq, c_nq = 0, fq_w
    c_fk = c_nq + nq_w
    c_fv = c_fk + fkv_w
    c_kc = c_fv + fkv_w
    c_vc, c_ks, c_vs, c_kw, c_vw = (c_kc + nkv_w * k for k in range(1, 6))
    n_main = c_vw + nkv_w
    tn = min(512, fkv_w, nkv_w)
    rope_ranges = ((c_nq, c_nq + nq_w), (c_kc, c_kc + nkv_w), (c_ks, c_ks + nkv_w), (c_kw, c_kw + nkv_w))

    nb_p = seq // BLOCK
    nbp_p = max(LANE, nb_p)
    nb_s = -(-(past_len + t_new) // BLOCK)
    nbp_s = LANE
    assert nb_s == 2 * n_pages + 1 and nb_s <= LANE and nbp_p % LANE == 0
    n_win_p = min(WINDOW, seq)
    nq_tiles = seq // Q_TILE

    cos_p, sin_p = _rope_tables(jnp.arange(seq))
    cos_s, sin_s = _rope_tables(jnp.tile(past_len + jnp.arange(t_new), n_seq))
    expand_t = jnp.asarray((np.arange(seq)[:, None] // BLOCK) == np.arange(nbp_p)[None, :], BF16)
    aug_np = np.zeros((HEAD_DIM, REP * Q_TILE), np.float32)
    for r in range(REP):
        aug_np[3 * r:3 * r + 3, r * Q_TILE:(r + 1) * Q_TILE] = 1.0
    aug = jnp.asarray(aug_np, BF16)
    pt_flat = page_table.reshape(-1).astype(jnp.int32)

    hp = x_prompt.reshape(seq, d_model)
    hs = x_sample.reshape(n_seq * t_new, d_model)
    outs_p, outs_s = [], []
    for i in range(depth):
        wi = w_in[i]
        w_main = jnp.concatenate(
            [wi[:, o_fq:o_fq + fq_w], wi[:, o_nq:o_nq + nq_w], wi[:, o_fk:o_fl], wi[:, o_kc:o_g]], axis=1).astype(BF16)
        w_small = jnp.concatenate(
            [wi[:, o_fl:o_fl + fox_heads], wi[:, o_g:o_g + n_gate],
             jnp.zeros((d_model, LANE - fox_heads - n_gate), F32)], axis=1).astype(BF16)
        b_small = jnp.concatenate([b_fgate[i], jnp.zeros((LANE - fox_heads,), F32)]).reshape(1, LANE)
        w_o = w_out[i].astype(BF16)
        w_oa, w_ob = w_o[:fq_w], w_o[fq_w:]
        w_g, w_u, w_d = w_gate[i].astype(BF16), w_up[i].astype(BF16), w_down[i].astype(BF16)
        w_pg, w_pp = w_ple_gate[i].astype(BF16), w_ple_proj[i].astype(BF16)
        pe = jnp.stack([cmp_pe_k[i], cmp_pe_v[i]])
        hidden = cmp_w1_k.shape[-1]
        w1 = jnp.stack([cmp_w1_k[i], cmp_w1_v[i]]).reshape(2, BLOCK, HEAD_DIM, hidden).astype(BF16)
        w1_pairs = w1.reshape(2, BLOCK // 2, 2 * HEAD_DIM, hidden)
        w2 = jnp.stack([cmp_w2_k[i], cmp_w2_v[i]]).astype(BF16)

        def tail(h, o_fox, o_nsa, p):
            h1 = outproj(o_fox, o_nsa, w_oa, w_ob, h)
            act = gateup(rmsnorm(h1, g_ffn[i], BF16), w_g, w_u)
            h2 = downproj(act, w_d, h1)
            return ple(rmsnorm(h2, g_ple[i], BF16), w_pg, p.astype(BF16), w_pp, h2)

        xn = rmsnorm(hp, g_mix[i], BF16)
        pf, pb = inproj(xn, w_main, cos_p, sin_p, tn, rope_ranges)
        small = smallproj(xn, w_small, b_small, fox_heads)
        logf = small[:, :fox_heads]
        f_cum, fp1, fp2, fp3 = cumsum_rows(logf)
        pieces = jnp.stack([fp1, fp2, fp3], axis=-1)
        pieces = pieces.reshape(seq, n_kv, REP * 3).transpose(1, 0, 2)
        k_fox = pb[:, c_fk:c_fk + fkv_w].reshape(seq, n_kv, HEAD_DIM).transpose(1, 0, 2)
        k_aug = jnp.concatenate([k_fox, pieces, jnp.zeros((n_kv, seq, HEAD_DIM - REP * 3), BF16)], axis=-1)
        o_fox = fox_prompt(_queries_t(pb[:, c_fq:c_fq + fq_w], n_kv), aug, k_aug,
                           _values_t(pb[:, c_fv:c_fv + fkv_w], n_kv), _head_lanes(f_cum, n_kv, 1))
        cmp_p = compress_prompt(pf, (c_kc // HEAD_DIM, c_vc // HEAD_DIM), pe, w1, w2, n_groups, nbp_p)
        cmp_vt = cmp_p[1].transpose(0, 2, 1).astype(BF16)
        gate_lanes = _head_lanes(small[:, fox_heads:fox_heads + n_gate], n_groups, 3)
        gate_lanes = jnp.pad(gate_lanes, ((0, 0), (0, 0), (0, 5), (0, 0)))
        o_nsa = nsa_prompt(_queries_t(pb[:, c_nq:c_nq + nq_w], n_groups), pb, c_ks // HEAD_DIM, c_kw // HEAD_DIM,
                           _values_t(pb[:, c_vs:c_vs + nkv_w], n_groups), _values_t(pb[:, c_vw:c_vw + nkv_w], n_groups),
                           cmp_p[0], cmp_vt, expand_t, gate_lanes, min(SEL_TOP_N, nb_p))
        hp = tail(hp, o_fox, o_nsa, p_prompt[i].reshape(seq, -1))
        outs_p.append((
            pf[:, c_fk:c_fk + 2 * fkv_w].reshape(1, seq, 2, n_kv, HEAD_DIM),
            logf.reshape(1, seq, fox_heads),
            pf[:, c_kc:c_kc + 4 * nkv_w].reshape(1, seq, 4, n_groups, HEAD_DIM),
            pf[seq - n_win_p:, c_kw:c_kw + 2 * nkv_w].reshape(1, n_win_p, 2, n_groups, HEAD_DIM),
        ))

        m_s = n_seq * t_new
        xs = rmsnorm(hs, g_mix[i], BF16)
        sf, sb = inproj(xs, w_main, cos_s, sin_s, tn, rope_ranges)
        small_s = smallproj(xs, w_small, b_small, fox_heads)
        logf_s = small_s[:, :fox_heads]
        logf_pool_t = cache_fox_logf[i].transpose(0, 2, 1)
        new_t = jnp.pad(logf_s.reshape(n_seq, t_new, fox_heads).transpose(0, 2, 1),
                        ((0, 0), (0, 0), (0, LANE - t_new)))
        f_all_t = cumsum_pages(pt_flat, logf_pool_t, new_t, n_seq, n_pages)
        fq_s = f_all_t[:, :, past_len:past_len + t_new].transpose(0, 2, 1)
        assert c_fk % (2 * fkv_w) == 0 and c_kc % nkv_w == 0 and c_ks % (2 * nkv_w) == 0 and c_kw % (2 * nkv_w) == 0
        q_fox_s = _heads_major(sb[:, c_fq:c_fq + fq_w], n_seq, t_new, n_kv)
        o_fox_s = fox_sample(pt_flat, cache_fox_kv, i, q_fox_s, sf, c_fk // (2 * fkv_w), fq_s, f_all_t, n_seq, n_pages)
        cmp_raw = compress_sample(pt_flat, cache_nsa_kv, i, sf, c_kc // nkv_w, pe, w1_pairs, w2, n_seq, n_pages, nb_s)
        cmp_s = cmp_raw[:, :, :nb_s * n_groups].reshape(n_seq, 2, nb_s, n_groups, HEAD_DIM).transpose(0, 1, 3, 2, 4)
        cmp_s = jnp.pad(cmp_s, ((0, 0), (0, 0), (0, 0), (0, nbp_s - nb_s), (0, 0)))
        q_nsa_s = _heads_major(sb[:, c_nq:c_nq + nq_w], n_seq, t_new, n_groups)
        gates_s = _heads_major(small_s[:, fox_heads:fox_heads + n_gate], n_seq, t_new, n_groups)
        gates_s = jnp.pad(gates_s, ((0, 0), (0, 0), (0, 0), (0, 13)))
        o_nsa_s = nsa_sample(pt_flat, cache_nsa_kv, state_nsa_win, i, q_nsa_s, cmp_s, sf, c_ks // (2 * nkv_w),
                             c_kw // (2 * nkv_w), gates_s, n_seq, n_pages, min(SEL_TOP_N, nb_s), past_len)
        new_win = win_shift(state_nsa_win, i, sf, c_kw // (2 * nkv_w))
        hs = tail(hs, _tokens_major(o_fox_s, n_seq, t_new, n_kv), _tokens_major(o_nsa_s, n_seq, t_new, n_groups),
                  p_sample[i].reshape(m_s, -1))
        outs_s.append((
            sf[:, c_fk:c_fk + 2 * fkv_w].reshape(n_seq, t_new, 2, n_kv, HEAD_DIM),
            logf_s.reshape(n_seq, t_new, fox_heads),
            sf[:, c_kc:c_kc + 4 * nkv_w].reshape(n_seq, t_new, 4, n_groups, HEAD_DIM),
            new_win,
        ))

    y_prompt = rmsnorm(hp, g_final, F32).reshape(x_prompt.shape)
    y_sample = rmsnorm(hs, g_final, F32).reshape(x_sample.shape)
    stack = lambda outs, j: jnp.stack([r[j] for r in outs], axis=0)
    return (y_prompt, y_sample, stack(outs_p, 0), stack(outs_p, 1), stack(outs_p, 2), stack(outs_p, 3),
            stack(outs_s, 0), stack(outs_s, 1), stack(outs_s, 2), stack(outs_s, 3))
```

```python
import functools

import numpy as np
import jax
import jax.numpy as jnp
from jax import lax
from jax.experimental import pallas as pl
from jax.experimental.pallas import tpu as pltpu

HEAD_DIM = 128
REP = 4
BLOCK = 64
SEL_TOP_N = 16
WINDOW = 512
Q_TILE = 512
K_TILE = 512
V_ROWS = 144
ROPE_THETA = 10000.0
RMS_EPS = 1e-6
ATTN_SCALE = HEAD_DIM ** -0.5
LOG2E = 1.4426950408889634
SCALE2 = ATTN_SCALE * LOG2E
NEG_INF = -1e30
SKIP_MARGIN = 40.0
REMOVED = -3e38
FORCE_BONUS = 1e4
LANE = 128
MIB = 1024 * 1024

F32 = jnp.float32
BF16 = jnp.bfloat16


def _params(sem, vmem_mib):
    return pltpu.CompilerParams(dimension_semantics=sem, vmem_limit_bytes=vmem_mib * MIB)


def _dot(a, b):
    return jnp.dot(a, b, preferred_element_type=F32)


def _dot_nt(a, b):
    return lax.dot_general(a, b, (((1,), (1,)), ((), ())), preferred_element_type=F32)


def _row_tile(m, cap):
    t = min(m, cap)
    assert m % t == 0
    return t


def _iota(shape, dim):
    return lax.broadcasted_iota(jnp.int32, shape, dim)


def _rmsnorm_kernel(x_ref, g_ref, o_ref):
    x = x_ref[...]
    y = x * lax.rsqrt(jnp.mean(x * x, axis=-1, keepdims=True) + RMS_EPS)
    o_ref[...] = (y * g_ref[...]).astype(o_ref.dtype)


def rmsnorm(x, g, out_dtype):
    m, d = x.shape
    tm = _row_tile(m, 256)
    return pl.pallas_call(
        _rmsnorm_kernel,
        grid=(m // tm,),
        in_specs=[pl.BlockSpec((tm, d), lambda i: (i, 0)), pl.BlockSpec((1, d), lambda i: (0, 0))],
        out_specs=pl.BlockSpec((tm, d), lambda i: (i, 0)),
        out_shape=jax.ShapeDtypeStruct((m, d), out_dtype),
        compiler_params=_params(("parallel",), 32),
        name="rmsnorm",
    )(x, g.reshape(1, d))


def _inproj_kernel(x_ref, w_ref, cos_ref, sin_ref, of_ref, ob_ref, *, tn, rope_ranges):
    acc = _dot(x_ref[...], w_ref[...])
    col0 = pl.program_id(1) * tn
    is_rope = None
    for lo, hi in rope_ranges:
        hit = (col0 >= lo) & (col0 < hi)
        is_rope = hit if is_rope is None else (is_rope | hit)

    @pl.when(is_rope)
    def _():
        cos = cos_ref[...]
        sin = sin_ref[...]
        for h in range(tn // HEAD_DIM):
            sl = slice(h * HEAD_DIM, (h + 1) * HEAD_DIM)
            xh = acc[:, sl]
            r = xh * cos + pltpu.roll(xh, HEAD_DIM // 2, 1) * sin
            of_ref[:, sl] = r
            ob_ref[:, sl] = r.astype(BF16)

    @pl.when(jnp.logical_not(is_rope))
    def _():
        of_ref[...] = acc
        ob_ref[...] = acc.astype(BF16)


def inproj(xn, w_main, cos, sin, tn, rope_ranges):
    m, k = xn.shape
    n = w_main.shape[1]
    tm = _row_tile(m, 1024)
    return pl.pallas_call(
        functools.partial(_inproj_kernel, tn=tn, rope_ranges=rope_ranges),
        grid=(m // tm, n // tn),
        in_specs=[
            pl.BlockSpec((tm, k), lambda i, j: (i, 0)),
            pl.BlockSpec((k, tn), lambda i, j: (0, j)),
            pl.BlockSpec((tm, HEAD_DIM), lambda i, j: (i, 0)),
            pl.BlockSpec((tm, HEAD_DIM), lambda i, j: (i, 0)),
        ],
        out_specs=[pl.BlockSpec((tm, tn), lambda i, j: (i, j)), pl.BlockSpec((tm, tn), lambda i, j: (i, j))],
        out_shape=[jax.ShapeDtypeStruct((m, n), F32), jax.ShapeDtypeStruct((m, n), BF16)],
        compiler_params=_params(("parallel", "arbitrary"), 48),
        name="inproj",
    )(xn, w_main, cos, sin)


def _smallproj_kernel(x_ref, w_ref, b_ref, o_ref, *, n_logf):
    v = _dot(x_ref[...], w_ref[...])
    z = v + b_ref[...]
    logf = -(jnp.maximum(-z, 0.0) + jnp.log1p(jnp.exp(-jnp.abs(z))))
    gate = jax.nn.sigmoid(v)
    lane = _iota(v.shape, 1)
    o_ref[...] = jnp.where(lane < n_logf, logf, gate)


def smallproj(xn, w_small, b_small, n_logf):
    m, k = xn.shape
    tm = _row_tile(m, 1024)
    return pl.pallas_call(
        functools.partial(_smallproj_kernel, n_logf=n_logf),
        grid=(m // tm,),
        in_specs=[
            pl.BlockSpec((tm, k), lambda i: (i, 0)),
            pl.BlockSpec((k, LANE), lambda i: (0, 0)),
            pl.BlockSpec((1, LANE), lambda i: (0, 0)),
        ],
        out_specs=pl.BlockSpec((tm, LANE), lambda i: (i, 0)),
        out_shape=jax.ShapeDtypeStruct((m, LANE), F32),
        compiler_params=_params(("parallel",), 32),
        name="smallproj",
    )(xn, w_small, b_small)


def _outproj_kernel(a_ref, b_ref, wa_ref, wb_ref, h_ref, o_ref):
    o_ref[...] = h_ref[...] + (_dot(a_ref[...], wa_ref[...]) + _dot(b_ref[...], wb_ref[...]))


def outproj(o_fox, o_nsa, w_a, w_b, h):
    m, ka = o_fox.shape
    kb = o_nsa.shape[1]
    n = w_a.shape[1]
    tm, tn = _row_tile(m, 1024), 512
    return pl.pallas_call(
        _outproj_kernel,
        grid=(m // tm, n // tn),
        in_specs=[
            pl.BlockSpec((tm, ka), lambda i, j: (i, 0)),
            pl.BlockSpec((tm, kb), lambda i, j: (i, 0)),
            pl.BlockSpec((ka, tn), lambda i, j: (0, j)),
            pl.BlockSpec((kb, tn), lambda i, j: (0, j)),
            pl.BlockSpec((tm, tn), lambda i, j: (i, j)),
        ],
        out_specs=pl.BlockSpec((tm, tn), lambda i, j: (i, j)),
        out_shape=jax.ShapeDtypeStruct((m, n), F32),
        compiler_params=_params(("parallel", "arbitrary"), 48),
        name="outproj",
    )(o_fox, o_nsa, w_a, w_b, h)


def _gateup_kernel(x_ref, wg_ref, wu_ref, o_ref):
    x = x_ref[...]
    g = _dot(x, wg_ref[...])
    u = _dot(x, wu_ref[...])
    o_ref[...] = (g * jax.nn.sigmoid(g) * u).astype(o_ref.dtype)


def gateup(xn, w_gate, w_up):
    m, k = xn.shape
    n = w_gate.shape[1]
    tm, tn = _row_tile(m, 1024), 256
    return pl.pallas_call(
        _gateup_kernel,
        grid=(m // tm, n // tn),
        in_specs=[
            pl.BlockSpec((tm, k), lambda i, j: (i, 0)),
            pl.BlockSpec((k, tn), lambda i, j: (0, j)),
            pl.BlockSpec((k, tn), lambda i, j: (0, j)),
        ],
        out_specs=pl.BlockSpec((tm, tn), lambda i, j: (i, j)),
        out_shape=jax.ShapeDtypeStruct((m, n), BF16),
        compiler_params=_params(("parallel", "arbitrary"), 48),
        name="gateup",
    )(xn, w_gate, w_up)


def _down_kernel(x_ref, w_ref, h_ref, o_ref):
    o_ref[...] = h_ref[...] + _dot(x_ref[...], w_ref[...])


def downproj(act, w_down, h):
    m, k = act.shape
    n = w_down.shape[1]
    tm, tn = _row_tile(m, 512), 512
    return pl.pallas_call(
        _down_kernel,
        grid=(m // tm, n // tn),
        in_specs=[
            pl.BlockSpec((tm, k), lambda i, j: (i, 0)),
            pl.BlockSpec((k, tn), lambda i, j: (0, j)),
            pl.BlockSpec((tm, tn), lambda i, j: (i, j)),
        ],
        out_specs=pl.BlockSpec((tm, tn), lambda i, j: (i, j)),
        out_shape=jax.ShapeDtypeStruct((m, n), F32),
        compiler_params=_params(("parallel", "arbitrary"), 58),
        name="downproj",
    )(act, w_down, h)


def _ple_kernel(x_ref, wg_ref, p_ref, wp_ref, h_ref, o_ref):
    gate = jax.nn.sigmoid(_dot(x_ref[...], wg_ref[...]))
    o_ref[...] = h_ref[...] + gate * _dot(p_ref[...], wp_ref[...])


def ple(xn, w_gate, p, w_proj, h):
    m, k = xn.shape
    kp = p.shape[1]
    n = w_gate.shape[1]
    tm, tn = _row_tile(m, 1024), 512
    return pl.pallas_call(
        _ple_kernel,
        grid=(m // tm, n // tn),
        in_specs=[
            pl.BlockSpec((tm, k), lambda i, j: (i, 0)),
            pl.BlockSpec((k, tn), lambda i, j: (0, j)),
            pl.BlockSpec((tm, kp), lambda i, j: (i, 0)),
            pl.BlockSpec((kp, tn), lambda i, j: (0, j)),
            pl.BlockSpec((tm, tn), lambda i, j: (i, j)),
        ],
        out_specs=pl.BlockSpec((tm, tn), lambda i, j: (i, j)),
        out_shape=jax.ShapeDtypeStruct((m, n), F32),
        compiler_params=_params(("parallel", "arbitrary"), 48),
        name="ple",
    )(xn, w_gate, p, w_proj, h)


def _split3(x):
    x1 = x.astype(BF16)
    r1 = x - x1.astype(F32)
    x2 = r1.astype(BF16)
    x3 = (r1 - x2.astype(F32)).astype(BF16)
    return x1, x2, x3


def _cumsum_rows_kernel(x_ref, tri_ref, o_ref, p1_ref, p2_ref, p3_ref, carry_ref):
    @pl.when(pl.program_id(0) == 0)
    def _():
        carry_ref[...] = jnp.zeros_like(carry_ref)

    x1, x2, x3 = _split3(x_ref[...])
    tri = tri_ref[...]
    out = (_dot(tri, x1) + _dot(tri, x2) + _dot(tri, x3)) + carry_ref[...]
    o_ref[...] = out
    carry_ref[...] = out[out.shape[0] - 1:, :]
    p1_ref[...], p2_ref[...], p3_ref[...] = _split3(out * (-1.0 / ATTN_SCALE))


def cumsum_rows(x):
    t, h = x.shape
    c = _row_tile(t, 512)
    tri = jnp.asarray(np.tril(np.ones((c, c), np.float32)), BF16)
    row_spec = pl.BlockSpec((c, h), lambda i: (i, 0))
    return pl.pallas_call(
        _cumsum_rows_kernel,
        grid=(t // c,),
        in_specs=[row_spec, pl.BlockSpec((c, c), lambda i: (0, 0))],
        out_specs=[row_spec] * 4,
        out_shape=[jax.ShapeDtypeStruct((t, h), F32)] + [jax.ShapeDtypeStruct((t, h), BF16)] * 3,
        scratch_shapes=[pltpu.VMEM((1, h), F32)],
        compiler_params=_params(("arbitrary",), 32),
        name="cumsum_rows",
    )(x, tri)


def _cumsum_pages_kernel(pt_ref, *refs, n_pages):
    page_refs = refs[:n_pages]
    new_ref, tri_ref, o_ref = refs[n_pages:]
    tri = tri_ref[...]
    h = new_ref.shape[0]
    carry = jnp.zeros((h, 1), F32)
    for p in range(n_pages + 1):
        x = page_refs[p][...] if p < n_pages else new_ref[...]
        x1, x2, x3 = _split3(x)
        out = (_dot(x1, tri) + _dot(x2, tri) + _dot(x3, tri)) + carry
        o_ref[:, p * LANE:(p + 1) * LANE] = out
        carry = out[:, LANE - 1:]


def cumsum_pages(pt_flat, logf_pool_t, new_t, n_seq, n_pages):
    h = logf_pool_t.shape[1]
    tri = jnp.asarray(np.triu(np.ones((LANE, LANE), np.float32)), BF16)
    page_spec = lambda p: pl.BlockSpec((None, h, LANE), lambda b, pt: (pt[b * n_pages + p], 0, 0))
    grid_spec = pltpu.PrefetchScalarGridSpec(
        num_scalar_prefetch=1,
        grid=(n_seq,),
        in_specs=[page_spec(p) for p in range(n_pages)] + [
            pl.BlockSpec((None, h, LANE), lambda b, pt: (b, 0, 0)),
            pl.BlockSpec((LANE, LANE), lambda b, pt: (0, 0)),
        ],
        out_specs=pl.BlockSpec((None, h, (n_pages + 1) * LANE), lambda b, pt: (b, 0, 0)),
    )
    return pl.pallas_call(
        functools.partial(_cumsum_pages_kernel, n_pages=n_pages),
        grid_spec=grid_spec,
        out_shape=jax.ShapeDtypeStruct((n_seq, h, (n_pages + 1) * LANE), F32),
        compiler_params=_params(("arbitrary",), 32),
        name="cumsum_pages",
    )(pt_flat, *([logf_pool_t] * n_pages), new_t, tri)


def _silu(x):
    return x * jax.nn.sigmoid(x)


def _compress_prompt_kernel(cb_ref, x_ref, pe_ref, w1_ref, w2_ref, o_ref, *, nb):
    hidden = w1_ref.shape[2]

    def body(l, acc):
        xl = x_ref[pl.ds(l, nb, stride=BLOCK), :] + pe_ref[pl.ds(l, 1), :]
        return acc + _dot(xl.astype(BF16), w1_ref[l])

    acc = lax.fori_loop(0, BLOCK, body, jnp.zeros((nb, hidden), F32))
    out = _dot(_silu(acc).astype(BF16), w2_ref[...])
    o_ref[...] = jnp.zeros_like(o_ref)
    o_ref[0:nb, :] = out


def compress_prompt(proj_f32, col_blocks, pe, w1, w2, n_groups, nbp):
    t = proj_f32.shape[0]
    nb = t // BLOCK
    hidden = w1.shape[-1]
    cb = jnp.asarray(col_blocks, jnp.int32)
    grid_spec = pltpu.PrefetchScalarGridSpec(
        num_scalar_prefetch=1,
        grid=(2, n_groups),
        in_specs=[
            pl.BlockSpec((t, HEAD_DIM), lambda kv, g, cb: (0, cb[kv] + g)),
            pl.BlockSpec((None, BLOCK, HEAD_DIM), lambda kv, g, cb: (kv, 0, 0)),
            pl.BlockSpec((None, BLOCK, HEAD_DIM, hidden), lambda kv, g, cb: (kv, 0, 0, 0)),
            pl.BlockSpec((None, hidden, HEAD_DIM), lambda kv, g, cb: (kv, 0, 0)),
        ],
        out_specs=pl.BlockSpec((None, None, nbp, HEAD_DIM), lambda kv, g, cb: (kv, g, 0, 0)),
    )
    return pl.pallas_call(
        functools.partial(_compress_prompt_kernel, nb=nb),
        grid_spec=grid_spec,
        out_shape=jax.ShapeDtypeStruct((2, n_groups, nbp, HEAD_DIM), F32),
        compiler_params=_params(("arbitrary", "arbitrary"), 40),
        name="compress_prompt",
    )(cb, proj_f32, pe, w1, w2)


def _compress_sample_kernel(pt_ref, *refs, n_pages, n_groups, nb):
    pages = refs[:n_pages]
    new_ref, pe_ref, w1_ref, w2_ref, o_ref, newblk_ref = refs[n_pages:]
    t_new = new_ref.shape[0]
    hidden = w1_ref.shape[2]
    rows_pad = o_ref.shape[0]
    n_rows = (2 * n_pages + 1) * n_groups

    newblk_ref[...] = jnp.zeros_like(newblk_ref)
    for l in range(t_new):
        for g in range(n_groups):
            newblk_ref[l, g:g + 1, :] = new_ref[l:l + 1, g * HEAD_DIM:(g + 1) * HEAD_DIM]

    def gather(l):
        rr = []
        for pg in pages:
            rr.append(pg[l])
            rr.append(pg[l + BLOCK])
        rr.append(newblk_ref[l])
        if rows_pad > n_rows:
            rr.append(jnp.zeros((rows_pad - n_rows, HEAD_DIM), F32))
        return jnp.concatenate(rr, axis=0) + pe_ref[l:l + 1, :]

    acc = jnp.zeros((rows_pad, hidden), F32)
    for l2 in range(BLOCK // 2):
        x = jnp.concatenate([gather(2 * l2), gather(2 * l2 + 1)], axis=1).astype(BF16)
        acc = acc + _dot(x, w1_ref[l2])
    out = _dot(_silu(acc).astype(BF16), w2_ref[...])
    o_ref[...] = jnp.where(_iota(out.shape, 0) < nb * n_groups, out, 0.0)


def compress_sample(pt_flat, nsa_cache, layer, new_rows, new_col_block, pe, w1_pairs, w2, n_seq, n_pages, nb):
    page, _, n_groups, _ = nsa_cache.shape[2:]
    assert page == 2 * BLOCK
    gw = n_groups * HEAD_DIM
    hidden = w1_pairs.shape[-1]
    t_new = new_rows.shape[0] // n_seq
    rows_pad = -(-((2 * n_pages + 1) * n_groups) // 8) * 8
    spec = lambda p: pl.BlockSpec((None, None, page, None, n_groups, HEAD_DIM),
                                  lambda kv, b, pt: (layer, pt[b * n_pages + p], 0, kv, 0, 0))
    grid_spec = pltpu.PrefetchScalarGridSpec(
        num_scalar_prefetch=1,
        grid=(2, n_seq),
        in_specs=[spec(p) for p in range(n_pages)] + [
            pl.BlockSpec((t_new, gw), lambda kv, b, pt: (b, new_col_block + kv)),
            pl.BlockSpec((None, BLOCK, HEAD_DIM), lambda kv, b, pt: (kv, 0, 0)),
            pl.BlockSpec((None, BLOCK // 2, 2 * HEAD_DIM, hidden), lambda kv, b, pt: (kv, 0, 0, 0)),
            pl.BlockSpec((None, hidden, HEAD_DIM), lambda kv, b, pt: (kv, 0, 0)),
        ],
        out_specs=pl.BlockSpec((None, None, rows_pad, HEAD_DIM), lambda kv, b, pt: (b, kv, 0, 0)),
        scratch_shapes=[pltpu.VMEM((BLOCK, n_groups, HEAD_DIM), F32)],
    )
    return pl.pallas_call(
        functools.partial(_compress_sample_kernel, n_pages=n_pages, n_groups=n_groups, nb=nb),
        grid_spec=grid_spec,
        out_shape=jax.ShapeDtypeStruct((n_seq, 2, rows_pad, HEAD_DIM), F32),
        compiler_params=_params(("arbitrary", "arbitrary"), 40),
        name="compress_sample",
    )(pt_flat, *([nsa_cache] * n_pages), new_rows, pe, w1_pairs, w2)


def _softmax_rows(s):
    m = jnp.max(s, axis=-1, keepdims=True)
    e = jnp.exp(s - m)
    return e / jnp.sum(e, axis=-1, keepdims=True)


def _softmax_cols(s):
    m = jnp.max(s, axis=0, keepdims=True)
    e = jnp.exp(s - m)
    return e / jnp.sum(e, axis=0, keepdims=True)


def _top_n_mask_t(score_t, blk_t, n_sel, n_blocks_pad):
    sel = jnp.zeros(score_t.shape, F32)
    for _ in range(n_sel):
        mx = jnp.max(score_t, axis=0, keepdims=True)
        idx = jnp.min(jnp.where(score_t == mx, blk_t, n_blocks_pad), axis=0, keepdims=True)
        hit = blk_t == idx
        sel = jnp.where(hit, 1.0, sel)
        score_t = jnp.where(hit, REMOVED, score_t)
    return sel


def _selection_scores_t(imp_t, blk_t, qpos_t):
    cur = qpos_t // BLOCK
    forced = (blk_t == 0) | (blk_t == cur) | (blk_t == cur - 1)
    avail = blk_t * BLOCK <= qpos_t
    return jnp.where(avail, imp_t + jnp.where(forced, FORCE_BONUS, 0.0), NEG_INF)


def _online_tile_t(x, vt_tile, m, acc_ref, shift):
    mx = jnp.max(x, axis=0, keepdims=True) * SCALE2
    if shift is not None:
        mx = mx + shift
    m_new = jnp.maximum(m, mx)
    off = m_new if shift is None else m_new - shift
    p = jnp.exp2(x * SCALE2 - off)
    acc_ref[...] = jnp.exp2(m - m_new) * acc_ref[...] + _dot(vt_tile, p.astype(BF16))
    return m_new


def _queries_t(q):
    return jnp.concatenate([q[:, r * HEAD_DIM:(r + 1) * HEAD_DIM].T for r in range(REP)], axis=1)


def _store_heads(o_ref, o_t):
    for r in range(REP):
        o_ref[:, r * HEAD_DIM:(r + 1) * HEAD_DIM] = o_t[:, r * Q_TILE:(r + 1) * Q_TILE].T.astype(o_ref.dtype)


def _fox_prompt_kernel(first_ref, q_ref, aug_ref, ka_ref, vt_ref, fq_ref, o_ref, acc_ref):
    q0 = pl.program_id(1) * Q_TILE
    cols = REP * Q_TILE
    qa = jnp.concatenate([_queries_t(q_ref[...]), aug_ref[...]], axis=0)
    fq2 = fq_ref[...] * LOG2E
    acc_ref[...] = jnp.zeros_like(acc_ref)
    n_full = q0 // K_TILE

    def scores(j):
        k0 = pl.multiple_of(j * K_TILE, K_TILE)
        return _dot(ka_ref[pl.ds(k0, K_TILE), :], qa)

    def update(j, x, m):
        k0 = pl.multiple_of(j * K_TILE, K_TILE)
        return _online_tile_t(x, vt_ref[:, pl.ds(k0, K_TILE)], m, acc_ref, fq2)

    first = first_ref[pl.program_id(0) * pl.num_programs(1) + pl.program_id(1)]
    m = lax.fori_loop(first, n_full, lambda j, m: update(j, scores(j), m), jnp.full((1, cols), NEG_INF, F32))
    kpos = n_full * K_TILE + _iota((K_TILE, cols), 0)
    qpos = q0 + (_iota((K_TILE, cols), 1) & (Q_TILE - 1))
    update(n_full, jnp.where(kpos <= qpos, scores(n_full), NEG_INF), m)
    acc = acc_ref[...]
    _store_heads(o_ref, acc[0:HEAD_DIM] / acc[HEAD_DIM:HEAD_DIM + 1])


def fox_first_tiles(q, k, f_cum, n_kv):
    assert Q_TILE == K_TILE
    t = q.shape[0]
    nt = t // K_TILE
    qn = jnp.sum(jnp.square(q.astype(F32)).reshape(nt, Q_TILE, n_kv, REP, HEAD_DIM), axis=-1)
    qmax = jnp.sqrt(jnp.max(qn, axis=(1, 3)))
    kn = jnp.sum(jnp.square(k.astype(F32)).reshape(nt, K_TILE, n_kv, HEAD_DIM), axis=-1)
    kmax = jnp.sqrt(jnp.max(kn, axis=1))
    f_first = f_cum[0::K_TILE].reshape(nt, n_kv, REP)
    f_last = f_cum[K_TILE - 1::K_TILE].reshape(nt, n_kv, REP)
    gap = jnp.max(f_first[:, None] - f_last[None, :], axis=-1)
    bound = 1.01 * ATTN_SCALE * qmax[:, None] * (kmax[None, :] + kmax[:, None]) + gap
    earlier = jnp.arange(nt)[None, :, None] < jnp.arange(nt)[:, None, None]
    skippable = (bound < -SKIP_MARGIN) & earlier
    first = jnp.sum(jnp.cumprod(skippable.astype(jnp.int32), axis=1), axis=1)
    return first.T.reshape(-1).astype(jnp.int32)


def fox_prompt(first_tiles, proj_bf16, q_blk0, aug, k_aug, v_t, fq_lanes):
    n_kv, _, t = v_t.shape
    cols = REP * Q_TILE
    grid_spec = pltpu.PrefetchScalarGridSpec(
        num_scalar_prefetch=1,
        grid=(n_kv, t // Q_TILE),
        in_specs=[
            pl.BlockSpec((Q_TILE, REP * HEAD_DIM), lambda g, i, first: (i, q_blk0 + g)),
            pl.BlockSpec((HEAD_DIM, cols), lambda g, i, first: (0, 0)),
            pl.BlockSpec((None, t, 2 * HEAD_DIM), lambda g, i, first: (g, 0, 0)),
            pl.BlockSpec((None, V_ROWS, t), lambda g, i, first: (g, 0, 0)),
            pl.BlockSpec((None, None, 1, cols), lambda g, i, first: (g, i, 0, 0)),
        ],
        out_specs=pl.BlockSpec((Q_TILE, REP * HEAD_DIM), lambda g, i, first: (i, g)),
        scratch_shapes=[pltpu.VMEM((V_ROWS, cols), F32)],
    )
    return pl.pallas_call(
        _fox_prompt_kernel,
        grid_spec=grid_spec,
        out_shape=jax.ShapeDtypeStruct((t, n_kv * REP * HEAD_DIM), BF16),
        compiler_params=_params(("arbitrary", "arbitrary"), 40),
        name="fox_prompt",
    )(first_tiles, proj_bf16, aug, k_aug, v_t, fq_lanes)


def _nsa_prompt_kernel(q_ref, ks_ref, vst_ref, kw_ref, vwt_ref, ck_ref, cvt_ref, et_ref, gate_ref, wmask_ref, o_ref,
                       acc_ref, *, n_sel):
    q0 = pl.program_id(1) * Q_TILE
    cols = REP * Q_TILE
    nbp = ck_ref.shape[0]
    qt = _queries_t(q_ref[...])

    x_c = _dot(ck_ref[...].astype(BF16), qt) * ATTN_SCALE
    blk = _iota((nbp, cols), 0)
    qpos = q0 + (_iota((nbp, cols), 1) & (Q_TILE - 1))
    x_c = jnp.where((blk + 1) * BLOCK - 1 <= qpos, x_c, NEG_INF)
    p_c = _softmax_cols(x_c) * (qpos >= BLOCK - 1).astype(F32)
    o_cmp = _dot(cvt_ref[...], p_c.astype(BF16))

    imp_t = p_c[:, 0:Q_TILE]
    for r in range(1, REP):
        imp_t = imp_t + p_c[:, r * Q_TILE:(r + 1) * Q_TILE]
    blk_t = _iota((nbp, Q_TILE), 0)
    qpos_t = q0 + _iota((nbp, Q_TILE), 1)
    sel_t = _top_n_mask_t(_selection_scores_t(imp_t, blk_t, qpos_t), blk_t, n_sel, nbp).astype(BF16)

    acc_ref[...] = jnp.zeros_like(acc_ref)
    n_full = q0 // K_TILE

    def tile(j, m, masked):
        k0 = pl.multiple_of(j * K_TILE, K_TILE)
        x = _dot(ks_ref[pl.ds(k0, K_TILE), :], qt)
        picked = _dot(et_ref[pl.ds(k0, K_TILE), :], sel_t)
        bias = (picked - 1.0) * -NEG_INF
        if masked:
            kpos = k0 + _iota((K_TILE, Q_TILE), 0)
            bias = jnp.where(kpos <= q0 + _iota((K_TILE, Q_TILE), 1), bias, NEG_INF)
        x = x + jnp.concatenate([bias] * REP, axis=1)
        return _online_tile_t(x, vst_ref[:, pl.ds(k0, K_TILE)], m, acc_ref, None)

    m = lax.fori_loop(0, n_full, lambda j, m: tile(j, m, False), jnp.full((1, cols), NEG_INF, F32))
    tile(n_full, m, True)
    acc = acc_ref[...]
    o_slc = acc[0:HEAD_DIM] / acc[HEAD_DIM:HEAD_DIM + 1]

    wlen = WINDOW + Q_TILE
    w0 = pl.multiple_of(jnp.maximum(q0 - WINDOW, 0), Q_TILE)
    x_w = _dot(kw_ref[pl.ds(w0, wlen), :], qt) * SCALE2
    x_w = x_w + jnp.concatenate([wmask_ref[...]] * REP, axis=1)
    p_w = jnp.exp2(x_w - jnp.max(x_w, axis=0, keepdims=True))
    a_w = _dot(vwt_ref[:, pl.ds(w0, wlen)], p_w.astype(BF16))
    o_win = a_w[0:HEAD_DIM] / a_w[HEAD_DIM:HEAD_DIM + 1]

    gates = gate_ref[...]
    _store_heads(o_ref, gates[0:1] * o_cmp + gates[1:2] * o_slc + gates[2:3] * o_win)


def _window_masks(t):
    n_var = WINDOW // Q_TILE + 1
    k = np.arange(WINDOW + Q_TILE)[None, :, None]
    q = np.arange(Q_TILE)[None, None, :]
    q0 = (np.arange(n_var) * Q_TILE)[:, None, None]
    dist = q0 + q - (np.maximum(q0 - WINDOW, 0) + k)
    return jnp.asarray(np.where((dist >= 0) & (dist < WINDOW), 0.0, NEG_INF), F32)


def nsa_prompt(proj_bf16, q_blk0, ks_blk0, kw_blk0, vs_t, vw_t, cmp_k, cmp_vt, expand_t, gate_lanes, n_sel):
    assert WINDOW % Q_TILE == 0
    wmask = _window_masks(vs_t.shape[2])
    last_var = wmask.shape[0] - 1
    n_groups, _, t = vs_t.shape
    nbp = cmp_k.shape[1]
    cols = REP * Q_TILE
    return pl.pallas_call(
        functools.partial(_nsa_prompt_kernel, n_sel=n_sel),
        grid=(n_groups, t // Q_TILE),
        in_specs=[
            pl.BlockSpec((Q_TILE, REP * HEAD_DIM), lambda g, i: (i, q_blk0 + g)),
            pl.BlockSpec((t, HEAD_DIM), lambda g, i: (0, ks_blk0 + g)),
            pl.BlockSpec((None, V_ROWS, t), lambda g, i: (g, 0, 0)),
            pl.BlockSpec((t, HEAD_DIM), lambda g, i: (0, kw_blk0 + g)),
            pl.BlockSpec((None, V_ROWS, t), lambda g, i: (g, 0, 0)),
            pl.BlockSpec((None, nbp, HEAD_DIM), lambda g, i: (g, 0, 0)),
            pl.BlockSpec((None, HEAD_DIM, nbp), lambda g, i: (g, 0, 0)),
            pl.BlockSpec((t, nbp), lambda g, i: (0, 0)),
            pl.BlockSpec((None, None, 8, cols), lambda g, i: (g, i, 0, 0)),
            pl.BlockSpec((None, WINDOW + Q_TILE, Q_TILE), lambda g, i: (jnp.minimum(i, last_var), 0, 0)),
        ],
        out_specs=pl.BlockSpec((Q_TILE, REP * HEAD_DIM), lambda g, i: (i, g)),
        out_shape=jax.ShapeDtypeStruct((t, n_groups * REP * HEAD_DIM), BF16),
        scratch_shapes=[pltpu.VMEM((V_ROWS, cols), F32)],
        compiler_params=_params(("arbitrary", "arbitrary"), 56),
        name="nsa_prompt",
    )(proj_bf16, proj_bf16, vs_t, proj_bf16, vw_t, cmp_k, cmp_vt, expand_t, gate_lanes, wmask)


def _pad_rows(x, n):
    return jnp.concatenate([x, jnp.zeros((n - x.shape[0], x.shape[1]), x.dtype)], axis=0)


def _copy_rows(block_refs, rows_ref):
    views = []
    for p, r in enumerate(block_refs):
        rows_ref[p] = r[...].reshape(rows_ref.shape[1:])
        views.append(rows_ref.at[p])
    return views


def _fox_sample_kernel(pt_ref, *refs, n_pages, n_kv):
    q_ref, new_ref, fq_ref, fk_ref, o_ref, kv_ref = refs[n_pages:]
    pages = refs[:n_pages]
    stride = 2 * n_kv
    page = pages[0].shape[0] // stride
    t_new = new_ref.shape[0]
    rows = REP * t_new
    past = n_pages * page
    n_keys = kv_ref.shape[1]
    for c in range(stride):
        for p in range(n_pages):
            kv_ref[c, p * page:(p + 1) * page, :] = pages[p][pl.ds(c, page, stride=stride), :].astype(BF16)
        kv_ref[c, past:n_keys, :] = _pad_rows(new_ref[:, c * HEAD_DIM:(c + 1) * HEAD_DIM], n_keys - past).astype(BF16)
    visible = _iota((t_new, n_keys), 1) <= past + _iota((t_new, n_keys), 0)
    for g in range(n_kv):
        s = _dot_nt(q_ref[g], kv_ref[g]) * ATTN_SCALE
        s = s.reshape(REP, t_new, n_keys) - fk_ref[g * REP:(g + 1) * REP, :][:, None, :]
        s = jnp.where(visible[None], s, NEG_INF).reshape(rows, n_keys)
        fq = jnp.concatenate([fq_ref[:, g * REP + r:g * REP + r + 1] for r in range(REP)], axis=0)
        prob = _softmax_rows(s + fq).astype(BF16)
        o_ref[g] = _dot(prob, kv_ref[n_kv + g]).astype(o_ref.dtype)


def fox_sample(pt_flat, fox_cache, layer, q_s, new_rows, new_col_block, fq_s, fk_s, n_seq, n_pages):
    depth, n_pool, page, _, n_kv, _ = fox_cache.shape
    fox_cache = fox_cache.reshape(depth, n_pool, page * 2 * n_kv, HEAD_DIM)
    t_new = new_rows.shape[0] // n_seq
    rows = REP * t_new
    n_heads = fq_s.shape[2]
    assert page == LANE
    spec = lambda p: pl.BlockSpec((None, None, page * 2 * n_kv, HEAD_DIM),
                                  lambda b, pt: (layer, pt[b * n_pages + p], 0, 0))
    grid_spec = pltpu.PrefetchScalarGridSpec(
        num_scalar_prefetch=1,
        grid=(n_seq,),
        in_specs=[spec(p) for p in range(n_pages)] + [
            pl.BlockSpec((None, n_kv, rows, HEAD_DIM), lambda b, pt: (b, 0, 0, 0)),
            pl.BlockSpec((t_new, 2 * n_kv * HEAD_DIM), lambda b, pt: (b, new_col_block)),
            pl.BlockSpec((None, t_new, n_heads), lambda b, pt: (b, 0, 0)),
            pl.BlockSpec((None, n_heads, (n_pages + 1) * page), lambda b, pt: (b, 0, 0)),
        ],
        out_specs=pl.BlockSpec((None, n_kv, rows, HEAD_DIM), lambda b, pt: (b, 0, 0, 0)),
        scratch_shapes=[pltpu.VMEM((2 * n_kv, (n_pages + 1) * page, HEAD_DIM), BF16)],
    )
    return pl.pallas_call(
        functools.partial(_fox_sample_kernel, n_pages=n_pages, n_kv=n_kv),
        grid_spec=grid_spec,
        out_shape=jax.ShapeDtypeStruct((n_seq, n_kv, rows, HEAD_DIM), BF16),
        compiler_params=_params(("arbitrary",), 52),
        name="fox_sample",
    )(pt_flat, *([fox_cache] * n_pages), q_s, new_rows, fq_s, fk_s)


def _nsa_sample_kernel(pt_ref, *refs, n_pages, n_groups, n_sel, past_len):
    (q_ref, cmp_ref, new_slc_ref, new_win_ref, win_ref, gate_ref, o_ref,
     krows_ref, vrows_ref, wrows_ref) = refs[2 * n_pages:]
    page = refs[0].shape[0]
    kpages = _copy_rows(refs[:n_pages], krows_ref)
    vpages = _copy_rows(refs[n_pages:2 * n_pages], vrows_ref)
    t_new = new_slc_ref.shape[0]
    rows = REP * t_new
    gw = n_groups * HEAD_DIM
    nbp = cmp_ref.shape[2]
    n_buf = win_ref.shape[0]
    win_rows = _copy_rows([win_ref], wrows_ref)[0]
    blocks_per_page = page // BLOCK

    qpos = past_len + _iota((t_new, nbp), 0)
    c_mask = (_iota((t_new, nbp), 1) + 1) * BLOCK - 1 <= qpos
    any_vis = (qpos >= BLOCK - 1).astype(F32)
    o_cmp, imps = [], []
    for g in range(n_groups):
        s_c = _dot_nt(q_ref[g], cmp_ref[0, g].astype(BF16)) * ATTN_SCALE
        s_c = jnp.where(c_mask[None], s_c.reshape(REP, t_new, nbp), NEG_INF)
        p_c = _softmax_rows(s_c) * any_vis[None]
        o_cmp.append(_dot(p_c.reshape(rows, nbp).astype(BF16), cmp_ref[1, g].astype(BF16)))
        imp = p_c[0]
        for r in range(1, REP):
            imp = imp + p_c[r]
        imps.append(imp)

    imp_all = _pad_rows(jnp.concatenate(imps, axis=0), LANE)
    blk_t = _iota((nbp, LANE), 0)
    qpos_t = past_len + _iota((nbp, LANE), 1) % t_new
    sel_t = _top_n_mask_t(_selection_scores_t(imp_all.T, blk_t, qpos_t), blk_t, n_sel, nbp)
    sel_all = sel_t.T

    lane_p = _iota((t_new, page), 1)
    row_p = _iota((t_new, page), 0)
    lane_blk = lane_p // BLOCK
    widx = _iota((t_new, n_buf + page), 1)
    wdist = n_buf + _iota((t_new, n_buf + page), 0) - widx
    w_mask = (wdist >= 0) & (wdist < WINDOW) & (widx < n_buf + t_new)
    gates = gate_ref[...]

    for g in range(n_groups):
        q4 = q_ref[g]
        gs = slice(g * HEAD_DIM, (g + 1) * HEAD_DIM)
        vs = slice(gw + g * HEAD_DIM, gw + (g + 1) * HEAD_DIM)
        sel_g = sel_all[g * t_new:(g + 1) * t_new, :]

        chunks = []
        for p in range(n_pages + 1):
            if p < n_pages:
                k_p = kpages[p][pl.ds(g, page, stride=n_groups), :].astype(BF16)
            else:
                k_p = _pad_rows(new_slc_ref[:, gs], page).astype(BF16)
            s = _dot_nt(q4, k_p) * ATTN_SCALE
            picked = jnp.zeros((t_new, page), F32)
            for c in range(blocks_per_page):
                b_idx = p * blocks_per_page + c
                picked = jnp.where(lane_blk == c, sel_g[:, b_idx:b_idx + 1], picked)
            ok = picked > 0.5
            if p == n_pages:
                ok = ok & (lane_p <= row_p)
            chunks.append(jnp.where(ok[None], s.reshape(REP, t_new, page), NEG_INF).reshape(rows, page))
        prob = _softmax_rows(jnp.concatenate(chunks, axis=1)).astype(BF16)
        o_slc = jnp.zeros((rows, HEAD_DIM), F32)
        for p in range(n_pages + 1):
            if p < n_pages:
                v_p = vpages[p][pl.ds(g, page, stride=n_groups), :].astype(BF16)
            else:
                v_p = _pad_rows(new_slc_ref[:, vs], page).astype(BF16)
            o_slc = o_slc + _dot(prob[:, p * page:(p + 1) * page], v_p)

        kw = jnp.concatenate([win_rows[pl.ds(g, n_buf, stride=2 * n_groups), :],
                              _pad_rows(new_win_ref[:, gs], page)], axis=0).astype(BF16)
        vw = jnp.concatenate([win_rows[pl.ds(n_groups + g, n_buf, stride=2 * n_groups), :],
                              _pad_rows(new_win_ref[:, vs], page)], axis=0).astype(BF16)
        s_w = _dot_nt(q4, kw) * ATTN_SCALE
        s_w = jnp.where(w_mask[None], s_w.reshape(REP, t_new, n_buf + page), NEG_INF).reshape(rows, n_buf + page)
        o_win = _dot(_softmax_rows(s_w).astype(BF16), vw)

        gt = gates[g]
        o_ref[g] = (gt[:, 0:1] * o_cmp[g] + gt[:, 1:2] * o_slc + gt[:, 2:3] * o_win).astype(o_ref.dtype)


def nsa_sample(pt_flat, nsa_cache, win_state, layer, q_s, cmp_s, new_rows, slc_col_block, win_col_block, gates_s,
               n_seq, n_pages, n_sel, past_len):
    page, _, n_groups, _ = nsa_cache.shape[2:]
    gw = n_groups * HEAD_DIM
    t_new = new_rows.shape[0] // n_seq
    rows = REP * t_new
    nbp = cmp_s.shape[3]
    n_buf = win_state.shape[2]
    spec = lambda p, slot: pl.BlockSpec((None, None, page, None, n_groups, HEAD_DIM),
                                        lambda b, pt: (layer, pt[b * n_pages + p], 0, slot, 0, 0))
    grid_spec = pltpu.PrefetchScalarGridSpec(
        num_scalar_prefetch=1,
        grid=(n_seq,),
        in_specs=[spec(p, 2) for p in range(n_pages)] + [spec(p, 3) for p in range(n_pages)] + [
            pl.BlockSpec((None, n_groups, rows, HEAD_DIM), lambda b, pt: (b, 0, 0, 0)),
            pl.BlockSpec((None, 2, n_groups, nbp, HEAD_DIM), lambda b, pt: (b, 0, 0, 0, 0)),
            pl.BlockSpec((t_new, 2 * gw), lambda b, pt: (b, slc_col_block)),
            pl.BlockSpec((t_new, 2 * gw), lambda b, pt: (b, win_col_block)),
            pl.BlockSpec((None, None, n_buf, 2, n_groups, HEAD_DIM), lambda b, pt: (layer, b, 0, 0, 0, 0)),
            pl.BlockSpec((None, n_groups, rows, 16), lambda b, pt: (b, 0, 0, 0)),
        ],
        out_specs=pl.BlockSpec((None, n_groups, rows, HEAD_DIM), lambda b, pt: (b, 0, 0, 0)),
        scratch_shapes=[pltpu.VMEM((n_pages, page * n_groups, HEAD_DIM), F32),
                        pltpu.VMEM((n_pages, page * n_groups, HEAD_DIM), F32),
                        pltpu.VMEM((1, n_buf * 2 * n_groups, HEAD_DIM), F32)],
    )
    return pl.pallas_call(
        functools.partial(_nsa_sample_kernel, n_pages=n_pages, n_groups=n_groups, n_sel=n_sel, past_len=past_len),
        grid_spec=grid_spec,
        out_shape=jax.ShapeDtypeStruct((n_seq, n_groups, rows, HEAD_DIM), BF16),
        compiler_params=_params(("arbitrary",), 56),
        name="nsa_sample",
    )(pt_flat, *([nsa_cache] * (2 * n_pages)), q_s, cmp_s, new_rows, new_rows, win_state, gates_s)


def _win_shift_kernel(win_ref, new_ref, o_ref):
    n_buf, _, n_groups, _ = win_ref.shape
    t_new = new_ref.shape[0]
    gw = n_groups * HEAD_DIM
    o_ref[0:n_buf - t_new] = win_ref[t_new:n_buf]
    for s in range(2):
        for g in range(n_groups):
            c0 = s * gw + g * HEAD_DIM
            o_ref[n_buf - t_new:n_buf, s, g, :] = new_ref[:, c0:c0 + HEAD_DIM]


def win_shift(win_state, layer, new_rows, win_col_block):
    _, n_seq, n_buf, _, n_groups, _ = win_state.shape
    t_new = new_rows.shape[0] // n_seq
    assert n_buf > t_new
    blk = (None, n_buf, 2, n_groups, HEAD_DIM)
    return pl.pallas_call(
        _win_shift_kernel,
        grid=(n_seq,),
        in_specs=[
            pl.BlockSpec((None,) + blk, lambda b: (layer, b, 0, 0, 0, 0)),
            pl.BlockSpec((t_new, 2 * n_groups * HEAD_DIM), lambda b: (b, win_col_block)),
        ],
        out_specs=pl.BlockSpec(blk, lambda b: (b, 0, 0, 0, 0)),
        out_shape=jax.ShapeDtypeStruct(win_state.shape[1:], F32),
        compiler_params=_params(("parallel",), 32),
        name="win_shift",
    )(win_state, new_rows)


def _rope_tables(pos):
    half = HEAD_DIM // 2
    inv_freq = ROPE_THETA ** (-jnp.arange(half, dtype=F32) / half)
    ang = pos.astype(F32)[:, None] * inv_freq[None, :]
    cos, sin = jnp.cos(ang), jnp.sin(ang)
    return jnp.concatenate([cos, cos], axis=-1), jnp.concatenate([-sin, sin], axis=-1)


def _heads_major(x, n_seq, t_new, n_groups):
    w = x.shape[1] // (n_groups * REP)
    return x.reshape(n_seq, t_new, n_groups, REP, w).transpose(0, 2, 3, 1, 4).reshape(n_seq, n_groups, REP * t_new, w)


def _tokens_major(x, n_seq, t_new, n_groups):
    w = x.shape[-1]
    return x.reshape(n_seq, n_groups, REP, t_new, w).transpose(0, 3, 1, 2, 4).reshape(n_seq * t_new, n_groups * REP * w)


def _values_t(v, n_groups):
    t = v.shape[0]
    vt = v.T.reshape(n_groups, HEAD_DIM, t)
    return jnp.concatenate([vt, jnp.ones((n_groups, V_ROWS - HEAD_DIM, t), v.dtype)], axis=1)


def _head_lanes(x, n_groups, w):
    t = x.shape[0]
    nq = t // Q_TILE
    x = x.reshape(nq, Q_TILE, n_groups, REP, w).transpose(2, 0, 4, 3, 1)
    return x.reshape(n_groups, nq, w, REP * Q_TILE)


def kernel(x_prompt, x_sample, cache_fox_kv, cache_fox_logf, cache_nsa_kv, state_nsa_win, page_table, p_prompt, p_sample, g_mix, w_in, b_fgate, cmp_pe_k, cmp_w1_k, cmp_w2_k, cmp_pe_v, cmp_w1_v, cmp_w2_v, w_out, g_ffn, w_gate, w_up, w_down, g_ple, w_ple_gate, w_ple_proj, g_final):
    depth = w_in.shape[0]
    _, seq, d_model = x_prompt.shape
    n_seq, t_new, _ = x_sample.shape
    page = cache_fox_kv.shape[2]
    n_pages = page_table.shape[1]
    past_len = n_pages * page
    fox_heads = b_fgate.shape[1]
    n_kv = cache_fox_kv.shape[4]
    n_groups = cache_nsa_kv.shape[4]
    nsa_heads = n_groups * REP
    assert fox_heads == n_kv * REP and w_out.shape[1] == (fox_heads + nsa_heads) * HEAD_DIM
    fq_w, nq_w = fox_heads * HEAD_DIM, nsa_heads * HEAD_DIM
    fkv_w, nkv_w = n_kv * HEAD_DIM, n_groups * HEAD_DIM
    n_gate = 3 * nsa_heads
    assert fox_heads + n_gate <= LANE and seq % K_TILE == 0 and seq >= WINDOW + Q_TILE and 3 * REP <= HEAD_DIM

    sizes = [fq_w, fkv_w, fkv_w, fox_heads, nq_w] + [nkv_w] * 6 + [n_gate]
    off = np.concatenate([[0], np.cumsum(sizes)]).astype(int)
    o_fq, o_fk, o_fv, o_fl, o_nq, o_kc, o_vc, o_ks, o_vs, o_kw, o_vw, o_g = [int(v) for v in off[:-1]]
    c_fq, c_nq = 0, fq_w
    c_fk = c_nq + nq_w
    c_fv = c_fk + fkv_w
    c_kc = c_fv + fkv_w
    c_vc, c_ks, c_vs, c_kw, c_vw = (c_kc + nkv_w * k for k in range(1, 6))
    n_main = c_vw + nkv_w
    tn = min(512, fkv_w, nkv_w)
    rope_ranges = ((c_nq, c_nq + nq_w), (c_kc, c_kc + nkv_w), (c_ks, c_ks + nkv_w), (c_kw, c_kw + nkv_w))

    nb_p = seq // BLOCK
    nbp_p = max(LANE, nb_p)
    nb_s = -(-(past_len + t_new) // BLOCK)
    nbp_s = LANE
    assert nb_s == 2 * n_pages + 1 and nb_s <= LANE and nbp_p % LANE == 0
    n_win_p = min(WINDOW, seq)
    nq_tiles = seq // Q_TILE

    cos_p, sin_p = _rope_tables(jnp.arange(seq))
    cos_s, sin_s = _rope_tables(jnp.tile(past_len + jnp.arange(t_new), n_seq))
    expand_t = jnp.asarray((np.arange(seq)[:, None] // BLOCK) == np.arange(nbp_p)[None, :], BF16)
    aug_np = np.zeros((HEAD_DIM, REP * Q_TILE), np.float32)
    for r in range(REP):
        aug_np[3 * r:3 * r + 3, r * Q_TILE:(r + 1) * Q_TILE] = 1.0
    aug = jnp.asarray(aug_np, BF16)
    pt_flat = page_table.reshape(-1).astype(jnp.int32)

    hp = x_prompt.reshape(seq, d_model)
    hs = x_sample.reshape(n_seq * t_new, d_model)
    outs_p, outs_s = [], []
    for i in range(depth):
        wi = w_in[i]
        w_main = jnp.concatenate(
            [wi[:, o_fq:o_fq + fq_w], wi[:, o_nq:o_nq + nq_w], wi[:, o_fk:o_fl], wi[:, o_kc:o_g]], axis=1).astype(BF16)
        w_small = jnp.concatenate(
            [wi[:, o_fl:o_fl + fox_heads], wi[:, o_g:o_g + n_gate],
             jnp.zeros((d_model, LANE - fox_heads - n_gate), F32)], axis=1).astype(BF16)
        b_small = jnp.concatenate([b_fgate[i], jnp.zeros((LANE - fox_heads,), F32)]).reshape(1, LANE)
        w_o = w_out[i].astype(BF16)
        w_oa, w_ob = w_o[:fq_w], w_o[fq_w:]
        w_g, w_u, w_d = w_gate[i].astype(BF16), w_up[i].astype(BF16), w_down[i].astype(BF16)
        w_pg, w_pp = w_ple_gate[i].astype(BF16), w_ple_proj[i].astype(BF16)
        pe = jnp.stack([cmp_pe_k[i], cmp_pe_v[i]])
        hidden = cmp_w1_k.shape[-1]
        w1 = jnp.stack([cmp_w1_k[i], cmp_w1_v[i]]).reshape(2, BLOCK, HEAD_DIM, hidden).astype(BF16)
        w1_pairs = w1.reshape(2, BLOCK // 2, 2 * HEAD_DIM, hidden)
        w2 = jnp.stack([cmp_w2_k[i], cmp_w2_v[i]]).astype(BF16)

        def tail(h, o_fox, o_nsa, p):
            h1 = outproj(o_fox, o_nsa, w_oa, w_ob, h)
            act = gateup(rmsnorm(h1, g_ffn[i], BF16), w_g, w_u)
            h2 = downproj(act, w_d, h1)
            return ple(rmsnorm(h2, g_ple[i], BF16), w_pg, p.astype(BF16), w_pp, h2)

        xn = rmsnorm(hp, g_mix[i], BF16)
        pf, pb = inproj(xn, w_main, cos_p, sin_p, tn, rope_ranges)
        small = smallproj(xn, w_small, b_small, fox_heads)
        logf = small[:, :fox_heads]
        f_cum, fp1, fp2, fp3 = cumsum_rows(logf)
        pieces = jnp.stack([fp1, fp2, fp3], axis=-1)
        pieces = pieces.reshape(seq, n_kv, REP * 3).transpose(1, 0, 2)
        k_fox = pb[:, c_fk:c_fk + fkv_w].reshape(seq, n_kv, HEAD_DIM).transpose(1, 0, 2)
        k_aug = jnp.concatenate([k_fox, pieces, jnp.zeros((n_kv, seq, HEAD_DIM - REP * 3), BF16)], axis=-1)
        first_tiles = fox_first_tiles(pb[:, c_fq:c_fq + fq_w], pb[:, c_fk:c_fk + fkv_w], f_cum, n_kv)
        o_fox = fox_prompt(first_tiles, pb, c_fq // (REP * HEAD_DIM), aug, k_aug,
                           _values_t(pb[:, c_fv:c_fv + fkv_w], n_kv), _head_lanes(f_cum, n_kv, 1))
        cmp_p = compress_prompt(pf, (c_kc // HEAD_DIM, c_vc // HEAD_DIM), pe, w1, w2, n_groups, nbp_p)
        cmp_vt = cmp_p[1].transpose(0, 2, 1).astype(BF16)
        gate_lanes = _head_lanes(small[:, fox_heads:fox_heads + n_gate], n_groups, 3)
        gate_lanes = jnp.pad(gate_lanes, ((0, 0), (0, 0), (0, 5), (0, 0)))
        o_nsa = nsa_prompt(pb, c_nq // (REP * HEAD_DIM), c_ks // HEAD_DIM, c_kw // HEAD_DIM,
                           _values_t(pb[:, c_vs:c_vs + nkv_w], n_groups), _values_t(pb[:, c_vw:c_vw + nkv_w], n_groups),
                           cmp_p[0], cmp_vt, expand_t, gate_lanes, min(SEL_TOP_N, nb_p))
        hp = tail(hp, o_fox, o_nsa, p_prompt[i].reshape(seq, -1))
        outs_p.append((
            pf[:, c_fk:c_fk + 2 * fkv_w].reshape(1, seq, 2, n_kv, HEAD_DIM),
            logf.reshape(1, seq, fox_heads),
            pf[:, c_kc:c_kc + 4 * nkv_w].reshape(1, seq, 4, n_groups, HEAD_DIM),
            pf[seq - n_win_p:, c_kw:c_kw + 2 * nkv_w].reshape(1, n_win_p, 2, n_groups, HEAD_DIM),
        ))

        m_s = n_seq * t_new
        xs = rmsnorm(hs, g_mix[i], BF16)
        sf, sb = inproj(xs, w_main, cos_s, sin_s, tn, rope_ranges)
        small_s = smallproj(xs, w_small, b_small, fox_heads)
        logf_s = small_s[:, :fox_heads]
        logf_pool_t = cache_fox_logf[i].transpose(0, 2, 1)
        new_t = jnp.pad(logf_s.reshape(n_seq, t_new, fox_heads).transpose(0, 2, 1),
                        ((0, 0), (0, 0), (0, LANE - t_new)))
        f_all_t = cumsum_pages(pt_flat, logf_pool_t, new_t, n_seq, n_pages)
        fq_s = f_all_t[:, :, past_len:past_len + t_new].transpose(0, 2, 1)
        assert c_fk % (2 * fkv_w) == 0 and c_kc % nkv_w == 0 and c_ks % (2 * nkv_w) == 0 and c_kw % (2 * nkv_w) == 0
        q_fox_s = _heads_major(sb[:, c_fq:c_fq + fq_w], n_seq, t_new, n_kv)
        o_fox_s = fox_sample(pt_flat, cache_fox_kv, i, q_fox_s, sf, c_fk // (2 * fkv_w), fq_s, f_all_t, n_seq, n_pages)
        cmp_raw = compress_sample(pt_flat, cache_nsa_kv, i, sf, c_kc // nkv_w, pe, w1_pairs, w2, n_seq, n_pages, nb_s)
        cmp_s = cmp_raw[:, :, :nb_s * n_groups].reshape(n_seq, 2, nb_s, n_groups, HEAD_DIM).transpose(0, 1, 3, 2, 4)
        cmp_s = jnp.pad(cmp_s, ((0, 0), (0, 0), (0, 0), (0, nbp_s - nb_s), (0, 0)))
        q_nsa_s = _heads_major(sb[:, c_nq:c_nq + nq_w], n_seq, t_new, n_groups)
        gates_s = _heads_major(small_s[:, fox_heads:fox_heads + n_gate], n_seq, t_new, n_groups)
        gates_s = jnp.pad(gates_s, ((0, 0), (0, 0), (0, 0), (0, 13)))
        o_nsa_s = nsa_sample(pt_flat, cache_nsa_kv, state_nsa_win, i, q_nsa_s, cmp_s, sf, c_ks // (2 * nkv_w),
                             c_kw // (2 * nkv_w), gates_s, n_seq, n_pages, min(SEL_TOP_N, nb_s), past_len)
        new_win = win_shift(state_nsa_win, i, sf, c_kw // (2 * nkv_w))
        hs = tail(hs, _tokens_major(o_fox_s, n_seq, t_new, n_kv), _tokens_major(o_nsa_s, n_seq, t_new, n_groups),
                  p_sample[i].reshape(m_s, -1))
        outs_s.append((
            sf[:, c_fk:c_fk + 2 * fkv_w].reshape(n_seq, t_new, 2, n_kv, HEAD_DIM),
            logf_s.reshape(n_seq, t_new, fox_heads),
            sf[:, c_kc:c_kc + 4 * nkv_w].reshape(n_seq, t_new, 4, n_groups, HEAD_DIM),
            new_win,
        ))

    y_prompt = rmsnorm(hp, g_final, F32).reshape(x_prompt.shape)
    y_sample = rmsnorm(hs, g_final, F32).reshape(x_sample.shape)
    stack = lambda outs, j: jnp.stack([r[j] for r in outs], axis=0)
    return (y_prompt, y_sample, stack(outs_p, 0), stack(outs_p, 1), stack(outs_p, 2), stack(outs_p, 3),
            stack(outs_s, 0), stack(outs_s, 1), stack(outs_s, 2), stack(outs_s, 3))
```

```python
import functools

import numpy as np
import jax
import jax.numpy as jnp
from jax import lax
from jax.experimental import pallas as pl
from jax.experimental.pallas import tpu as pltpu

HEAD_DIM = 128
REP = 4
BLOCK = 64
SEL_TOP_N = 16
WINDOW = 512
Q_TILE = 512
K_TILE = 512
V_ROWS = 144
ROPE_THETA = 10000.0
RMS_EPS = 1e-6
ATTN_SCALE = HEAD_DIM ** -0.5
LOG2E = 1.4426950408889634
SCALE2 = ATTN_SCALE * LOG2E
NEG_INF = -1e30
SKIP_MARGIN = 40.0
REMOVED = -3e38
FORCE_BONUS = 1e4
LANE = 128
MIB = 1024 * 1024

F32 = jnp.float32
BF16 = jnp.bfloat16


def _params(sem, vmem_mib):
    return pltpu.CompilerParams(dimension_semantics=sem, vmem_limit_bytes=vmem_mib * MIB)


def _dot(a, b):
    return jnp.dot(a, b, preferred_element_type=F32)


def _dot_nt(a, b):
    return lax.dot_general(a, b, (((1,), (1,)), ((), ())), preferred_element_type=F32)


def _row_tile(m, cap):
    t = min(m, cap)
    assert m % t == 0
    return t


def _iota(shape, dim):
    return lax.broadcasted_iota(jnp.int32, shape, dim)


def _rmsnorm_kernel(x_ref, g_ref, o_ref):
    x = x_ref[...]
    y = x * lax.rsqrt(jnp.mean(x * x, axis=-1, keepdims=True) + RMS_EPS)
    o_ref[...] = (y * g_ref[...]).astype(o_ref.dtype)


def rmsnorm(x, g, out_dtype):
    m, d = x.shape
    tm = _row_tile(m, 256)
    return pl.pallas_call(
        _rmsnorm_kernel,
        grid=(m // tm,),
        in_specs=[pl.BlockSpec((tm, d), lambda i: (i, 0)), pl.BlockSpec((1, d), lambda i: (0, 0))],
        out_specs=pl.BlockSpec((tm, d), lambda i: (i, 0)),
        out_shape=jax.ShapeDtypeStruct((m, d), out_dtype),
        compiler_params=_params(("parallel",), 32),
        name="rmsnorm",
    )(x, g.reshape(1, d))


def _inproj_kernel(x_ref, w_ref, cos_ref, sin_ref, of_ref, ob_ref, *, tn, rope_ranges):
    acc = _dot(x_ref[...], w_ref[...])
    col0 = pl.program_id(1) * tn
    is_rope = None
    for lo, hi in rope_ranges:
        hit = (col0 >= lo) & (col0 < hi)
        is_rope = hit if is_rope is None else (is_rope | hit)

    @pl.when(is_rope)
    def _():
        cos = cos_ref[...]
        sin = sin_ref[...]
        for h in range(tn // HEAD_DIM):
            sl = slice(h * HEAD_DIM, (h + 1) * HEAD_DIM)
            xh = acc[:, sl]
            r = xh * cos + pltpu.roll(xh, HEAD_DIM // 2, 1) * sin
            of_ref[:, sl] = r
            ob_ref[:, sl] = r.astype(BF16)

    @pl.when(jnp.logical_not(is_rope))
    def _():
        of_ref[...] = acc
        ob_ref[...] = acc.astype(BF16)


def inproj(xn, w_main, cos, sin, tn, rope_ranges):
    m, k = xn.shape
    n = w_main.shape[1]
    tm = _row_tile(m, 1024)
    return pl.pallas_call(
        functools.partial(_inproj_kernel, tn=tn, rope_ranges=rope_ranges),
        grid=(m // tm, n // tn),
        in_specs=[
            pl.BlockSpec((tm, k), lambda i, j: (i, 0)),
            pl.BlockSpec((k, tn), lambda i, j: (0, j)),
            pl.BlockSpec((tm, HEAD_DIM), lambda i, j: (i, 0)),
            pl.BlockSpec((tm, HEAD_DIM), lambda i, j: (i, 0)),
        ],
        out_specs=[pl.BlockSpec((tm, tn), lambda i, j: (i, j)), pl.BlockSpec((tm, tn), lambda i, j: (i, j))],
        out_shape=[jax.ShapeDtypeStruct((m, n), F32), jax.ShapeDtypeStruct((m, n), BF16)],
        compiler_params=_params(("parallel", "arbitrary"), 48),
        name="inproj",
    )(xn, w_main, cos, sin)


def _smallproj_kernel(x_ref, w_ref, b_ref, o_ref, *, n_logf):
    v = _dot(x_ref[...], w_ref[...])
    z = v + b_ref[...]
    logf = -(jnp.maximum(-z, 0.0) + jnp.log1p(jnp.exp(-jnp.abs(z))))
    gate = jax.nn.sigmoid(v)
    lane = _iota(v.shape, 1)
    o_ref[...] = jnp.where(lane < n_logf, logf, gate)


def smallproj(xn, w_small, b_small, n_logf):
    m, k = xn.shape
    tm = _row_tile(m, 1024)
    return pl.pallas_call(
        functools.partial(_smallproj_kernel, n_logf=n_logf),
        grid=(m // tm,),
        in_specs=[
            pl.BlockSpec((tm, k), lambda i: (i, 0)),
            pl.BlockSpec((k, LANE), lambda i: (0, 0)),
            pl.BlockSpec((1, LANE), lambda i: (0, 0)),
        ],
        out_specs=pl.BlockSpec((tm, LANE), lambda i: (i, 0)),
        out_shape=jax.ShapeDtypeStruct((m, LANE), F32),
        compiler_params=_params(("parallel",), 32),
        name="smallproj",
    )(xn, w_small, b_small)


def _outproj_kernel(a_ref, b_ref, wa_ref, wb_ref, h_ref, o_ref):
    o_ref[...] = h_ref[...] + (_dot(a_ref[...], wa_ref[...]) + _dot(b_ref[...], wb_ref[...]))


def outproj(o_fox, o_nsa, w_a, w_b, h):
    m, ka = o_fox.shape
    kb = o_nsa.shape[1]
    n = w_a.shape[1]
    tm, tn = _row_tile(m, 1024), 512
    return pl.pallas_call(
        _outproj_kernel,
        grid=(m // tm, n // tn),
        in_specs=[
            pl.BlockSpec((tm, ka), lambda i, j: (i, 0)),
            pl.BlockSpec((tm, kb), lambda i, j: (i, 0)),
            pl.BlockSpec((ka, tn), lambda i, j: (0, j)),
            pl.BlockSpec((kb, tn), lambda i, j: (0, j)),
            pl.BlockSpec((tm, tn), lambda i, j: (i, j)),
        ],
        out_specs=pl.BlockSpec((tm, tn), lambda i, j: (i, j)),
        out_shape=jax.ShapeDtypeStruct((m, n), F32),
        compiler_params=_params(("parallel", "arbitrary"), 48),
        name="outproj",
    )(o_fox, o_nsa, w_a, w_b, h)


def _gateup_kernel(x_ref, wg_ref, wu_ref, o_ref):
    x = x_ref[...]
    g = _dot(x, wg_ref[...])
    u = _dot(x, wu_ref[...])
    o_ref[...] = (g * jax.nn.sigmoid(g) * u).astype(o_ref.dtype)


def gateup(xn, w_gate, w_up):
    m, k = xn.shape
    n = w_gate.shape[1]
    tm, tn = _row_tile(m, 1024), 256
    return pl.pallas_call(
        _gateup_kernel,
        grid=(m // tm, n // tn),
        in_specs=[
            pl.BlockSpec((tm, k), lambda i, j: (i, 0)),
            pl.BlockSpec((k, tn), lambda i, j: (0, j)),
            pl.BlockSpec((k, tn), lambda i, j: (0, j)),
        ],
        out_specs=pl.BlockSpec((tm, tn), lambda i, j: (i, j)),
        out_shape=jax.ShapeDtypeStruct((m, n), BF16),
        compiler_params=_params(("parallel", "arbitrary"), 48),
        name="gateup",
    )(xn, w_gate, w_up)


def _down_kernel(x_ref, w_ref, h_ref, o_ref):
    o_ref[...] = h_ref[...] + _dot(x_ref[...], w_ref[...])


def downproj(act, w_down, h):
    m, k = act.shape
    n = w_down.shape[1]
    tm, tn = _row_tile(m, 512), 512
    return pl.pallas_call(
        _down_kernel,
        grid=(m // tm, n // tn),
        in_specs=[
            pl.BlockSpec((tm, k), lambda i, j: (i, 0)),
            pl.BlockSpec((k, tn), lambda i, j: (0, j)),
            pl.BlockSpec((tm, tn), lambda i, j: (i, j)),
        ],
        out_specs=pl.BlockSpec((tm, tn), lambda i, j: (i, j)),
        out_shape=jax.ShapeDtypeStruct((m, n), F32),
        compiler_params=_params(("parallel", "arbitrary"), 58),
        name="downproj",
    )(act, w_down, h)


def _ple_kernel(x_ref, wg_ref, p_ref, wp_ref, h_ref, o_ref):
    gate = jax.nn.sigmoid(_dot(x_ref[...], wg_ref[...]))
    o_ref[...] = h_ref[...] + gate * _dot(p_ref[...], wp_ref[...])


def ple(xn, w_gate, p, w_proj, h):
    m, k = xn.shape
    kp = p.shape[1]
    n = w_gate.shape[1]
    tm, tn = _row_tile(m, 1024), 512
    return pl.pallas_call(
        _ple_kernel,
        grid=(m // tm, n // tn),
        in_specs=[
            pl.BlockSpec((tm, k), lambda i, j: (i, 0)),
            pl.BlockSpec((k, tn), lambda i, j: (0, j)),
            pl.BlockSpec((tm, kp), lambda i, j: (i, 0)),
            pl.BlockSpec((kp, tn), lambda i, j: (0, j)),
            pl.BlockSpec((tm, tn), lambda i, j: (i, j)),
        ],
        out_specs=pl.BlockSpec((tm, tn), lambda i, j: (i, j)),
        out_shape=jax.ShapeDtypeStruct((m, n), F32),
        compiler_params=_params(("parallel", "arbitrary"), 48),
        name="ple",
    )(xn, w_gate, p, w_proj, h)


def _split3(x):
    x1 = x.astype(BF16)
    r1 = x - x1.astype(F32)
    x2 = r1.astype(BF16)
    x3 = (r1 - x2.astype(F32)).astype(BF16)
    return x1, x2, x3


def _cumsum_rows_kernel(x_ref, tri_ref, o_ref, p1_ref, p2_ref, p3_ref, carry_ref):
    @pl.when(pl.program_id(0) == 0)
    def _():
        carry_ref[...] = jnp.zeros_like(carry_ref)

    x1, x2, x3 = _split3(x_ref[...])
    tri = tri_ref[...]
    out = (_dot(tri, x1) + _dot(tri, x2) + _dot(tri, x3)) + carry_ref[...]
    o_ref[...] = out
    carry_ref[...] = out[out.shape[0] - 1:, :]
    p1_ref[...], p2_ref[...], p3_ref[...] = _split3(out * (-1.0 / ATTN_SCALE))


def cumsum_rows(x):
    t, h = x.shape
    c = _row_tile(t, 512)
    tri = jnp.asarray(np.tril(np.ones((c, c), np.float32)), BF16)
    row_spec = pl.BlockSpec((c, h), lambda i: (i, 0))
    return pl.pallas_call(
        _cumsum_rows_kernel,
        grid=(t // c,),
        in_specs=[row_spec, pl.BlockSpec((c, c), lambda i: (0, 0))],
        out_specs=[row_spec] * 4,
        out_shape=[jax.ShapeDtypeStruct((t, h), F32)] + [jax.ShapeDtypeStruct((t, h), BF16)] * 3,
        scratch_shapes=[pltpu.VMEM((1, h), F32)],
        compiler_params=_params(("arbitrary",), 32),
        name="cumsum_rows",
    )(x, tri)


def _cumsum_lanes(chunk_refs, tri_ones):
    h = chunk_refs[0].shape[0]
    parts = []
    for ref in chunk_refs:
        y = _dot(jnp.concatenate(_split3(ref[...]), axis=0), tri_ones)
        parts.append(y[0:h] + y[h:2 * h] + y[2 * h:3 * h])
    carry = jnp.zeros((h, LANE), F32)
    outs = []
    for part in parts:
        outs.append(part[:, :LANE] + carry)
        carry = carry + part[:, LANE:]
    return outs


def _silu(x):
    return x * jax.nn.sigmoid(x)


def _compress_prompt_kernel(cb_ref, x_ref, pe_ref, w1_ref, w2_ref, o_ref, *, nb):
    hidden = w1_ref.shape[2]

    def body(l, acc):
        xl = x_ref[pl.ds(l, nb, stride=BLOCK), :] + pe_ref[pl.ds(l, 1), :]
        return acc + _dot(xl.astype(BF16), w1_ref[l])

    acc = lax.fori_loop(0, BLOCK, body, jnp.zeros((nb, hidden), F32))
    out = _dot(_silu(acc).astype(BF16), w2_ref[...])
    o_ref[...] = jnp.zeros_like(o_ref)
    o_ref[0:nb, :] = out


def compress_prompt(proj_f32, col_blocks, pe, w1, w2, n_groups, nbp):
    t = proj_f32.shape[0]
    nb = t // BLOCK
    hidden = w1.shape[-1]
    cb = jnp.asarray(col_blocks, jnp.int32)
    grid_spec = pltpu.PrefetchScalarGridSpec(
        num_scalar_prefetch=1,
        grid=(2, n_groups),
        in_specs=[
            pl.BlockSpec((t, HEAD_DIM), lambda kv, g, cb: (0, cb[kv] + g)),
            pl.BlockSpec((None, BLOCK, HEAD_DIM), lambda kv, g, cb: (kv, 0, 0)),
            pl.BlockSpec((None, BLOCK, HEAD_DIM, hidden), lambda kv, g, cb: (kv, 0, 0, 0)),
            pl.BlockSpec((None, hidden, HEAD_DIM), lambda kv, g, cb: (kv, 0, 0)),
        ],
        out_specs=pl.BlockSpec((None, None, nbp, HEAD_DIM), lambda kv, g, cb: (kv, g, 0, 0)),
    )
    return pl.pallas_call(
        functools.partial(_compress_prompt_kernel, nb=nb),
        grid_spec=grid_spec,
        out_shape=jax.ShapeDtypeStruct((2, n_groups, nbp, HEAD_DIM), F32),
        compiler_params=_params(("arbitrary", "arbitrary"), 40),
        name="compress_prompt",
    )(cb, proj_f32, pe, w1, w2)


def _compress_sample_kernel(pt_ref, *refs, n_pages, n_groups, nb):
    pages = refs[:n_pages]
    new_ref, pe_ref, w1_ref, w2_ref, o_ref, newblk_ref = refs[n_pages:]
    t_new = new_ref.shape[0]
    hidden = w1_ref.shape[2]
    rows_pad = o_ref.shape[0]
    n_rows = (2 * n_pages + 1) * n_groups

    newblk_ref[...] = jnp.zeros_like(newblk_ref)
    for l in range(t_new):
        for g in range(n_groups):
            newblk_ref[l, g:g + 1, :] = new_ref[l:l + 1, g * HEAD_DIM:(g + 1) * HEAD_DIM]

    def gather(l):
        rr = []
        for pg in pages:
            rr.append(pg[l])
            rr.append(pg[l + BLOCK])
        rr.append(newblk_ref[l])
        if rows_pad > n_rows:
            rr.append(jnp.zeros((rows_pad - n_rows, HEAD_DIM), F32))
        return jnp.concatenate(rr, axis=0) + pe_ref[l:l + 1, :]

    acc = jnp.zeros((rows_pad, hidden), F32)
    for l2 in range(BLOCK // 2):
        x = jnp.concatenate([gather(2 * l2), gather(2 * l2 + 1)], axis=1).astype(BF16)
        acc = acc + _dot(x, w1_ref[l2])
    out = _dot(_silu(acc).astype(BF16), w2_ref[...])
    o_ref[...] = jnp.where(_iota(out.shape, 0) < nb * n_groups, out, 0.0)


def compress_sample(pt_flat, nsa_cache, layer, new_rows, new_col_block, pe, w1_pairs, w2, n_seq, n_pages, nb):
    page, _, n_groups, _ = nsa_cache.shape[2:]
    assert page == 2 * BLOCK
    gw = n_groups * HEAD_DIM
    hidden = w1_pairs.shape[-1]
    t_new = new_rows.shape[0] // n_seq
    rows_pad = -(-((2 * n_pages + 1) * n_groups) // 8) * 8
    spec = lambda p: pl.BlockSpec((None, None, page, None, n_groups, HEAD_DIM),
                                  lambda kv, b, pt: (layer, pt[b * n_pages + p], 0, kv, 0, 0))
    grid_spec = pltpu.PrefetchScalarGridSpec(
        num_scalar_prefetch=1,
        grid=(2, n_seq),
        in_specs=[spec(p) for p in range(n_pages)] + [
            pl.BlockSpec((t_new, gw), lambda kv, b, pt: (b, new_col_block + kv)),
            pl.BlockSpec((None, BLOCK, HEAD_DIM), lambda kv, b, pt: (kv, 0, 0)),
            pl.BlockSpec((None, BLOCK // 2, 2 * HEAD_DIM, hidden), lambda kv, b, pt: (kv, 0, 0, 0)),
            pl.BlockSpec((None, hidden, HEAD_DIM), lambda kv, b, pt: (kv, 0, 0)),
        ],
        out_specs=pl.BlockSpec((None, None, rows_pad, HEAD_DIM), lambda kv, b, pt: (b, kv, 0, 0)),
        scratch_shapes=[pltpu.VMEM((BLOCK, n_groups, HEAD_DIM), F32)],
    )
    return pl.pallas_call(
        functools.partial(_compress_sample_kernel, n_pages=n_pages, n_groups=n_groups, nb=nb),
        grid_spec=grid_spec,
        out_shape=jax.ShapeDtypeStruct((n_seq, 2, rows_pad, HEAD_DIM), F32),
        compiler_params=_params(("arbitrary", "arbitrary"), 40),
        name="compress_sample",
    )(pt_flat, *([nsa_cache] * n_pages), new_rows, pe, w1_pairs, w2)


def _softmax_rows(s):
    m = jnp.max(s, axis=-1, keepdims=True)
    e = jnp.exp(s - m)
    return e / jnp.sum(e, axis=-1, keepdims=True)


def _softmax_cols(s):
    m = jnp.max(s, axis=0, keepdims=True)
    e = jnp.exp(s - m)
    return e / jnp.sum(e, axis=0, keepdims=True)


def _top_n_mask_t(score_t, blk_t, n_sel, n_blocks_pad):
    sel = jnp.zeros(score_t.shape, F32)
    for _ in range(n_sel):
        mx = jnp.max(score_t, axis=0, keepdims=True)
        idx = jnp.min(jnp.where(score_t == mx, blk_t, n_blocks_pad), axis=0, keepdims=True)
        hit = blk_t == idx
        sel = jnp.where(hit, 1.0, sel)
        score_t = jnp.where(hit, REMOVED, score_t)
    return sel


def _selection_scores_t(imp_t, blk_t, qpos_t):
    cur = qpos_t // BLOCK
    forced = (blk_t == 0) | (blk_t == cur) | (blk_t == cur - 1)
    avail = blk_t * BLOCK <= qpos_t
    return jnp.where(avail, imp_t + jnp.where(forced, FORCE_BONUS, 0.0), NEG_INF)


def _online_tile_t(x, vt_tile, m, acc_ref, shift):
    mx = jnp.max(x, axis=0, keepdims=True) * SCALE2
    if shift is not None:
        mx = mx + shift
    m_new = jnp.maximum(m, mx)
    off = m_new if shift is None else m_new - shift
    p = jnp.exp2(x * SCALE2 - off)
    acc_ref[...] = jnp.exp2(m - m_new) * acc_ref[...] + _dot(vt_tile, p.astype(BF16))
    return m_new


def _queries_t(q):
    return jnp.concatenate([q[:, r * HEAD_DIM:(r + 1) * HEAD_DIM].T for r in range(REP)], axis=1)


def _store_heads(o_ref, o_t):
    for r in range(REP):
        o_ref[:, r * HEAD_DIM:(r + 1) * HEAD_DIM] = o_t[:, r * Q_TILE:(r + 1) * Q_TILE].T.astype(o_ref.dtype)


def _fox_prompt_kernel(first_ref, q_ref, aug_ref, ka_ref, vt_ref, fq_ref, o_ref, acc_ref):
    q0 = pl.program_id(1) * Q_TILE
    cols = REP * Q_TILE
    qa = jnp.concatenate([_queries_t(q_ref[...]), aug_ref[...]], axis=0)
    fq2 = fq_ref[...] * LOG2E
    acc_ref[...] = jnp.zeros_like(acc_ref)
    n_full = q0 // K_TILE

    def scores(j):
        k0 = pl.multiple_of(j * K_TILE, K_TILE)
        return _dot(ka_ref[pl.ds(k0, K_TILE), :], qa)

    def update(j, x, m):
        k0 = pl.multiple_of(j * K_TILE, K_TILE)
        return _online_tile_t(x, vt_ref[:, pl.ds(k0, K_TILE)], m, acc_ref, fq2)

    first = first_ref[pl.program_id(0) * pl.num_programs(1) + pl.program_id(1)]
    m = lax.fori_loop(first, n_full, lambda j, m: update(j, scores(j), m), jnp.full((1, cols), NEG_INF, F32))
    kpos = n_full * K_TILE + _iota((K_TILE, cols), 0)
    qpos = q0 + (_iota((K_TILE, cols), 1) & (Q_TILE - 1))
    update(n_full, jnp.where(kpos <= qpos, scores(n_full), NEG_INF), m)
    acc = acc_ref[...]
    _store_heads(o_ref, acc[0:HEAD_DIM] / acc[HEAD_DIM:HEAD_DIM + 1])


def fox_first_tiles(q, k, f_cum, n_kv):
    assert Q_TILE == K_TILE
    t = q.shape[0]
    nt = t // K_TILE
    qn = jnp.sum(jnp.square(q.astype(F32)).reshape(nt, Q_TILE, n_kv, REP, HEAD_DIM), axis=-1)
    qmax = jnp.sqrt(jnp.max(qn, axis=(1, 3)))
    kn = jnp.sum(jnp.square(k.astype(F32)).reshape(nt, K_TILE, n_kv, HEAD_DIM), axis=-1)
    kmax = jnp.sqrt(jnp.max(kn, axis=1))
    f_first = f_cum[0::K_TILE].reshape(nt, n_kv, REP)
    f_last = f_cum[K_TILE - 1::K_TILE].reshape(nt, n_kv, REP)
    gap = jnp.max(f_first[:, None] - f_last[None, :], axis=-1)
    bound = 1.01 * ATTN_SCALE * qmax[:, None] * (kmax[None, :] + kmax[:, None]) + gap
    earlier = jnp.arange(nt)[None, :, None] < jnp.arange(nt)[:, None, None]
    skippable = (bound < -SKIP_MARGIN) & earlier
    tile_idx = jnp.arange(nt, dtype=jnp.int32)[None, :, None]
    first = jnp.min(jnp.where(skippable, nt, tile_idx), axis=1)
    return first.T.reshape(-1).astype(jnp.int32)


def fox_prompt(first_tiles, proj_bf16, q_blk0, aug, k_aug, v_t, fq_lanes):
    n_kv, _, t = v_t.shape
    cols = REP * Q_TILE
    grid_spec = pltpu.PrefetchScalarGridSpec(
        num_scalar_prefetch=1,
        grid=(n_kv, t // Q_TILE),
        in_specs=[
            pl.BlockSpec((Q_TILE, REP * HEAD_DIM), lambda g, i, first: (i, q_blk0 + g)),
            pl.BlockSpec((HEAD_DIM, cols), lambda g, i, first: (0, 0)),
            pl.BlockSpec((None, t, 2 * HEAD_DIM), lambda g, i, first: (g, 0, 0)),
            pl.BlockSpec((None, V_ROWS, t), lambda g, i, first: (g, 0, 0)),
            pl.BlockSpec((None, None, 1, cols), lambda g, i, first: (g, i, 0, 0)),
        ],
        out_specs=pl.BlockSpec((Q_TILE, REP * HEAD_DIM), lambda g, i, first: (i, g)),
        scratch_shapes=[pltpu.VMEM((V_ROWS, cols), F32)],
    )
    return pl.pallas_call(
        _fox_prompt_kernel,
        grid_spec=grid_spec,
        out_shape=jax.ShapeDtypeStruct((t, n_kv * REP * HEAD_DIM), BF16),
        compiler_params=_params(("arbitrary", "arbitrary"), 40),
        name="fox_prompt",
    )(first_tiles, proj_bf16, aug, k_aug, v_t, fq_lanes)


def _nsa_prompt_kernel(q_ref, ka_ref, vst_ref, kw_ref, vwt_ref, ck_ref, cvt_ref, gate_ref, wmask_ref, o_ref,
                       acc_ref, *, n_sel):
    q0 = pl.program_id(1) * Q_TILE
    cols = REP * Q_TILE
    nbp = ck_ref.shape[0]
    qt = _queries_t(q_ref[...])

    x_c = _dot(ck_ref[...].astype(BF16), qt) * ATTN_SCALE
    blk = _iota((nbp, cols), 0)
    qpos = q0 + (_iota((nbp, cols), 1) & (Q_TILE - 1))
    x_c = jnp.where((blk + 1) * BLOCK - 1 <= qpos, x_c, NEG_INF)
    p_c = _softmax_cols(x_c) * (qpos >= BLOCK - 1).astype(F32)
    o_cmp = _dot(cvt_ref[...], p_c.astype(BF16))

    imp_t = p_c[:, 0:Q_TILE]
    for r in range(1, REP):
        imp_t = imp_t + p_c[:, r * Q_TILE:(r + 1) * Q_TILE]
    blk_t = _iota((nbp, Q_TILE), 0)
    qpos_t = q0 + _iota((nbp, Q_TILE), 1)
    sel_t = _top_n_mask_t(_selection_scores_t(imp_t, blk_t, qpos_t), blk_t, n_sel, nbp)

    sel_bias = jnp.where(sel_t > 0.5, 0.0, NEG_INF).astype(BF16)
    qa = jnp.concatenate([qt, jnp.concatenate([sel_bias] * REP, axis=1)], axis=0)
    acc_ref[...] = jnp.zeros_like(acc_ref)
    n_full = q0 // K_TILE

    def scores(j):
        k0 = pl.multiple_of(j * K_TILE, K_TILE)
        return _dot(ka_ref[pl.ds(k0, K_TILE), :], qa)

    def update(j, x, m):
        k0 = pl.multiple_of(j * K_TILE, K_TILE)
        return _online_tile_t(x, vst_ref[:, pl.ds(k0, K_TILE)], m, acc_ref, None)

    m = lax.fori_loop(0, n_full, lambda j, m: update(j, scores(j), m), jnp.full((1, cols), NEG_INF, F32))
    kpos = n_full * K_TILE + _iota((K_TILE, cols), 0)
    causal = kpos <= q0 + (_iota((K_TILE, cols), 1) & (Q_TILE - 1))
    update(n_full, jnp.where(causal, scores(n_full), NEG_INF), m)
    acc = acc_ref[...]
    o_slc = acc[0:HEAD_DIM] / acc[HEAD_DIM:HEAD_DIM + 1]

    wlen = WINDOW + Q_TILE
    w0 = pl.multiple_of(jnp.maximum(q0 - WINDOW, 0), Q_TILE)
    x_w = _dot(kw_ref[pl.ds(w0, wlen), :], qt) * SCALE2
    x_w = x_w + jnp.concatenate([wmask_ref[...]] * REP, axis=1)
    p_w = jnp.exp2(x_w - jnp.max(x_w, axis=0, keepdims=True))
    a_w = _dot(vwt_ref[:, pl.ds(w0, wlen)], p_w.astype(BF16))
    o_win = a_w[0:HEAD_DIM] / a_w[HEAD_DIM:HEAD_DIM + 1]

    gates = gate_ref[...]
    _store_heads(o_ref, gates[0:1] * o_cmp + gates[1:2] * o_slc + gates[2:3] * o_win)


def _window_masks(t):
    n_var = WINDOW // Q_TILE + 1
    k = np.arange(WINDOW + Q_TILE)[None, :, None]
    q = np.arange(Q_TILE)[None, None, :]
    q0 = (np.arange(n_var) * Q_TILE)[:, None, None]
    dist = q0 + q - (np.maximum(q0 - WINDOW, 0) + k)
    return jnp.asarray(np.where((dist >= 0) & (dist < WINDOW), 0.0, NEG_INF), F32)


def nsa_prompt(proj_bf16, q_blk0, k_aug, kw_blk0, vs_t, vw_t, cmp_k, cmp_vt, gate_lanes, n_sel):
    assert WINDOW % Q_TILE == 0
    wmask = _window_masks(vs_t.shape[2])
    last_var = wmask.shape[0] - 1
    n_groups, _, t = vs_t.shape
    nbp = cmp_k.shape[1]
    cols = REP * Q_TILE
    return pl.pallas_call(
        functools.partial(_nsa_prompt_kernel, n_sel=n_sel),
        grid=(n_groups, t // Q_TILE),
        in_specs=[
            pl.BlockSpec((Q_TILE, REP * HEAD_DIM), lambda g, i: (i, q_blk0 + g)),
            pl.BlockSpec((None, t, HEAD_DIM + nbp), lambda g, i: (g, 0, 0)),
            pl.BlockSpec((None, V_ROWS, t), lambda g, i: (g, 0, 0)),
            pl.BlockSpec((t, HEAD_DIM), lambda g, i: (0, kw_blk0 + g)),
            pl.BlockSpec((None, V_ROWS, t), lambda g, i: (g, 0, 0)),
            pl.BlockSpec((None, nbp, HEAD_DIM), lambda g, i: (g, 0, 0)),
            pl.BlockSpec((None, HEAD_DIM, nbp), lambda g, i: (g, 0, 0)),
            pl.BlockSpec((None, None, 8, cols), lambda g, i: (g, i, 0, 0)),
            pl.BlockSpec((None, WINDOW + Q_TILE, Q_TILE), lambda g, i: (jnp.minimum(i, last_var), 0, 0)),
        ],
        out_specs=pl.BlockSpec((Q_TILE, REP * HEAD_DIM), lambda g, i: (i, g)),
        out_shape=jax.ShapeDtypeStruct((t, n_groups * REP * HEAD_DIM), BF16),
        scratch_shapes=[pltpu.VMEM((V_ROWS, cols), F32)],
        compiler_params=_params(("arbitrary", "arbitrary"), 56),
        name="nsa_prompt",
    )(proj_bf16, k_aug, vs_t, proj_bf16, vw_t, cmp_k, cmp_vt, gate_lanes, wmask)


def _pad_rows(x, n):
    return jnp.concatenate([x, jnp.zeros((n - x.shape[0], x.shape[1]), x.dtype)], axis=0)


def _copy_rows(block_refs, rows_ref):
    views = []
    for p, r in enumerate(block_refs):
        rows_ref[p] = r[...].reshape(rows_ref.shape[1:])
        views.append(rows_ref.at[p])
    return views


def _fox_sample_kernel(pt_ref, *refs, n_pages, n_kv):
    lf_pages = refs[n_pages:2 * n_pages]
    q_ref, new_ref, lf_new_ref, tri_ref, o_ref, kv_ref = refs[2 * n_pages:]
    pages = refs[:n_pages]
    stride = 2 * n_kv
    page = pages[0].shape[0] // stride
    t_new = new_ref.shape[0]
    rows = REP * t_new
    past = n_pages * page
    n_keys = kv_ref.shape[1]
    for c in range(stride):
        for p in range(n_pages):
            kv_ref[c, p * page:(p + 1) * page, :] = pages[p][pl.ds(c, page, stride=stride), :].astype(BF16)
        kv_ref[c, past:n_keys, :] = _pad_rows(new_ref[:, c * HEAD_DIM:(c + 1) * HEAD_DIM], n_keys - past).astype(BF16)
    visible = _iota((t_new, n_keys), 1) <= past + _iota((t_new, n_keys), 0)
    f_chunks = _cumsum_lanes(list(lf_pages) + [lf_new_ref], tri_ref[...])
    f_keys = jnp.concatenate(f_chunks, axis=1)
    f_queries = _pad_rows(f_chunks[-1], LANE).T[0:t_new, :]
    for g in range(n_kv):
        s = _dot_nt(q_ref[g], kv_ref[g]) * ATTN_SCALE
        s = s.reshape(REP, t_new, n_keys) - f_keys[g * REP:(g + 1) * REP, :][:, None, :]
        s = jnp.where(visible[None], s, NEG_INF).reshape(rows, n_keys)
        fq = jnp.concatenate([f_queries[:, g * REP + r:g * REP + r + 1] for r in range(REP)], axis=0)
        prob = _softmax_rows(s + fq).astype(BF16)
        o_ref[g] = _dot(prob, kv_ref[n_kv + g]).astype(o_ref.dtype)


def fox_sample(pt_flat, fox_cache, layer, logf_pool_t, logf_new_t, q_s, new_rows, new_col_block, n_seq, n_pages):
    depth, n_pool, page, _, n_kv, _ = fox_cache.shape
    fox_cache = fox_cache.reshape(depth, n_pool, page * 2 * n_kv, HEAD_DIM)
    t_new = new_rows.shape[0] // n_seq
    rows = REP * t_new
    n_heads = logf_pool_t.shape[1]
    assert page == LANE
    tri = jnp.asarray(np.concatenate([np.triu(np.ones((LANE, LANE), np.float32)),
                                      np.ones((LANE, LANE), np.float32)], axis=1), BF16)
    spec = lambda p: pl.BlockSpec((None, None, page * 2 * n_kv, HEAD_DIM),
                                  lambda b, pt: (layer, pt[b * n_pages + p], 0, 0))
    lf_spec = lambda p: pl.BlockSpec((None, n_heads, page), lambda b, pt: (pt[b * n_pages + p], 0, 0))
    grid_spec = pltpu.PrefetchScalarGridSpec(
        num_scalar_prefetch=1,
        grid=(n_seq,),
        in_specs=[spec(p) for p in range(n_pages)] + [lf_spec(p) for p in range(n_pages)] + [
            pl.BlockSpec((None, n_kv, rows, HEAD_DIM), lambda b, pt: (b, 0, 0, 0)),
            pl.BlockSpec((t_new, 2 * n_kv * HEAD_DIM), lambda b, pt: (b, new_col_block)),
            pl.BlockSpec((None, n_heads, page), lambda b, pt: (b, 0, 0)),
            pl.BlockSpec((LANE, 2 * LANE), lambda b, pt: (0, 0)),
        ],
        out_specs=pl.BlockSpec((None, n_kv, rows, HEAD_DIM), lambda b, pt: (b, 0, 0, 0)),
        scratch_shapes=[pltpu.VMEM((2 * n_kv, (n_pages + 1) * page, HEAD_DIM), BF16)],
    )
    return pl.pallas_call(
        functools.partial(_fox_sample_kernel, n_pages=n_pages, n_kv=n_kv),
        grid_spec=grid_spec,
        out_shape=jax.ShapeDtypeStruct((n_seq, n_kv, rows, HEAD_DIM), BF16),
        compiler_params=_params(("arbitrary",), 52),
        name="fox_sample",
    )(pt_flat, *([fox_cache] * n_pages), *([logf_pool_t] * n_pages), q_s, new_rows, logf_new_t, tri)


def _nsa_sample_kernel(pt_ref, *refs, n_pages, n_groups, n_sel, past_len):
    (q_ref, cmp_ref, new_slc_ref, new_win_ref, win_ref, gate_ref, o_ref, win_out_ref,
     krows_ref, vrows_ref, wrows_ref) = refs[2 * n_pages:]
    page = refs[0].shape[0]
    kpages = _copy_rows(refs[:n_pages], krows_ref)
    vpages = _copy_rows(refs[n_pages:2 * n_pages], vrows_ref)
    t_new = new_slc_ref.shape[0]
    rows = REP * t_new
    gw = n_groups * HEAD_DIM
    nbp = cmp_ref.shape[2]
    n_buf = win_ref.shape[0]
    win_rows = _copy_rows([win_ref], wrows_ref)[0]
    blocks_per_page = page // BLOCK

    qpos = past_len + _iota((t_new, nbp), 0)
    c_mask = (_iota((t_new, nbp), 1) + 1) * BLOCK - 1 <= qpos
    any_vis = (qpos >= BLOCK - 1).astype(F32)
    o_cmp, imps = [], []
    for g in range(n_groups):
        s_c = _dot_nt(q_ref[g], cmp_ref[0, g].astype(BF16)) * ATTN_SCALE
        s_c = jnp.where(c_mask[None], s_c.reshape(REP, t_new, nbp), NEG_INF)
        p_c = _softmax_rows(s_c) * any_vis[None]
        o_cmp.append(_dot(p_c.reshape(rows, nbp).astype(BF16), cmp_ref[1, g].astype(BF16)))
        imp = p_c[0]
        for r in range(1, REP):
            imp = imp + p_c[r]
        imps.append(imp)

    imp_all = _pad_rows(jnp.concatenate(imps, axis=0), LANE)
    blk_t = _iota((nbp, LANE), 0)
    qpos_t = past_len + _iota((nbp, LANE), 1) % t_new
    sel_t = _top_n_mask_t(_selection_scores_t(imp_all.T, blk_t, qpos_t), blk_t, n_sel, nbp)
    sel_all = sel_t.T

    lane_p = _iota((t_new, page), 1)
    row_p = _iota((t_new, page), 0)
    lane_blk = lane_p // BLOCK
    widx = _iota((t_new, n_buf + page), 1)
    wdist = n_buf + _iota((t_new, n_buf + page), 0) - widx
    w_mask = (wdist >= 0) & (wdist < WINDOW) & (widx < n_buf + t_new)
    gates = gate_ref[...]

    for g in range(n_groups):
        q4 = q_ref[g]
        gs = slice(g * HEAD_DIM, (g + 1) * HEAD_DIM)
        vs = slice(gw + g * HEAD_DIM, gw + (g + 1) * HEAD_DIM)
        sel_g = sel_all[g * t_new:(g + 1) * t_new, :]

        chunks = []
        for p in range(n_pages + 1):
            if p < n_pages:
                k_p = kpages[p][pl.ds(g, page, stride=n_groups), :].astype(BF16)
            else:
                k_p = _pad_rows(new_slc_ref[:, gs], page).astype(BF16)
            s = _dot_nt(q4, k_p) * ATTN_SCALE
            picked = jnp.zeros((t_new, page), F32)
            for c in range(blocks_per_page):
                b_idx = p * blocks_per_page + c
                picked = jnp.where(lane_blk == c, sel_g[:, b_idx:b_idx + 1], picked)
            ok = picked > 0.5
            if p == n_pages:
                ok = ok & (lane_p <= row_p)
            chunks.append(jnp.where(ok[None], s.reshape(REP, t_new, page), NEG_INF).reshape(rows, page))
        prob = _softmax_rows(jnp.concatenate(chunks, axis=1)).astype(BF16)
        o_slc = jnp.zeros((rows, HEAD_DIM), F32)
        for p in range(n_pages + 1):
            if p < n_pages:
                v_p = vpages[p][pl.ds(g, page, stride=n_groups), :].astype(BF16)
            else:
                v_p = _pad_rows(new_slc_ref[:, vs], page).astype(BF16)
            o_slc = o_slc + _dot(prob[:, p * page:(p + 1) * page], v_p)

        kw = jnp.concatenate([win_rows[pl.ds(g, n_buf, stride=2 * n_groups), :],
                              _pad_rows(new_win_ref[:, gs], page)], axis=0).astype(BF16)
        vw = jnp.concatenate([win_rows[pl.ds(n_groups + g, n_buf, stride=2 * n_groups), :],
                              _pad_rows(new_win_ref[:, vs], page)], axis=0).astype(BF16)
        s_w = _dot_nt(q4, kw) * ATTN_SCALE
        s_w = jnp.where(w_mask[None], s_w.reshape(REP, t_new, n_buf + page), NEG_INF).reshape(rows, n_buf + page)
        o_win = _dot(_softmax_rows(s_w).astype(BF16), vw)

        gt = gates[g]
        o_ref[g] = (gt[:, 0:1] * o_cmp[g] + gt[:, 1:2] * o_slc + gt[:, 2:3] * o_win).astype(o_ref.dtype)

    win_out_ref[0:n_buf - t_new] = win_ref[t_new:n_buf]
    for c in range(2 * n_groups):
        win_out_ref[n_buf - t_new:n_buf, c // n_groups, c % n_groups, :] = new_win_ref[:, c * HEAD_DIM:(c + 1) * HEAD_DIM]


def nsa_sample(pt_flat, nsa_cache, win_state, layer, q_s, cmp_s, new_rows, slc_col_block, win_col_block, gates_s,
               n_seq, n_pages, n_sel, past_len):
    page, _, n_groups, _ = nsa_cache.shape[2:]
    gw = n_groups * HEAD_DIM
    t_new = new_rows.shape[0] // n_seq
    rows = REP * t_new
    nbp = cmp_s.shape[3]
    n_buf = win_state.shape[2]
    spec = lambda p, slot: pl.BlockSpec((None, None, page, None, n_groups, HEAD_DIM),
                                        lambda b, pt: (layer, pt[b * n_pages + p], 0, slot, 0, 0))
    grid_spec = pltpu.PrefetchScalarGridSpec(
        num_scalar_prefetch=1,
        grid=(n_seq,),
        in_specs=[spec(p, 2) for p in range(n_pages)] + [spec(p, 3) for p in range(n_pages)] + [
            pl.BlockSpec((None, n_groups, rows, HEAD_DIM), lambda b, pt: (b, 0, 0, 0)),
            pl.BlockSpec((None, 2, n_groups, nbp, HEAD_DIM), lambda b, pt: (b, 0, 0, 0, 0)),
            pl.BlockSpec((t_new, 2 * gw), lambda b, pt: (b, slc_col_block)),
            pl.BlockSpec((t_new, 2 * gw), lambda b, pt: (b, win_col_block)),
            pl.BlockSpec((None, None, n_buf, 2, n_groups, HEAD_DIM), lambda b, pt: (layer, b, 0, 0, 0, 0)),
            pl.BlockSpec((None, n_groups, rows, 16), lambda b, pt: (b, 0, 0, 0)),
        ],
        out_specs=[pl.BlockSpec((None, n_groups, rows, HEAD_DIM), lambda b, pt: (b, 0, 0, 0)),
                   pl.BlockSpec((None, n_buf, 2, n_groups, HEAD_DIM), lambda b, pt: (b, 0, 0, 0, 0))],
        scratch_shapes=[pltpu.VMEM((n_pages, page * n_groups, HEAD_DIM), F32),
                        pltpu.VMEM((n_pages, page * n_groups, HEAD_DIM), F32),
                        pltpu.VMEM((1, n_buf * 2 * n_groups, HEAD_DIM), F32)],
    )
    assert n_buf > t_new
    return pl.pallas_call(
        functools.partial(_nsa_sample_kernel, n_pages=n_pages, n_groups=n_groups, n_sel=n_sel, past_len=past_len),
        grid_spec=grid_spec,
        out_shape=[jax.ShapeDtypeStruct((n_seq, n_groups, rows, HEAD_DIM), BF16),
                   jax.ShapeDtypeStruct(win_state.shape[1:], F32)],
        compiler_params=_params(("arbitrary",), 56),
        name="nsa_sample",
    )(pt_flat, *([nsa_cache] * (2 * n_pages)), q_s, cmp_s, new_rows, new_rows, win_state, gates_s)


def _rope_tables(pos):
    half = HEAD_DIM // 2
    inv_freq = ROPE_THETA ** (-jnp.arange(half, dtype=F32) / half)
    ang = pos.astype(F32)[:, None] * inv_freq[None, :]
    cos, sin = jnp.cos(ang), jnp.sin(ang)
    return jnp.concatenate([cos, cos], axis=-1), jnp.concatenate([-sin, sin], axis=-1)


def _heads_major(x, n_seq, t_new, n_groups):
    w = x.shape[1] // (n_groups * REP)
    return x.reshape(n_seq, t_new, n_groups, REP, w).transpose(0, 2, 3, 1, 4).reshape(n_seq, n_groups, REP * t_new, w)


def _tokens_major(x, n_seq, t_new, n_groups):
    w = x.shape[-1]
    return x.reshape(n_seq, n_groups, REP, t_new, w).transpose(0, 3, 1, 2, 4).reshape(n_seq * t_new, n_groups * REP * w)


def _values_t(v, n_groups):
    t = v.shape[0]
    vt = v.T.reshape(n_groups, HEAD_DIM, t)
    return jnp.concatenate([vt, jnp.ones((n_groups, V_ROWS - HEAD_DIM, t), v.dtype)], axis=1)


def _head_lanes(x, n_groups, w):
    t = x.shape[0]
    nq = t // Q_TILE
    x = x.reshape(nq, Q_TILE, n_groups, REP, w).transpose(2, 0, 4, 3, 1)
    return x.reshape(n_groups, nq, w, REP * Q_TILE)


def kernel(x_prompt, x_sample, cache_fox_kv, cache_fox_logf, cache_nsa_kv, state_nsa_win, page_table, p_prompt, p_sample, g_mix, w_in, b_fgate, cmp_pe_k, cmp_w1_k, cmp_w2_k, cmp_pe_v, cmp_w1_v, cmp_w2_v, w_out, g_ffn, w_gate, w_up, w_down, g_ple, w_ple_gate, w_ple_proj, g_final):
    depth = w_in.shape[0]
    _, seq, d_model = x_prompt.shape
    n_seq, t_new, _ = x_sample.shape
    page = cache_fox_kv.shape[2]
    n_pages = page_table.shape[1]
    past_len = n_pages * page
    fox_heads = b_fgate.shape[1]
    n_kv = cache_fox_kv.shape[4]
    n_groups = cache_nsa_kv.shape[4]
    nsa_heads = n_groups * REP
    assert fox_heads == n_kv * REP and w_out.shape[1] == (fox_heads + nsa_heads) * HEAD_DIM
    fq_w, nq_w = fox_heads * HEAD_DIM, nsa_heads * HEAD_DIM
    fkv_w, nkv_w = n_kv * HEAD_DIM, n_groups * HEAD_DIM
    n_gate = 3 * nsa_heads
    assert fox_heads + n_gate <= LANE and seq % K_TILE == 0 and seq >= WINDOW + Q_TILE and 3 * REP <= HEAD_DIM

    sizes = [fq_w, fkv_w, fkv_w, fox_heads, nq_w] + [nkv_w] * 6 + [n_gate]
    off = np.concatenate([[0], np.cumsum(sizes)]).astype(int)
    o_fq, o_fk, o_fv, o_fl, o_nq, o_kc, o_vc, o_ks, o_vs, o_kw, o_vw, o_g = [int(v) for v in off[:-1]]
    c_fq, c_nq = 0, fq_w
    c_fk = c_nq + nq_w
    c_fv = c_fk + fkv_w
    c_kc = c_fv + fkv_w
    c_vc, c_ks, c_vs, c_kw, c_vw = (c_kc + nkv_w * k for k in range(1, 6))
    n_main = c_vw + nkv_w
    tn = min(512, fkv_w, nkv_w)
    rope_ranges = ((c_nq, c_nq + nq_w), (c_kc, c_kc + nkv_w), (c_ks, c_ks + nkv_w), (c_kw, c_kw + nkv_w))

    nb_p = seq // BLOCK
    nbp_p = max(LANE, nb_p)
    nb_s = -(-(past_len + t_new) // BLOCK)
    nbp_s = LANE
    assert nb_s == 2 * n_pages + 1 and nb_s <= LANE and nbp_p % LANE == 0
    n_win_p = min(WINDOW, seq)
    nq_tiles = seq // Q_TILE

    cos_p, sin_p = _rope_tables(jnp.arange(seq))
    cos_s, sin_s = _rope_tables(jnp.tile(past_len + jnp.arange(t_new), n_seq))
    expand_t = jnp.asarray((np.arange(seq)[:, None] // BLOCK) == np.arange(nbp_p)[None, :], BF16)
    aug_np = np.zeros((HEAD_DIM, REP * Q_TILE), np.float32)
    for r in range(REP):
        aug_np[3 * r:3 * r + 3, r * Q_TILE:(r + 1) * Q_TILE] = 1.0
    aug = jnp.asarray(aug_np, BF16)
    pt_flat = page_table.reshape(-1).astype(jnp.int32)

    hp = x_prompt.reshape(seq, d_model)
    hs = x_sample.reshape(n_seq * t_new, d_model)
    outs_p, outs_s = [], []
    for i in range(depth):
        wi = w_in[i]
        w_main = jnp.concatenate(
            [wi[:, o_fq:o_fq + fq_w], wi[:, o_nq:o_nq + nq_w], wi[:, o_fk:o_fl], wi[:, o_kc:o_g]], axis=1).astype(BF16)
        w_small = jnp.concatenate(
            [wi[:, o_fl:o_fl + fox_heads], wi[:, o_g:o_g + n_gate],
             jnp.zeros((d_model, LANE - fox_heads - n_gate), F32)], axis=1).astype(BF16)
        b_small = jnp.concatenate([b_fgate[i], jnp.zeros((LANE - fox_heads,), F32)]).reshape(1, LANE)
        w_o = w_out[i].astype(BF16)
        w_oa, w_ob = w_o[:fq_w], w_o[fq_w:]
        w_g, w_u, w_d = w_gate[i].astype(BF16), w_up[i].astype(BF16), w_down[i].astype(BF16)
        w_pg, w_pp = w_ple_gate[i].astype(BF16), w_ple_proj[i].astype(BF16)
        pe = jnp.stack([cmp_pe_k[i], cmp_pe_v[i]])
        hidden = cmp_w1_k.shape[-1]
        w1 = jnp.stack([cmp_w1_k[i], cmp_w1_v[i]]).reshape(2, BLOCK, HEAD_DIM, hidden).astype(BF16)
        w1_pairs = w1.reshape(2, BLOCK // 2, 2 * HEAD_DIM, hidden)
        w2 = jnp.stack([cmp_w2_k[i], cmp_w2_v[i]]).astype(BF16)

        def tail(h, o_fox, o_nsa, p):
            h1 = outproj(o_fox, o_nsa, w_oa, w_ob, h)
            act = gateup(rmsnorm(h1, g_ffn[i], BF16), w_g, w_u)
            h2 = downproj(act, w_d, h1)
            return ple(rmsnorm(h2, g_ple[i], BF16), w_pg, p.astype(BF16), w_pp, h2)

        xn = rmsnorm(hp, g_mix[i], BF16)
        pf, pb = inproj(xn, w_main, cos_p, sin_p, tn, rope_ranges)
        small = smallproj(xn, w_small, b_small, fox_heads)
        logf = small[:, :fox_heads]
        f_cum, fp1, fp2, fp3 = cumsum_rows(logf)
        pieces = jnp.stack([fp1, fp2, fp3], axis=-1)
        pieces = pieces.reshape(seq, n_kv, REP * 3).transpose(1, 0, 2)
        k_fox = pb[:, c_fk:c_fk + fkv_w].reshape(seq, n_kv, HEAD_DIM).transpose(1, 0, 2)
        k_aug = jnp.concatenate([k_fox, pieces, jnp.zeros((n_kv, seq, HEAD_DIM - REP * 3), BF16)], axis=-1)
        first_tiles = fox_first_tiles(pb[:, c_fq:c_fq + fq_w], pb[:, c_fk:c_fk + fkv_w], f_cum, n_kv)
        o_fox = fox_prompt(first_tiles, pb, c_fq // (REP * HEAD_DIM), aug, k_aug,
                           _values_t(pb[:, c_fv:c_fv + fkv_w], n_kv), _head_lanes(f_cum, n_kv, 1))
        cmp_p = compress_prompt(pf, (c_kc // HEAD_DIM, c_vc // HEAD_DIM), pe, w1, w2, n_groups, nbp_p)
        cmp_vt = cmp_p[1].transpose(0, 2, 1).astype(BF16)
        gate_lanes = _head_lanes(small[:, fox_heads:fox_heads + n_gate], n_groups, 3)
        gate_lanes = jnp.pad(gate_lanes, ((0, 0), (0, 0), (0, 5), (0, 0)))
        k_slc = pb[:, c_ks:c_ks + nkv_w].reshape(seq, n_groups, HEAD_DIM).transpose(1, 0, 2)
        k_aug_nsa = jnp.concatenate([k_slc, jnp.broadcast_to(expand_t, (n_groups,) + expand_t.shape)], axis=-1)
        o_nsa = nsa_prompt(pb, c_nq // (REP * HEAD_DIM), k_aug_nsa, c_kw // HEAD_DIM,
                           _values_t(pb[:, c_vs:c_vs + nkv_w], n_groups), _values_t(pb[:, c_vw:c_vw + nkv_w], n_groups),
                           cmp_p[0], cmp_vt, gate_lanes, min(SEL_TOP_N, nb_p))
        hp = tail(hp, o_fox, o_nsa, p_prompt[i].reshape(seq, -1))
        outs_p.append((
            pf[:, c_fk:c_fk + 2 * fkv_w].reshape(1, seq, 2, n_kv, HEAD_DIM),
            logf.reshape(1, seq, fox_heads),
            pf[:, c_kc:c_kc + 4 * nkv_w].reshape(1, seq, 4, n_groups, HEAD_DIM),
            pf[seq - n_win_p:, c_kw:c_kw + 2 * nkv_w].reshape(1, n_win_p, 2, n_groups, HEAD_DIM),
        ))

        m_s = n_seq * t_new
        xs = rmsnorm(hs, g_mix[i], BF16)
        sf, sb = inproj(xs, w_main, cos_s, sin_s, tn, rope_ranges)
        small_s = smallproj(xs, w_small, b_small, fox_heads)
        logf_s = small_s[:, :fox_heads]
        logf_pool_t = cache_fox_logf[i].transpose(0, 2, 1)
        new_t = jnp.pad(logf_s.reshape(n_seq, t_new, fox_heads).transpose(0, 2, 1),
                        ((0, 0), (0, 0), (0, LANE - t_new)))
        assert c_fk % (2 * fkv_w) == 0 and c_kc % nkv_w == 0 and c_ks % (2 * nkv_w) == 0 and c_kw % (2 * nkv_w) == 0
        q_fox_s = _heads_major(sb[:, c_fq:c_fq + fq_w], n_seq, t_new, n_kv)
        o_fox_s = fox_sample(pt_flat, cache_fox_kv, i, logf_pool_t, new_t, q_fox_s, sf, c_fk // (2 * fkv_w),
                             n_seq, n_pages)
        cmp_raw = compress_sample(pt_flat, cache_nsa_kv, i, sf, c_kc // nkv_w, pe, w1_pairs, w2, n_seq, n_pages, nb_s)
        cmp_s = cmp_raw[:, :, :nb_s * n_groups].reshape(n_seq, 2, nb_s, n_groups, HEAD_DIM).transpose(0, 1, 3, 2, 4)
        cmp_s = jnp.pad(cmp_s, ((0, 0), (0, 0), (0, 0), (0, nbp_s - nb_s), (0, 0)))
        q_nsa_s = _heads_major(sb[:, c_nq:c_nq + nq_w], n_seq, t_new, n_groups)
        gates_s = _heads_major(small_s[:, fox_heads:fox_heads + n_gate], n_seq, t_new, n_groups)
        gates_s = jnp.pad(gates_s, ((0, 0), (0, 0), (0, 0), (0, 13)))
        o_nsa_s, new_win = nsa_sample(pt_flat, cache_nsa_kv, state_nsa_win, i, q_nsa_s, cmp_s, sf,
                                      c_ks // (2 * nkv_w), c_kw // (2 * nkv_w), gates_s,
                                      n_seq, n_pages, min(SEL_TOP_N, nb_s), past_len)
        hs = tail(hs, _tokens_major(o_fox_s, n_seq, t_new, n_kv), _tokens_major(o_nsa_s, n_seq, t_new, n_groups),
                  p_sample[i].reshape(m_s, -1))
        outs_s.append((
            sf[:, c_fk:c_fk + 2 * fkv_w].reshape(n_seq, t_new, 2, n_kv, HEAD_DIM),
            logf_s.reshape(n_seq, t_new, fox_heads),
            sf[:, c_kc:c_kc + 4 * nkv_w].reshape(n_seq, t_new, 4, n_groups, HEAD_DIM),
            new_win,
        ))

    y_prompt = rmsnorm(hp, g_final, F32).reshape(x_prompt.shape)
    y_sample = rmsnorm(hs, g_final, F32).reshape(x_sample.shape)
    stack = lambda outs, j: jnp.stack([r[j] for r in outs], axis=0)
    return (y_prompt, y_sample, stack(outs_p, 0), stack(outs_p, 1), stack(outs_p, 2), stack(outs_p, 3),
            stack(outs_s, 0), stack(outs_s, 1), stack(outs_s, 2), stack(outs_s, 3))
```

```python
import functools

import numpy as np
import jax
import jax.numpy as jnp
from jax import lax
from jax.experimental import pallas as pl
from jax.experimental.pallas import tpu as pltpu

HEAD_DIM = 128
REP = 4
BLOCK = 64
SEL_TOP_N = 16
WINDOW = 512
Q_TILE = 512
K_TILE = 512
V_ROWS = 144
ROPE_THETA = 10000.0
RMS_EPS = 1e-6
ATTN_SCALE = HEAD_DIM ** -0.5
LOG2E = 1.4426950408889634
SCALE2 = ATTN_SCALE * LOG2E
NEG_INF = -1e30
SKIP_MARGIN = 40.0
REMOVED = -3e38
FORCE_BONUS = 1e4
LANE = 128
MIB = 1024 * 1024

F32 = jnp.float32
BF16 = jnp.bfloat16


def _params(sem, vmem_mib):
    return pltpu.CompilerParams(dimension_semantics=sem, vmem_limit_bytes=vmem_mib * MIB)


def _dot(a, b):
    return jnp.dot(a, b, preferred_element_type=F32)


def _dot_nt(a, b):
    return lax.dot_general(a, b, (((1,), (1,)), ((), ())), preferred_element_type=F32)


def _row_tile(m, cap):
    t = min(m, cap)
    assert m % t == 0
    return t


def _iota(shape, dim):
    return lax.broadcasted_iota(jnp.int32, shape, dim)


def _rmsnorm_kernel(x_ref, g_ref, o_ref):
    x = x_ref[...]
    y = x * lax.rsqrt(jnp.mean(x * x, axis=-1, keepdims=True) + RMS_EPS)
    o_ref[...] = (y * g_ref[...]).astype(o_ref.dtype)


def rmsnorm(x, g, out_dtype):
    m, d = x.shape
    tm = _row_tile(m, 256)
    return pl.pallas_call(
        _rmsnorm_kernel,
        grid=(m // tm,),
        in_specs=[pl.BlockSpec((tm, d), lambda i: (i, 0)), pl.BlockSpec((1, d), lambda i: (0, 0))],
        out_specs=pl.BlockSpec((tm, d), lambda i: (i, 0)),
        out_shape=jax.ShapeDtypeStruct((m, d), out_dtype),
        compiler_params=_params(("parallel",), 32),
        name="rmsnorm",
    )(x, g.reshape(1, d))


def _inproj_kernel(x_ref, w_ref, cos_ref, sin_ref, of_ref, ob_ref, *, tn, rope_ranges):
    acc = _dot(x_ref[...], w_ref[...])
    col0 = pl.program_id(1) * tn
    is_rope = None
    for lo, hi in rope_ranges:
        hit = (col0 >= lo) & (col0 < hi)
        is_rope = hit if is_rope is None else (is_rope | hit)

    @pl.when(is_rope)
    def _():
        cos = cos_ref[...]
        sin = sin_ref[...]
        for h in range(tn // HEAD_DIM):
            sl = slice(h * HEAD_DIM, (h + 1) * HEAD_DIM)
            xh = acc[:, sl]
            r = xh * cos + pltpu.roll(xh, HEAD_DIM // 2, 1) * sin
            of_ref[:, sl] = r
            ob_ref[:, sl] = r.astype(BF16)

    @pl.when(jnp.logical_not(is_rope))
    def _():
        of_ref[...] = acc
        ob_ref[...] = acc.astype(BF16)


def inproj(xn, w_main, cos, sin, tn, rope_ranges):
    m, k = xn.shape
    n = w_main.shape[1]
    tm = _row_tile(m, 1024)
    return pl.pallas_call(
        functools.partial(_inproj_kernel, tn=tn, rope_ranges=rope_ranges),
        grid=(m // tm, n // tn),
        in_specs=[
            pl.BlockSpec((tm, k), lambda i, j: (i, 0)),
            pl.BlockSpec((k, tn), lambda i, j: (0, j)),
            pl.BlockSpec((tm, HEAD_DIM), lambda i, j: (i, 0)),
            pl.BlockSpec((tm, HEAD_DIM), lambda i, j: (i, 0)),
        ],
        out_specs=[pl.BlockSpec((tm, tn), lambda i, j: (i, j)), pl.BlockSpec((tm, tn), lambda i, j: (i, j))],
        out_shape=[jax.ShapeDtypeStruct((m, n), F32), jax.ShapeDtypeStruct((m, n), BF16)],
        compiler_params=_params(("parallel", "arbitrary"), 48),
        name="inproj",
    )(xn, w_main, cos, sin)


def _smallproj_kernel(x_ref, w_ref, b_ref, o_ref, *, n_logf):
    v = _dot(x_ref[...], w_ref[...])
    z = v + b_ref[...]
    logf = -(jnp.maximum(-z, 0.0) + jnp.log1p(jnp.exp(-jnp.abs(z))))
    gate = jax.nn.sigmoid(v)
    lane = _iota(v.shape, 1)
    o_ref[...] = jnp.where(lane < n_logf, logf, gate)


def smallproj(xn, w_small, b_small, n_logf):
    m, k = xn.shape
    tm = _row_tile(m, 1024)
    return pl.pallas_call(
        functools.partial(_smallproj_kernel, n_logf=n_logf),
        grid=(m // tm,),
        in_specs=[
            pl.BlockSpec((tm, k), lambda i: (i, 0)),
            pl.BlockSpec((k, LANE), lambda i: (0, 0)),
            pl.BlockSpec((1, LANE), lambda i: (0, 0)),
        ],
        out_specs=pl.BlockSpec((tm, LANE), lambda i: (i, 0)),
        out_shape=jax.ShapeDtypeStruct((m, LANE), F32),
        compiler_params=_params(("parallel",), 32),
        name="smallproj",
    )(xn, w_small, b_small)


def _outproj_kernel(a_ref, b_ref, wa_ref, wb_ref, h_ref, o_ref):
    o_ref[...] = h_ref[...] + (_dot(a_ref[...], wa_ref[...]) + _dot(b_ref[...], wb_ref[...]))


def outproj(o_fox, o_nsa, w_a, w_b, h):
    m, ka = o_fox.shape
    kb = o_nsa.shape[1]
    n = w_a.shape[1]
    tm, tn = _row_tile(m, 1024), 512
    return pl.pallas_call(
        _outproj_kernel,
        grid=(m // tm, n // tn),
        in_specs=[
            pl.BlockSpec((tm, ka), lambda i, j: (i, 0)),
            pl.BlockSpec((tm, kb), lambda i, j: (i, 0)),
            pl.BlockSpec((ka, tn), lambda i, j: (0, j)),
            pl.BlockSpec((kb, tn), lambda i, j: (0, j)),
            pl.BlockSpec((tm, tn), lambda i, j: (i, j)),
        ],
        out_specs=pl.BlockSpec((tm, tn), lambda i, j: (i, j)),
        out_shape=jax.ShapeDtypeStruct((m, n), F32),
        compiler_params=_params(("parallel", "arbitrary"), 48),
        name="outproj",
    )(o_fox, o_nsa, w_a, w_b, h)


def _gateup_kernel(x_ref, wg_ref, wu_ref, o_ref):
    x = x_ref[...]
    g = _dot(x, wg_ref[...])
    u = _dot(x, wu_ref[...])
    o_ref[...] = (g * jax.nn.sigmoid(g) * u).astype(o_ref.dtype)


def gateup(xn, w_gate, w_up):
    m, k = xn.shape
    n = w_gate.shape[1]
    tm, tn = _row_tile(m, 2048), 256
    return pl.pallas_call(
        _gateup_kernel,
        grid=(m // tm, n // tn),
        in_specs=[
            pl.BlockSpec((tm, k), lambda i, j: (i, 0)),
            pl.BlockSpec((k, tn), lambda i, j: (0, j)),
            pl.BlockSpec((k, tn), lambda i, j: (0, j)),
        ],
        out_specs=pl.BlockSpec((tm, tn), lambda i, j: (i, j)),
        out_shape=jax.ShapeDtypeStruct((m, n), BF16),
        compiler_params=_params(("parallel", "arbitrary"), 56),
        name="gateup",
    )(xn, w_gate, w_up)


def _down_kernel(x_ref, w_ref, h_ref, o_ref):
    o_ref[...] = h_ref[...] + _dot(x_ref[...], w_ref[...])


def downproj(act, w_down, h):
    m, k = act.shape
    n = w_down.shape[1]
    tm, tn = _row_tile(m, 512), 512
    return pl.pallas_call(
        _down_kernel,
        grid=(m // tm, n // tn),
        in_specs=[
            pl.BlockSpec((tm, k), lambda i, j: (i, 0)),
            pl.BlockSpec((k, tn), lambda i, j: (0, j)),
            pl.BlockSpec((tm, tn), lambda i, j: (i, j)),
        ],
        out_specs=pl.BlockSpec((tm, tn), lambda i, j: (i, j)),
        out_shape=jax.ShapeDtypeStruct((m, n), F32),
        compiler_params=_params(("parallel", "arbitrary"), 58),
        name="downproj",
    )(act, w_down, h)


def _ple_kernel(x_ref, wg_ref, p_ref, wp_ref, h_ref, o_ref):
    gate = jax.nn.sigmoid(_dot(x_ref[...], wg_ref[...]))
    o_ref[...] = h_ref[...] + gate * _dot(p_ref[...], wp_ref[...])


def ple(xn, w_gate, p, w_proj, h):
    m, k = xn.shape
    kp = p.shape[1]
    n = w_gate.shape[1]
    tm, tn = _row_tile(m, 1024), 512
    return pl.pallas_call(
        _ple_kernel,
        grid=(m // tm, n // tn),
        in_specs=[
            pl.BlockSpec((tm, k), lambda i, j: (i, 0)),
            pl.BlockSpec((k, tn), lambda i, j: (0, j)),
            pl.BlockSpec((tm, kp), lambda i, j: (i, 0)),
            pl.BlockSpec((kp, tn), lambda i, j: (0, j)),
            pl.BlockSpec((tm, tn), lambda i, j: (i, j)),
        ],
        out_specs=pl.BlockSpec((tm, tn), lambda i, j: (i, j)),
        out_shape=jax.ShapeDtypeStruct((m, n), F32),
        compiler_params=_params(("parallel", "arbitrary"), 48),
        name="ple",
    )(xn, w_gate, p, w_proj, h)


def _split3(x):
    x1 = x.astype(BF16)
    r1 = x - x1.astype(F32)
    x2 = r1.astype(BF16)
    x3 = (r1 - x2.astype(F32)).astype(BF16)
    return x1, x2, x3


def _cumsum_rows_kernel(x_ref, tri_ref, o_ref, p1_ref, p2_ref, p3_ref, carry_ref):
    @pl.when(pl.program_id(0) == 0)
    def _():
        carry_ref[...] = jnp.zeros_like(carry_ref)

    x1, x2, x3 = _split3(x_ref[...])
    tri = tri_ref[...]
    out = (_dot(tri, x1) + _dot(tri, x2) + _dot(tri, x3)) + carry_ref[...]
    o_ref[...] = out
    carry_ref[...] = out[out.shape[0] - 1:, :]
    p1_ref[...], p2_ref[...], p3_ref[...] = _split3(out * (-1.0 / ATTN_SCALE))


def cumsum_rows(x):
    t, h = x.shape
    c = _row_tile(t, 512)
    tri = jnp.asarray(np.tril(np.ones((c, c), np.float32)), BF16)
    row_spec = pl.BlockSpec((c, h), lambda i: (i, 0))
    return pl.pallas_call(
        _cumsum_rows_kernel,
        grid=(t // c,),
        in_specs=[row_spec, pl.BlockSpec((c, c), lambda i: (0, 0))],
        out_specs=[row_spec] * 4,
        out_shape=[jax.ShapeDtypeStruct((t, h), F32)] + [jax.ShapeDtypeStruct((t, h), BF16)] * 3,
        scratch_shapes=[pltpu.VMEM((1, h), F32)],
        compiler_params=_params(("arbitrary",), 32),
        name="cumsum_rows",
    )(x, tri)


def _cumsum_lanes(chunk_refs, tri_ones):
    h = chunk_refs[0].shape[0]
    parts = []
    for ref in chunk_refs:
        y = _dot(jnp.concatenate(_split3(ref[...]), axis=0), tri_ones)
        parts.append(y[0:h] + y[h:2 * h] + y[2 * h:3 * h])
    carry = jnp.zeros((h, LANE), F32)
    outs = []
    for part in parts:
        outs.append(part[:, :LANE] + carry)
        carry = carry + part[:, LANE:]
    return outs


def _silu(x):
    return x * jax.nn.sigmoid(x)


def _compress_prompt_kernel(cb_ref, x_ref, pe_ref, w1_ref, w2_ref, o_ref, *, nb):
    hidden = w1_ref.shape[2]

    def body(l, acc):
        xl = x_ref[pl.ds(l, nb, stride=BLOCK), :] + pe_ref[pl.ds(l, 1), :]
        return acc + _dot(xl.astype(BF16), w1_ref[l])

    acc = lax.fori_loop(0, BLOCK, body, jnp.zeros((nb, hidden), F32))
    out = _dot(_silu(acc).astype(BF16), w2_ref[...])
    o_ref[...] = jnp.zeros_like(o_ref)
    o_ref[0:nb, :] = out


def compress_prompt(proj_f32, col_blocks, pe, w1, w2, n_groups, nbp):
    t = proj_f32.shape[0]
    nb = t // BLOCK
    hidden = w1.shape[-1]
    cb = jnp.asarray(col_blocks, jnp.int32)
    grid_spec = pltpu.PrefetchScalarGridSpec(
        num_scalar_prefetch=1,
        grid=(2, n_groups),
        in_specs=[
            pl.BlockSpec((t, HEAD_DIM), lambda kv, g, cb: (0, cb[kv] + g)),
            pl.BlockSpec((None, BLOCK, HEAD_DIM), lambda kv, g, cb: (kv, 0, 0)),
            pl.BlockSpec((None, BLOCK, HEAD_DIM, hidden), lambda kv, g, cb: (kv, 0, 0, 0)),
            pl.BlockSpec((None, hidden, HEAD_DIM), lambda kv, g, cb: (kv, 0, 0)),
        ],
        out_specs=pl.BlockSpec((None, None, nbp, HEAD_DIM), lambda kv, g, cb: (kv, g, 0, 0)),
    )
    return pl.pallas_call(
        functools.partial(_compress_prompt_kernel, nb=nb),
        grid_spec=grid_spec,
        out_shape=jax.ShapeDtypeStruct((2, n_groups, nbp, HEAD_DIM), F32),
        compiler_params=_params(("arbitrary", "arbitrary"), 40),
        name="compress_prompt",
    )(cb, proj_f32, pe, w1, w2)


def _compress_sample_kernel(pt_ref, *refs, n_pages, n_groups, nb):
    pages = refs[:n_pages]
    new_ref, pe_ref, w1_ref, w2_ref, o_ref, newblk_ref = refs[n_pages:]
    t_new = new_ref.shape[0]
    hidden = w1_ref.shape[2]
    rows_pad = o_ref.shape[0]
    n_rows = (2 * n_pages + 1) * n_groups

    newblk_ref[...] = jnp.zeros_like(newblk_ref)
    for l in range(t_new):
        for g in range(n_groups):
            newblk_ref[l, g:g + 1, :] = new_ref[l:l + 1, g * HEAD_DIM:(g + 1) * HEAD_DIM]

    def gather(l):
        rr = []
        for pg in pages:
            rr.append(pg[l])
            rr.append(pg[l + BLOCK])
        rr.append(newblk_ref[l])
        if rows_pad > n_rows:
            rr.append(jnp.zeros((rows_pad - n_rows, HEAD_DIM), F32))
        return jnp.concatenate(rr, axis=0) + pe_ref[l:l + 1, :]

    acc = jnp.zeros((rows_pad, hidden), F32)
    for l2 in range(BLOCK // 2):
        x = jnp.concatenate([gather(2 * l2), gather(2 * l2 + 1)], axis=1).astype(BF16)
        acc = acc + _dot(x, w1_ref[l2])
    out = _dot(_silu(acc).astype(BF16), w2_ref[...])
    o_ref[...] = jnp.where(_iota(out.shape, 0) < nb * n_groups, out, 0.0)


def compress_sample(pt_flat, nsa_cache, layer, new_rows, new_col_block, pe, w1_pairs, w2, n_seq, n_pages, nb):
    page, _, n_groups, _ = nsa_cache.shape[2:]
    assert page == 2 * BLOCK
    gw = n_groups * HEAD_DIM
    hidden = w1_pairs.shape[-1]
    t_new = new_rows.shape[0] // n_seq
    rows_pad = -(-((2 * n_pages + 1) * n_groups) // 8) * 8
    spec = lambda p: pl.BlockSpec((None, None, page, None, n_groups, HEAD_DIM),
                                  lambda kv, b, pt: (layer, pt[b * n_pages + p], 0, kv, 0, 0))
    grid_spec = pltpu.PrefetchScalarGridSpec(
        num_scalar_prefetch=1,
        grid=(2, n_seq),
        in_specs=[spec(p) for p in range(n_pages)] + [
            pl.BlockSpec((t_new, gw), lambda kv, b, pt: (b, new_col_block + kv)),
            pl.BlockSpec((None, BLOCK, HEAD_DIM), lambda kv, b, pt: (kv, 0, 0)),
            pl.BlockSpec((None, BLOCK // 2, 2 * HEAD_DIM, hidden), lambda kv, b, pt: (kv, 0, 0, 0)),
            pl.BlockSpec((None, hidden, HEAD_DIM), lambda kv, b, pt: (kv, 0, 0)),
        ],
        out_specs=pl.BlockSpec((None, None, rows_pad, HEAD_DIM), lambda kv, b, pt: (b, kv, 0, 0)),
        scratch_shapes=[pltpu.VMEM((BLOCK, n_groups, HEAD_DIM), F32)],
    )
    return pl.pallas_call(
        functools.partial(_compress_sample_kernel, n_pages=n_pages, n_groups=n_groups, nb=nb),
        grid_spec=grid_spec,
        out_shape=jax.ShapeDtypeStruct((n_seq, 2, rows_pad, HEAD_DIM), F32),
        compiler_params=_params(("arbitrary", "arbitrary"), 40),
        name="compress_sample",
    )(pt_flat, *([nsa_cache] * n_pages), new_rows, pe, w1_pairs, w2)


def _softmax_rows(s):
    m = jnp.max(s, axis=-1, keepdims=True)
    e = jnp.exp(s - m)
    return e / jnp.sum(e, axis=-1, keepdims=True)


def _softmax_cols(s):
    m = jnp.max(s, axis=0, keepdims=True)
    e = jnp.exp(s - m)
    return e / jnp.sum(e, axis=0, keepdims=True)


def _top_n_mask_t(score_t, blk_t, n_sel, n_blocks_pad):
    sel = jnp.zeros(score_t.shape, F32)
    for _ in range(n_sel):
        mx = jnp.max(score_t, axis=0, keepdims=True)
        idx = jnp.min(jnp.where(score_t == mx, blk_t, n_blocks_pad), axis=0, keepdims=True)
        hit = blk_t == idx
        sel = jnp.where(hit, 1.0, sel)
        score_t = jnp.where(hit, REMOVED, score_t)
    return sel


def _selection_scores_t(imp_t, blk_t, qpos_t):
    cur = qpos_t // BLOCK
    forced = (blk_t == 0) | (blk_t == cur) | (blk_t == cur - 1)
    avail = blk_t * BLOCK <= qpos_t
    return jnp.where(avail, imp_t + jnp.where(forced, FORCE_BONUS, 0.0), NEG_INF)


def _online_tile_t(x, vt_tile, m, acc_ref, shift):
    mx = jnp.max(x, axis=0, keepdims=True) * SCALE2
    if shift is not None:
        mx = mx + shift
    m_new = jnp.maximum(m, mx)
    off = m_new if shift is None else m_new - shift
    p = jnp.exp2(x * SCALE2 - off)
    acc_ref[...] = jnp.exp2(m - m_new) * acc_ref[...] + _dot(vt_tile, p.astype(BF16))
    return m_new


def _queries_t(q):
    return jnp.concatenate([q[:, r * HEAD_DIM:(r + 1) * HEAD_DIM].T for r in range(REP)], axis=1)


def _store_heads(o_ref, o_t):
    for r in range(REP):
        o_ref[:, r * HEAD_DIM:(r + 1) * HEAD_DIM] = o_t[:, r * Q_TILE:(r + 1) * Q_TILE].T.astype(o_ref.dtype)


def _fox_prompt_kernel(first_ref, q_ref, aug_ref, ka_ref, vt_ref, fq_ref, o_ref, acc_ref):
    q0 = pl.program_id(1) * Q_TILE
    cols = REP * Q_TILE
    qa = jnp.concatenate([_queries_t(q_ref[...]), aug_ref[...]], axis=0)
    fq2 = fq_ref[...] * LOG2E
    acc_ref[...] = jnp.zeros_like(acc_ref)
    n_full = q0 // K_TILE

    def scores(j):
        k0 = pl.multiple_of(j * K_TILE, K_TILE)
        return _dot(ka_ref[pl.ds(k0, K_TILE), :], qa)

    def update(j, x, m):
        k0 = pl.multiple_of(j * K_TILE, K_TILE)
        return _online_tile_t(x, vt_ref[:, pl.ds(k0, K_TILE)], m, acc_ref, fq2)

    first = first_ref[pl.program_id(0) * pl.num_programs(1) + pl.program_id(1)]
    m = lax.fori_loop(first, n_full, lambda j, m: update(j, scores(j), m), jnp.full((1, cols), NEG_INF, F32))
    kpos = n_full * K_TILE + _iota((K_TILE, cols), 0)
    qpos = q0 + (_iota((K_TILE, cols), 1) & (Q_TILE - 1))
    update(n_full, jnp.where(kpos <= qpos, scores(n_full), NEG_INF), m)
    acc = acc_ref[...]
    _store_heads(o_ref, acc[0:HEAD_DIM] / acc[HEAD_DIM:HEAD_DIM + 1])


def fox_first_tiles(q, k, f_cum, n_kv):
    assert Q_TILE == K_TILE
    t = q.shape[0]
    nt = t // K_TILE
    qn = jnp.sum(jnp.square(q.astype(F32)).reshape(nt, Q_TILE, n_kv, REP, HEAD_DIM), axis=-1)
    qmax = jnp.sqrt(jnp.max(qn, axis=(1, 3)))
    kn = jnp.sum(jnp.square(k.astype(F32)).reshape(nt, K_TILE, n_kv, HEAD_DIM), axis=-1)
    kmax = jnp.sqrt(jnp.max(kn, axis=1))
    f_first = f_cum[0::K_TILE].reshape(nt, n_kv, REP)
    f_last = f_cum[K_TILE - 1::K_TILE].reshape(nt, n_kv, REP)
    gap = jnp.max(f_first[:, None] - f_last[None, :], axis=-1)
    bound = 1.01 * ATTN_SCALE * qmax[:, None] * (kmax[None, :] + kmax[:, None]) + gap
    earlier = jnp.arange(nt)[None, :, None] < jnp.arange(nt)[:, None, None]
    skippable = (bound < -SKIP_MARGIN) & earlier
    tile_idx = jnp.arange(nt, dtype=jnp.int32)[None, :, None]
    first = jnp.min(jnp.where(skippable, nt, tile_idx), axis=1)
    return first.T.reshape(-1).astype(jnp.int32)


def fox_prompt(first_tiles, proj_bf16, q_blk0, aug, k_aug, v_t, fq_lanes):
    n_kv, _, t = v_t.shape
    cols = REP * Q_TILE
    grid_spec = pltpu.PrefetchScalarGridSpec(
        num_scalar_prefetch=1,
        grid=(n_kv, t // Q_TILE),
        in_specs=[
            pl.BlockSpec((Q_TILE, REP * HEAD_DIM), lambda g, i, first: (i, q_blk0 + g)),
            pl.BlockSpec((HEAD_DIM, cols), lambda g, i, first: (0, 0)),
            pl.BlockSpec((None, t, 2 * HEAD_DIM), lambda g, i, first: (g, 0, 0)),
            pl.BlockSpec((None, V_ROWS, t), lambda g, i, first: (g, 0, 0)),
            pl.BlockSpec((None, None, 1, cols), lambda g, i, first: (g, i, 0, 0)),
        ],
        out_specs=pl.BlockSpec((Q_TILE, REP * HEAD_DIM), lambda g, i, first: (i, g)),
        scratch_shapes=[pltpu.VMEM((V_ROWS, cols), F32)],
    )
    return pl.pallas_call(
        _fox_prompt_kernel,
        grid_spec=grid_spec,
        out_shape=jax.ShapeDtypeStruct((t, n_kv * REP * HEAD_DIM), BF16),
        compiler_params=_params(("arbitrary", "arbitrary"), 40),
        name="fox_prompt",
    )(first_tiles, proj_bf16, aug, k_aug, v_t, fq_lanes)


def _nsa_prompt_kernel(q_ref, ka_ref, vst_ref, kw_ref, vwt_ref, ck_ref, cvt_ref, gate_ref, wmask_ref, o_ref,
                       acc_ref, *, n_sel):
    q0 = pl.program_id(1) * Q_TILE
    cols = REP * Q_TILE
    nbp = ck_ref.shape[0]
    qt = _queries_t(q_ref[...])

    x_c = _dot(ck_ref[...].astype(BF16), qt) * ATTN_SCALE
    blk = _iota((nbp, cols), 0)
    qpos = q0 + (_iota((nbp, cols), 1) & (Q_TILE - 1))
    x_c = jnp.where((blk + 1) * BLOCK - 1 <= qpos, x_c, NEG_INF)
    p_c = _softmax_cols(x_c) * (qpos >= BLOCK - 1).astype(F32)
    o_cmp = _dot(cvt_ref[...], p_c.astype(BF16))

    imp_t = p_c[:, 0:Q_TILE]
    for r in range(1, REP):
        imp_t = imp_t + p_c[:, r * Q_TILE:(r + 1) * Q_TILE]
    blk_t = _iota((nbp, Q_TILE), 0)
    qpos_t = q0 + _iota((nbp, Q_TILE), 1)
    sel_t = _top_n_mask_t(_selection_scores_t(imp_t, blk_t, qpos_t), blk_t, n_sel, nbp)

    sel_bias = jnp.where(sel_t > 0.5, 0.0, NEG_INF).astype(BF16)
    qa = jnp.concatenate([qt, jnp.concatenate([sel_bias] * REP, axis=1)], axis=0)
    acc_ref[...] = jnp.zeros_like(acc_ref)
    n_full = q0 // K_TILE

    def scores(j):
        k0 = pl.multiple_of(j * K_TILE, K_TILE)
        return _dot(ka_ref[pl.ds(k0, K_TILE), :], qa)

    def update(j, x, m):
        k0 = pl.multiple_of(j * K_TILE, K_TILE)
        return _online_tile_t(x, vst_ref[:, pl.ds(k0, K_TILE)], m, acc_ref, None)

    m = lax.fori_loop(0, n_full, lambda j, m: update(j, scores(j), m), jnp.full((1, cols), NEG_INF, F32))
    kpos = n_full * K_TILE + _iota((K_TILE, cols), 0)
    causal = kpos <= q0 + (_iota((K_TILE, cols), 1) & (Q_TILE - 1))
    update(n_full, jnp.where(causal, scores(n_full), NEG_INF), m)
    acc = acc_ref[...]
    o_slc = acc[0:HEAD_DIM] / acc[HEAD_DIM:HEAD_DIM + 1]

    wlen = WINDOW + Q_TILE
    w0 = pl.multiple_of(jnp.maximum(q0 - WINDOW, 0), Q_TILE)
    x_w = _dot(kw_ref[pl.ds(w0, wlen), :], qt) * SCALE2
    x_w = x_w + jnp.concatenate([wmask_ref[...]] * REP, axis=1)
    p_w = jnp.exp2(x_w - jnp.max(x_w, axis=0, keepdims=True))
    a_w = _dot(vwt_ref[:, pl.ds(w0, wlen)], p_w.astype(BF16))
    o_win = a_w[0:HEAD_DIM] / a_w[HEAD_DIM:HEAD_DIM + 1]

    gates = gate_ref[...]
    _store_heads(o_ref, gates[0:1] * o_cmp + gates[1:2] * o_slc + gates[2:3] * o_win)


def _window_masks(t):
    n_var = WINDOW // Q_TILE + 1
    k = np.arange(WINDOW + Q_TILE)[None, :, None]
    q = np.arange(Q_TILE)[None, None, :]
    q0 = (np.arange(n_var) * Q_TILE)[:, None, None]
    dist = q0 + q - (np.maximum(q0 - WINDOW, 0) + k)
    return jnp.asarray(np.where((dist >= 0) & (dist < WINDOW), 0.0, NEG_INF), F32)


def nsa_prompt(proj_bf16, q_blk0, k_aug, kw_blk0, vs_t, vw_t, cmp_k, cmp_vt, gate_lanes, n_sel):
    assert WINDOW % Q_TILE == 0
    wmask = _window_masks(vs_t.shape[2])
    last_var = wmask.shape[0] - 1
    n_groups, _, t = vs_t.shape
    nbp = cmp_k.shape[1]
    cols = REP * Q_TILE
    return pl.pallas_call(
        functools.partial(_nsa_prompt_kernel, n_sel=n_sel),
        grid=(n_groups, t // Q_TILE),
        in_specs=[
            pl.BlockSpec((Q_TILE, REP * HEAD_DIM), lambda g, i: (i, q_blk0 + g)),
            pl.BlockSpec((None, t, HEAD_DIM + nbp), lambda g, i: (g, 0, 0)),
            pl.BlockSpec((None, V_ROWS, t), lambda g, i: (g, 0, 0)),
            pl.BlockSpec((t, HEAD_DIM), lambda g, i: (0, kw_blk0 + g)),
            pl.BlockSpec((None, V_ROWS, t), lambda g, i: (g, 0, 0)),
            pl.BlockSpec((None, nbp, HEAD_DIM), lambda g, i: (g, 0, 0)),
            pl.BlockSpec((None, HEAD_DIM, nbp), lambda g, i: (g, 0, 0)),
            pl.BlockSpec((None, None, 8, cols), lambda g, i: (g, i, 0, 0)),
            pl.BlockSpec((None, WINDOW + Q_TILE, Q_TILE), lambda g, i: (jnp.minimum(i, last_var), 0, 0)),
        ],
        out_specs=pl.BlockSpec((Q_TILE, REP * HEAD_DIM), lambda g, i: (i, g)),
        out_shape=jax.ShapeDtypeStruct((t, n_groups * REP * HEAD_DIM), BF16),
        scratch_shapes=[pltpu.VMEM((V_ROWS, cols), F32)],
        compiler_params=_params(("arbitrary", "arbitrary"), 56),
        name="nsa_prompt",
    )(proj_bf16, k_aug, vs_t, proj_bf16, vw_t, cmp_k, cmp_vt, gate_lanes, wmask)


def _pad_rows(x, n):
    return jnp.concatenate([x, jnp.zeros((n - x.shape[0], x.shape[1]), x.dtype)], axis=0)


def _copy_rows(block_refs, rows_ref):
    views = []
    for p, r in enumerate(block_refs):
        rows_ref[p] = r[...].reshape(rows_ref.shape[1:])
        views.append(rows_ref.at[p])
    return views


def _fox_sample_kernel(pt_ref, *refs, n_pages, n_kv):
    lf_pages = refs[n_pages:2 * n_pages]
    q_ref, new_ref, lf_new_ref, tri_ref, o_ref, kv_ref = refs[2 * n_pages:]
    pages = refs[:n_pages]
    stride = 2 * n_kv
    page = pages[0].shape[0] // stride
    t_new = new_ref.shape[0]
    rows = REP * t_new
    past = n_pages * page
    n_keys = kv_ref.shape[1]
    for c in range(stride):
        for p in range(n_pages):
            kv_ref[c, p * page:(p + 1) * page, :] = pages[p][pl.ds(c, page, stride=stride), :].astype(BF16)
        kv_ref[c, past:n_keys, :] = _pad_rows(new_ref[:, c * HEAD_DIM:(c + 1) * HEAD_DIM], n_keys - past).astype(BF16)
    visible = _iota((t_new, n_keys), 1) <= past + _iota((t_new, n_keys), 0)
    f_chunks = _cumsum_lanes(list(lf_pages) + [lf_new_ref], tri_ref[...])
    f_keys = jnp.concatenate(f_chunks, axis=1)
    f_queries = _pad_rows(f_chunks[-1], LANE).T[0:t_new, :]
    for g in range(n_kv):
        s = _dot_nt(q_ref[g], kv_ref[g]) * ATTN_SCALE
        s = s.reshape(REP, t_new, n_keys) - f_keys[g * REP:(g + 1) * REP, :][:, None, :]
        s = jnp.where(visible[None], s, NEG_INF).reshape(rows, n_keys)
        fq = jnp.concatenate([f_queries[:, g * REP + r:g * REP + r + 1] for r in range(REP)], axis=0)
        prob = _softmax_rows(s + fq).astype(BF16)
        o_ref[g] = _dot(prob, kv_ref[n_kv + g]).astype(o_ref.dtype)


def fox_sample(pt_flat, fox_cache, layer, logf_pool_t, logf_new_t, q_s, new_rows, new_col_block, n_seq, n_pages):
    depth, n_pool, page, _, n_kv, _ = fox_cache.shape
    fox_cache = fox_cache.reshape(depth, n_pool, page * 2 * n_kv, HEAD_DIM)
    t_new = new_rows.shape[0] // n_seq
    rows = REP * t_new
    n_heads = logf_pool_t.shape[1]
    assert page == LANE
    tri = jnp.asarray(np.concatenate([np.triu(np.ones((LANE, LANE), np.float32)),
                                      np.ones((LANE, LANE), np.float32)], axis=1), BF16)
    spec = lambda p: pl.BlockSpec((None, None, page * 2 * n_kv, HEAD_DIM),
                                  lambda b, pt: (layer, pt[b * n_pages + p], 0, 0))
    lf_spec = lambda p: pl.BlockSpec((None, n_heads, page), lambda b, pt: (pt[b * n_pages + p], 0, 0))
    grid_spec = pltpu.PrefetchScalarGridSpec(
        num_scalar_prefetch=1,
        grid=(n_seq,),
        in_specs=[spec(p) for p in range(n_pages)] + [lf_spec(p) for p in range(n_pages)] + [
            pl.BlockSpec((None, n_kv, rows, HEAD_DIM), lambda b, pt: (b, 0, 0, 0)),
            pl.BlockSpec((t_new, 2 * n_kv * HEAD_DIM), lambda b, pt: (b, new_col_block)),
            pl.BlockSpec((None, n_heads, page), lambda b, pt: (b, 0, 0)),
            pl.BlockSpec((LANE, 2 * LANE), lambda b, pt: (0, 0)),
        ],
        out_specs=pl.BlockSpec((None, n_kv, rows, HEAD_DIM), lambda b, pt: (b, 0, 0, 0)),
        scratch_shapes=[pltpu.VMEM((2 * n_kv, (n_pages + 1) * page, HEAD_DIM), BF16)],
    )
    return pl.pallas_call(
        functools.partial(_fox_sample_kernel, n_pages=n_pages, n_kv=n_kv),
        grid_spec=grid_spec,
        out_shape=jax.ShapeDtypeStruct((n_seq, n_kv, rows, HEAD_DIM), BF16),
        compiler_params=_params(("arbitrary",), 52),
        name="fox_sample",
    )(pt_flat, *([fox_cache] * n_pages), *([logf_pool_t] * n_pages), q_s, new_rows, logf_new_t, tri)


def _nsa_sample_kernel(pt_ref, *refs, n_pages, n_groups, n_sel, past_len):
    (q_ref, cmp_ref, new_slc_ref, new_win_ref, win_ref, gate_ref, o_ref, win_out_ref,
     krows_ref, vrows_ref, wrows_ref) = refs[2 * n_pages:]
    page = refs[0].shape[0]
    kpages = _copy_rows(refs[:n_pages], krows_ref)
    vpages = _copy_rows(refs[n_pages:2 * n_pages], vrows_ref)
    t_new = new_slc_ref.shape[0]
    rows = REP * t_new
    gw = n_groups * HEAD_DIM
    nbp = cmp_ref.shape[2]
    n_buf = win_ref.shape[0]
    win_rows = _copy_rows([win_ref], wrows_ref)[0]
    blocks_per_page = page // BLOCK

    qpos = past_len + _iota((t_new, nbp), 0)
    c_mask = (_iota((t_new, nbp), 1) + 1) * BLOCK - 1 <= qpos
    any_vis = (qpos >= BLOCK - 1).astype(F32)
    o_cmp, imps = [], []
    for g in range(n_groups):
        s_c = _dot_nt(q_ref[g], cmp_ref[0, g].astype(BF16)) * ATTN_SCALE
        s_c = jnp.where(c_mask[None], s_c.reshape(REP, t_new, nbp), NEG_INF)
        p_c = _softmax_rows(s_c) * any_vis[None]
        o_cmp.append(_dot(p_c.reshape(rows, nbp).astype(BF16), cmp_ref[1, g].astype(BF16)))
        imp = p_c[0]
        for r in range(1, REP):
            imp = imp + p_c[r]
        imps.append(imp)

    imp_all = _pad_rows(jnp.concatenate(imps, axis=0), LANE)
    blk_t = _iota((nbp, LANE), 0)
    qpos_t = past_len + _iota((nbp, LANE), 1) % t_new
    sel_t = _top_n_mask_t(_selection_scores_t(imp_all.T, blk_t, qpos_t), blk_t, n_sel, nbp)
    sel_all = sel_t.T

    lane_p = _iota((t_new, page), 1)
    row_p = _iota((t_new, page), 0)
    lane_blk = lane_p // BLOCK
    widx = _iota((t_new, n_buf + page), 1)
    wdist = n_buf + _iota((t_new, n_buf + page), 0) - widx
    w_mask = (wdist >= 0) & (wdist < WINDOW) & (widx < n_buf + t_new)
    gates = gate_ref[...]

    for g in range(n_groups):
        q4 = q_ref[g]
        gs = slice(g * HEAD_DIM, (g + 1) * HEAD_DIM)
        vs = slice(gw + g * HEAD_DIM, gw + (g + 1) * HEAD_DIM)
        sel_g = sel_all[g * t_new:(g + 1) * t_new, :]

        chunks = []
        for p in range(n_pages + 1):
            if p < n_pages:
                k_p = kpages[p][pl.ds(g, page, stride=n_groups), :].astype(BF16)
            else:
                k_p = _pad_rows(new_slc_ref[:, gs], page).astype(BF16)
            s = _dot_nt(q4, k_p) * ATTN_SCALE
            picked = jnp.zeros((t_new, page), F32)
            for c in range(blocks_per_page):
                b_idx = p * blocks_per_page + c
                picked = jnp.where(lane_blk == c, sel_g[:, b_idx:b_idx + 1], picked)
            ok = picked > 0.5
            if p == n_pages:
                ok = ok & (lane_p <= row_p)
            chunks.append(jnp.where(ok[None], s.reshape(REP, t_new, page), NEG_INF).reshape(rows, page))
        prob = _softmax_rows(jnp.concatenate(chunks, axis=1)).astype(BF16)
        o_slc = jnp.zeros((rows, HEAD_DIM), F32)
        for p in range(n_pages + 1):
            if p < n_pages:
                v_p = vpages[p][pl.ds(g, page, stride=n_groups), :].astype(BF16)
            else:
                v_p = _pad_rows(new_slc_ref[:, vs], page).astype(BF16)
            o_slc = o_slc + _dot(prob[:, p * page:(p + 1) * page], v_p)

        kw = jnp.concatenate([win_rows[pl.ds(g, n_buf, stride=2 * n_groups), :],
                              _pad_rows(new_win_ref[:, gs], page)], axis=0).astype(BF16)
        vw = jnp.concatenate([win_rows[pl.ds(n_groups + g, n_buf, stride=2 * n_groups), :],
                              _pad_rows(new_win_ref[:, vs], page)], axis=0).astype(BF16)
        s_w = _dot_nt(q4, kw) * ATTN_SCALE
        s_w = jnp.where(w_mask[None], s_w.reshape(REP, t_new, n_buf + page), NEG_INF).reshape(rows, n_buf + page)
        o_win = _dot(_softmax_rows(s_w).astype(BF16), vw)

        gt = gates[g]
        o_ref[g] = (gt[:, 0:1] * o_cmp[g] + gt[:, 1:2] * o_slc + gt[:, 2:3] * o_win).astype(o_ref.dtype)

    win_out_ref[0:n_buf - t_new] = win_ref[t_new:n_buf]
    for c in range(2 * n_groups):
        win_out_ref[n_buf - t_new:n_buf, c // n_groups, c % n_groups, :] = new_win_ref[:, c * HEAD_DIM:(c + 1) * HEAD_DIM]


def nsa_sample(pt_flat, nsa_cache, win_state, layer, q_s, cmp_s, new_rows, slc_col_block, win_col_block, gates_s,
               n_seq, n_pages, n_sel, past_len):
    page, _, n_groups, _ = nsa_cache.shape[2:]
    gw = n_groups * HEAD_DIM
    t_new = new_rows.shape[0] // n_seq
    rows = REP * t_new
    nbp = cmp_s.shape[3]
    n_buf = win_state.shape[2]
    spec = lambda p, slot: pl.BlockSpec((None, None, page, None, n_groups, HEAD_DIM),
                                        lambda b, pt: (layer, pt[b * n_pages + p], 0, slot, 0, 0))
    grid_spec = pltpu.PrefetchScalarGridSpec(
        num_scalar_prefetch=1,
        grid=(n_seq,),
        in_specs=[spec(p, 2) for p in range(n_pages)] + [spec(p, 3) for p in range(n_pages)] + [
            pl.BlockSpec((None, n_groups, rows, HEAD_DIM), lambda b, pt: (b, 0, 0, 0)),
            pl.BlockSpec((None, 2, n_groups, nbp, HEAD_DIM), lambda b, pt: (b, 0, 0, 0, 0)),
            pl.BlockSpec((t_new, 2 * gw), lambda b, pt: (b, slc_col_block)),
            pl.BlockSpec((t_new, 2 * gw), lambda b, pt: (b, win_col_block)),
            pl.BlockSpec((None, None, n_buf, 2, n_groups, HEAD_DIM), lambda b, pt: (layer, b, 0, 0, 0, 0)),
            pl.BlockSpec((None, n_groups, rows, 16), lambda b, pt: (b, 0, 0, 0)),
        ],
        out_specs=[pl.BlockSpec((None, n_groups, rows, HEAD_DIM), lambda b, pt: (b, 0, 0, 0)),
                   pl.BlockSpec((None, n_buf, 2, n_groups, HEAD_DIM), lambda b, pt: (b, 0, 0, 0, 0))],
        scratch_shapes=[pltpu.VMEM((n_pages, page * n_groups, HEAD_DIM), F32),
                        pltpu.VMEM((n_pages, page * n_groups, HEAD_DIM), F32),
                        pltpu.VMEM((1, n_buf * 2 * n_groups, HEAD_DIM), F32)],
    )
    assert n_buf > t_new
    return pl.pallas_call(
        functools.partial(_nsa_sample_kernel, n_pages=n_pages, n_groups=n_groups, n_sel=n_sel, past_len=past_len),
        grid_spec=grid_spec,
        out_shape=[jax.ShapeDtypeStruct((n_seq, n_groups, rows, HEAD_DIM), BF16),
                   jax.ShapeDtypeStruct(win_state.shape[1:], F32)],
        compiler_params=_params(("arbitrary",), 56),
        name="nsa_sample",
    )(pt_flat, *([nsa_cache] * (2 * n_pages)), q_s, cmp_s, new_rows, new_rows, win_state, gates_s)


def _rope_tables(pos):
    half = HEAD_DIM // 2
    inv_freq = ROPE_THETA ** (-jnp.arange(half, dtype=F32) / half)
    ang = pos.astype(F32)[:, None] * inv_freq[None, :]
    cos, sin = jnp.cos(ang), jnp.sin(ang)
    return jnp.concatenate([cos, cos], axis=-1), jnp.concatenate([-sin, sin], axis=-1)


def _heads_major(x, n_seq, t_new, n_groups):
    w = x.shape[1] // (n_groups * REP)
    return x.reshape(n_seq, t_new, n_groups, REP, w).transpose(0, 2, 3, 1, 4).reshape(n_seq, n_groups, REP * t_new, w)


def _tokens_major(x, n_seq, t_new, n_groups):
    w = x.shape[-1]
    return x.reshape(n_seq, n_groups, REP, t_new, w).transpose(0, 3, 1, 2, 4).reshape(n_seq * t_new, n_groups * REP * w)


def _values_t(v, n_groups):
    t = v.shape[0]
    vt = v.T.reshape(n_groups, HEAD_DIM, t)
    return jnp.concatenate([vt, jnp.ones((n_groups, V_ROWS - HEAD_DIM, t), v.dtype)], axis=1)


def _head_lanes(x, n_groups, w):
    t = x.shape[0]
    nq = t // Q_TILE
    x = x.reshape(nq, Q_TILE, n_groups, REP, w).transpose(2, 0, 4, 3, 1)
    return x.reshape(n_groups, nq, w, REP * Q_TILE)


def kernel(x_prompt, x_sample, cache_fox_kv, cache_fox_logf, cache_nsa_kv, state_nsa_win, page_table, p_prompt, p_sample, g_mix, w_in, b_fgate, cmp_pe_k, cmp_w1_k, cmp_w2_k, cmp_pe_v, cmp_w1_v, cmp_w2_v, w_out, g_ffn, w_gate, w_up, w_down, g_ple, w_ple_gate, w_ple_proj, g_final):
    depth = w_in.shape[0]
    _, seq, d_model = x_prompt.shape
    n_seq, t_new, _ = x_sample.shape
    page = cache_fox_kv.shape[2]
    n_pages = page_table.shape[1]
    past_len = n_pages * page
    fox_heads = b_fgate.shape[1]
    n_kv = cache_fox_kv.shape[4]
    n_groups = cache_nsa_kv.shape[4]
    nsa_heads = n_groups * REP
    assert fox_heads == n_kv * REP and w_out.shape[1] == (fox_heads + nsa_heads) * HEAD_DIM
    fq_w, nq_w = fox_heads * HEAD_DIM, nsa_heads * HEAD_DIM
    fkv_w, nkv_w = n_kv * HEAD_DIM, n_groups * HEAD_DIM
    n_gate = 3 * nsa_heads
    assert fox_heads + n_gate <= LANE and seq % K_TILE == 0 and seq >= WINDOW + Q_TILE and 3 * REP <= HEAD_DIM

    sizes = [fq_w, fkv_w, fkv_w, fox_heads, nq_w] + [nkv_w] * 6 + [n_gate]
    off = np.concatenate([[0], np.cumsum(sizes)]).astype(int)
    o_fq, o_fk, o_fv, o_fl, o_nq, o_kc, o_vc, o_ks, o_vs, o_kw, o_vw, o_g = [int(v) for v in off[:-1]]
    c_fq, c_nq = 0, fq_w
    c_fk = c_nq + nq_w
    c_fv = c_fk + fkv_w
    c_kc = c_fv + fkv_w
    c_vc, c_ks, c_vs, c_kw, c_vw = (c_kc + nkv_w * k for k in range(1, 6))
    n_main = c_vw + nkv_w
    tn = min(512, fkv_w, nkv_w)
    rope_ranges = ((c_nq, c_nq + nq_w), (c_kc, c_kc + nkv_w), (c_ks, c_ks + nkv_w), (c_kw, c_kw + nkv_w))

    nb_p = seq // BLOCK
    nbp_p = max(LANE, nb_p)
    nb_s = -(-(past_len + t_new) // BLOCK)
    nbp_s = LANE
    assert nb_s == 2 * n_pages + 1 and nb_s <= LANE and nbp_p % LANE == 0
    n_win_p = min(WINDOW, seq)
    nq_tiles = seq // Q_TILE

    cos_p, sin_p = _rope_tables(jnp.arange(seq))
    cos_s, sin_s = _rope_tables(jnp.tile(past_len + jnp.arange(t_new), n_seq))
    expand_t = jnp.asarray((np.arange(seq)[:, None] // BLOCK) == np.arange(nbp_p)[None, :], BF16)
    aug_np = np.zeros((HEAD_DIM, REP * Q_TILE), np.float32)
    for r in range(REP):
        aug_np[3 * r:3 * r + 3, r * Q_TILE:(r + 1) * Q_TILE] = 1.0
    aug = jnp.asarray(aug_np, BF16)
    pt_flat = page_table.reshape(-1).astype(jnp.int32)

    hp = x_prompt.reshape(seq, d_model)
    hs = x_sample.reshape(n_seq * t_new, d_model)
    outs_p, outs_s = [], []
    for i in range(depth):
        wi = w_in[i]
        w_main = jnp.concatenate(
            [wi[:, o_fq:o_fq + fq_w], wi[:, o_nq:o_nq + nq_w], wi[:, o_fk:o_fl], wi[:, o_kc:o_g]], axis=1).astype(BF16)
        w_small = jnp.concatenate(
            [wi[:, o_fl:o_fl + fox_heads], wi[:, o_g:o_g + n_gate],
             jnp.zeros((d_model, LANE - fox_heads - n_gate), F32)], axis=1).astype(BF16)
        b_small = jnp.concatenate([b_fgate[i], jnp.zeros((LANE - fox_heads,), F32)]).reshape(1, LANE)
        w_o = w_out[i].astype(BF16)
        w_oa, w_ob = w_o[:fq_w], w_o[fq_w:]
        w_g, w_u, w_d = w_gate[i].astype(BF16), w_up[i].astype(BF16), w_down[i].astype(BF16)
        w_pg, w_pp = w_ple_gate[i].astype(BF16), w_ple_proj[i].astype(BF16)
        pe = jnp.stack([cmp_pe_k[i], cmp_pe_v[i]])
        hidden = cmp_w1_k.shape[-1]
        w1 = jnp.stack([cmp_w1_k[i], cmp_w1_v[i]]).reshape(2, BLOCK, HEAD_DIM, hidden).astype(BF16)
        w1_pairs = w1.reshape(2, BLOCK // 2, 2 * HEAD_DIM, hidden)
        w2 = jnp.stack([cmp_w2_k[i], cmp_w2_v[i]]).astype(BF16)

        def tail(h, o_fox, o_nsa, p):
            h1 = outproj(o_fox, o_nsa, w_oa, w_ob, h)
            act = gateup(rmsnorm(h1, g_ffn[i], BF16), w_g, w_u)
            h2 = downproj(act, w_d, h1)
            return ple(rmsnorm(h2, g_ple[i], BF16), w_pg, p.astype(BF16), w_pp, h2)

        xn = rmsnorm(hp, g_mix[i], BF16)
        pf, pb = inproj(xn, w_main, cos_p, sin_p, tn, rope_ranges)
        small = smallproj(xn, w_small, b_small, fox_heads)
        logf = small[:, :fox_heads]
        f_cum, fp1, fp2, fp3 = cumsum_rows(logf)
        pieces = jnp.stack([fp1, fp2, fp3], axis=-1)
        pieces = pieces.reshape(seq, n_kv, REP * 3).transpose(1, 0, 2)
        k_fox = pb[:, c_fk:c_fk + fkv_w].reshape(seq, n_kv, HEAD_DIM).transpose(1, 0, 2)
        k_aug = jnp.concatenate([k_fox, pieces, jnp.zeros((n_kv, seq, HEAD_DIM - REP * 3), BF16)], axis=-1)
        first_tiles = fox_first_tiles(pb[:, c_fq:c_fq + fq_w], pb[:, c_fk:c_fk + fkv_w], f_cum, n_kv)
        o_fox = fox_prompt(first_tiles, pb, c_fq // (REP * HEAD_DIM), aug, k_aug,
                           _values_t(pb[:, c_fv:c_fv + fkv_w], n_kv), _head_lanes(f_cum, n_kv, 1))
        cmp_p = compress_prompt(pf, (c_kc // HEAD_DIM, c_vc // HEAD_DIM), pe, w1, w2, n_groups, nbp_p)
        cmp_vt = cmp_p[1].transpose(0, 2, 1).astype(BF16)
        gate_lanes = _head_lanes(small[:, fox_heads:fox_heads + n_gate], n_groups, 3)
        gate_lanes = jnp.pad(gate_lanes, ((0, 0), (0, 0), (0, 5), (0, 0)))
        k_slc = pb[:, c_ks:c_ks + nkv_w].reshape(seq, n_groups, HEAD_DIM).transpose(1, 0, 2)
        k_aug_nsa = jnp.concatenate([k_slc, jnp.broadcast_to(expand_t, (n_groups,) + expand_t.shape)], axis=-1)
        o_nsa = nsa_prompt(pb, c_nq // (REP * HEAD_DIM), k_aug_nsa, c_kw // HEAD_DIM,
                           _values_t(pb[:, c_vs:c_vs + nkv_w], n_groups), _values_t(pb[:, c_vw:c_vw + nkv_w], n_groups),
                           cmp_p[0], cmp_vt, gate_lanes, min(SEL_TOP_N, nb_p))
        hp = tail(hp, o_fox, o_nsa, p_prompt[i].reshape(seq, -1))
        outs_p.append((
            pf[:, c_fk:c_fk + 2 * fkv_w].reshape(1, seq, 2, n_kv, HEAD_DIM),
            logf.reshape(1, seq, fox_heads),
            pf[:, c_kc:c_kc + 4 * nkv_w].reshape(1, seq, 4, n_groups, HEAD_DIM),
            pf[seq - n_win_p:, c_kw:c_kw + 2 * nkv_w].reshape(1, n_win_p, 2, n_groups, HEAD_DIM),
        ))

        m_s = n_seq * t_new
        xs = rmsnorm(hs, g_mix[i], BF16)
        sf, sb = inproj(xs, w_main, cos_s, sin_s, tn, rope_ranges)
        small_s = smallproj(xs, w_small, b_small, fox_heads)
        logf_s = small_s[:, :fox_heads]
        logf_pool_t = cache_fox_logf[i].transpose(0, 2, 1)
        new_t = jnp.pad(logf_s.reshape(n_seq, t_new, fox_heads).transpose(0, 2, 1),
                        ((0, 0), (0, 0), (0, LANE - t_new)))
        assert c_fk % (2 * fkv_w) == 0 and c_kc % nkv_w == 0 and c_ks % (2 * nkv_w) == 0 and c_kw % (2 * nkv_w) == 0
        q_fox_s = _heads_major(sb[:, c_fq:c_fq + fq_w], n_seq, t_new, n_kv)
        o_fox_s = fox_sample(pt_flat, cache_fox_kv, i, logf_pool_t, new_t, q_fox_s, sf, c_fk // (2 * fkv_w),
                             n_seq, n_pages)
        cmp_raw = compress_sample(pt_flat, cache_nsa_kv, i, sf, c_kc // nkv_w, pe, w1_pairs, w2, n_seq, n_pages, nb_s)
        cmp_s = cmp_raw[:, :, :nb_s * n_groups].reshape(n_seq, 2, nb_s, n_groups, HEAD_DIM).transpose(0, 1, 3, 2, 4)
        cmp_s = jnp.pad(cmp_s, ((0, 0), (0, 0), (0, 0), (0, nbp_s - nb_s), (0, 0)))
        q_nsa_s = _heads_major(sb[:, c_nq:c_nq + nq_w], n_seq, t_new, n_groups)
        gates_s = _heads_major(small_s[:, fox_heads:fox_heads + n_gate], n_seq, t_new, n_groups)
        gates_s = jnp.pad(gates_s, ((0, 0), (0, 0), (0, 0), (0, 13)))
        o_nsa_s, new_win = nsa_sample(pt_flat, cache_nsa_kv, state_nsa_win, i, q_nsa_s, cmp_s, sf,
                                      c_ks // (2 * nkv_w), c_kw // (2 * nkv_w), gates_s,
                                      n_seq, n_pages, min(SEL_TOP_N, nb_s), past_len)
        hs = tail(hs, _tokens_major(o_fox_s, n_seq, t_new, n_kv), _tokens_major(o_nsa_s, n_seq, t_new, n_groups),
                  p_sample[i].reshape(m_s, -1))
        outs_s.append((
            sf[:, c_fk:c_fk + 2 * fkv_w].reshape(n_seq, t_new, 2, n_kv, HEAD_DIM),
            logf_s.reshape(n_seq, t_new, fox_heads),
            sf[:, c_kc:c_kc + 4 * nkv_w].reshape(n_seq, t_new, 4, n_groups, HEAD_DIM),
            new_win,
        ))

    y_prompt = rmsnorm(hp, g_final, F32).reshape(x_prompt.shape)
    y_sample = rmsnorm(hs, g_final, F32).reshape(x_sample.shape)
    stack = lambda outs, j: jnp.stack([r[j] for r in outs], axis=0)
    return (y_prompt, y_sample, stack(outs_p, 0), stack(outs_p, 1), stack(outs_p, 2), stack(outs_p, 3),
            stack(outs_s, 0), stack(outs_s, 1), stack(outs_s, 2), stack(outs_s, 3))
```

```python
import functools

import numpy as np
import jax
import jax.numpy as jnp
from jax import lax
from jax.experimental import pallas as pl
from jax.experimental.pallas import tpu as pltpu

HEAD_DIM = 128
REP = 4
BLOCK = 64
SEL_TOP_N = 16
WINDOW = 512
Q_TILE = 512
K_TILE = 512
V_ROWS = 144
ROPE_THETA = 10000.0
RMS_EPS = 1e-6
ATTN_SCALE = HEAD_DIM ** -0.5
LOG2E = 1.4426950408889634
SCALE2 = ATTN_SCALE * LOG2E
NEG_INF = -1e30
SKIP_MARGIN = 40.0
REMOVED = -3e38
FORCE_BONUS = 1e4
LANE = 128
MIB = 1024 * 1024

F32 = jnp.float32
BF16 = jnp.bfloat16


def _params(sem, vmem_mib):
    return pltpu.CompilerParams(dimension_semantics=sem, vmem_limit_bytes=vmem_mib * MIB)


def _dot(a, b):
    return jnp.dot(a, b, preferred_element_type=F32)


def _dot_nt(a, b):
    return lax.dot_general(a, b, (((1,), (1,)), ((), ())), preferred_element_type=F32)


def _row_tile(m, cap):
    t = min(m, cap)
    assert m % t == 0
    return t


def _iota(shape, dim):
    return lax.broadcasted_iota(jnp.int32, shape, dim)


def _rmsnorm_kernel(x_ref, g_ref, o_ref):
    x = x_ref[...]
    y = x * lax.rsqrt(jnp.mean(x * x, axis=-1, keepdims=True) + RMS_EPS)
    o_ref[...] = (y * g_ref[...]).astype(o_ref.dtype)


def rmsnorm(x, g, out_dtype):
    m, d = x.shape
    tm = _row_tile(m, 256)
    return pl.pallas_call(
        _rmsnorm_kernel,
        grid=(m // tm,),
        in_specs=[pl.BlockSpec((tm, d), lambda i: (i, 0)), pl.BlockSpec((1, d), lambda i: (0, 0))],
        out_specs=pl.BlockSpec((tm, d), lambda i: (i, 0)),
        out_shape=jax.ShapeDtypeStruct((m, d), out_dtype),
        compiler_params=_params(("parallel",), 32),
        name="rmsnorm",
    )(x, g.reshape(1, d))


def _inproj_kernel(x_ref, w_ref, cos_ref, sin_ref, of_ref, ob_ref, *, tn, rope_ranges):
    acc = _dot(x_ref[...], w_ref[...])
    col0 = pl.program_id(1) * tn
    is_rope = None
    for lo, hi in rope_ranges:
        hit = (col0 >= lo) & (col0 < hi)
        is_rope = hit if is_rope is None else (is_rope | hit)

    @pl.when(is_rope)
    def _():
        cos = cos_ref[...]
        sin = sin_ref[...]
        for h in range(tn // HEAD_DIM):
            sl = slice(h * HEAD_DIM, (h + 1) * HEAD_DIM)
            xh = acc[:, sl]
            r = xh * cos + pltpu.roll(xh, HEAD_DIM // 2, 1) * sin
            of_ref[:, sl] = r
            ob_ref[:, sl] = r.astype(BF16)

    @pl.when(jnp.logical_not(is_rope))
    def _():
        of_ref[...] = acc
        ob_ref[...] = acc.astype(BF16)


def inproj(xn, w_main, cos, sin, tn, rope_ranges):
    m, k = xn.shape
    n = w_main.shape[1]
    tm = _row_tile(m, 1024)
    return pl.pallas_call(
        functools.partial(_inproj_kernel, tn=tn, rope_ranges=rope_ranges),
        grid=(m // tm, n // tn),
        in_specs=[
            pl.BlockSpec((tm, k), lambda i, j: (i, 0)),
            pl.BlockSpec((k, tn), lambda i, j: (0, j)),
            pl.BlockSpec((tm, HEAD_DIM), lambda i, j: (i, 0)),
            pl.BlockSpec((tm, HEAD_DIM), lambda i, j: (i, 0)),
        ],
        out_specs=[pl.BlockSpec((tm, tn), lambda i, j: (i, j)), pl.BlockSpec((tm, tn), lambda i, j: (i, j))],
        out_shape=[jax.ShapeDtypeStruct((m, n), F32), jax.ShapeDtypeStruct((m, n), BF16)],
        compiler_params=_params(("parallel", "arbitrary"), 48),
        name="inproj",
    )(xn, w_main, cos, sin)


def _smallproj_kernel(x_ref, w_ref, b_ref, o_ref, *, n_logf):
    v = _dot(x_ref[...], w_ref[...])
    z = v + b_ref[...]
    logf = -(jnp.maximum(-z, 0.0) + jnp.log1p(jnp.exp(-jnp.abs(z))))
    gate = jax.nn.sigmoid(v)
    lane = _iota(v.shape, 1)
    o_ref[...] = jnp.where(lane < n_logf, logf, gate)


def smallproj(xn, w_small, b_small, n_logf):
    m, k = xn.shape
    tm = _row_tile(m, 1024)
    return pl.pallas_call(
        functools.partial(_smallproj_kernel, n_logf=n_logf),
        grid=(m // tm,),
        in_specs=[
            pl.BlockSpec((tm, k), lambda i: (i, 0)),
            pl.BlockSpec((k, LANE), lambda i: (0, 0)),
            pl.BlockSpec((1, LANE), lambda i: (0, 0)),
        ],
        out_specs=pl.BlockSpec((tm, LANE), lambda i: (i, 0)),
        out_shape=jax.ShapeDtypeStruct((m, LANE), F32),
        compiler_params=_params(("parallel",), 32),
        name="smallproj",
    )(xn, w_small, b_small)


def _outproj_kernel(a_ref, b_ref, wa_ref, wb_ref, h_ref, o_ref):
    o_ref[...] = h_ref[...] + (_dot(a_ref[...], wa_ref[...]) + _dot(b_ref[...], wb_ref[...]))


def outproj(o_fox, o_nsa, w_a, w_b, h):
    m, ka = o_fox.shape
    kb = o_nsa.shape[1]
    n = w_a.shape[1]
    tm, tn = _row_tile(m, 1024), 512
    return pl.pallas_call(
        _outproj_kernel,
        grid=(m // tm, n // tn),
        in_specs=[
            pl.BlockSpec((tm, ka), lambda i, j: (i, 0)),
            pl.BlockSpec((tm, kb), lambda i, j: (i, 0)),
            pl.BlockSpec((ka, tn), lambda i, j: (0, j)),
            pl.BlockSpec((kb, tn), lambda i, j: (0, j)),
            pl.BlockSpec((tm, tn), lambda i, j: (i, j)),
        ],
        out_specs=pl.BlockSpec((tm, tn), lambda i, j: (i, j)),
        out_shape=jax.ShapeDtypeStruct((m, n), F32),
        compiler_params=_params(("parallel", "arbitrary"), 48),
        name="outproj",
    )(o_fox, o_nsa, w_a, w_b, h)


def _gateup_kernel(x_ref, wg_ref, wu_ref, o_ref):
    x = x_ref[...]
    g = _dot(x, wg_ref[...])
    u = _dot(x, wu_ref[...])
    o_ref[...] = (g * jax.nn.sigmoid(g) * u).astype(o_ref.dtype)


def gateup(xn, w_gate, w_up):
    m, k = xn.shape
    n = w_gate.shape[1]
    tm, tn = _row_tile(m, 2048), 256
    return pl.pallas_call(
        _gateup_kernel,
        grid=(m // tm, n // tn),
        in_specs=[
            pl.BlockSpec((tm, k), lambda i, j: (i, 0)),
            pl.BlockSpec((k, tn), lambda i, j: (0, j)),
            pl.BlockSpec((k, tn), lambda i, j: (0, j)),
        ],
        out_specs=pl.BlockSpec((tm, tn), lambda i, j: (i, j)),
        out_shape=jax.ShapeDtypeStruct((m, n), BF16),
        compiler_params=_params(("parallel", "arbitrary"), 56),
        name="gateup",
    )(xn, w_gate, w_up)


def _down_kernel(x_ref, w_ref, h_ref, o_ref):
    o_ref[...] = h_ref[...] + _dot(x_ref[...], w_ref[...])


def downproj(act, w_down, h):
    m, k = act.shape
    n = w_down.shape[1]
    tm, tn = _row_tile(m, 512), 512
    return pl.pallas_call(
        _down_kernel,
        grid=(m // tm, n // tn),
        in_specs=[
            pl.BlockSpec((tm, k), lambda i, j: (i, 0)),
            pl.BlockSpec((k, tn), lambda i, j: (0, j)),
            pl.BlockSpec((tm, tn), lambda i, j: (i, j)),
        ],
        out_specs=pl.BlockSpec((tm, tn), lambda i, j: (i, j)),
        out_shape=jax.ShapeDtypeStruct((m, n), F32),
        compiler_params=_params(("parallel", "arbitrary"), 58),
        name="downproj",
    )(act, w_down, h)


def _ple_kernel(x_ref, wg_ref, p_ref, wp_ref, h_ref, o_ref):
    gate = jax.nn.sigmoid(_dot(x_ref[...], wg_ref[...]))
    o_ref[...] = h_ref[...] + gate * _dot(p_ref[...], wp_ref[...])


def ple(xn, w_gate, p, w_proj, h):
    m, k = xn.shape
    kp = p.shape[1]
    n = w_gate.shape[1]
    tm, tn = _row_tile(m, 1024), 512
    return pl.pallas_call(
        _ple_kernel,
        grid=(m // tm, n // tn),
        in_specs=[
            pl.BlockSpec((tm, k), lambda i, j: (i, 0)),
            pl.BlockSpec((k, tn), lambda i, j: (0, j)),
            pl.BlockSpec((tm, kp), lambda i, j: (i, 0)),
            pl.BlockSpec((kp, tn), lambda i, j: (0, j)),
            pl.BlockSpec((tm, tn), lambda i, j: (i, j)),
        ],
        out_specs=pl.BlockSpec((tm, tn), lambda i, j: (i, j)),
        out_shape=jax.ShapeDtypeStruct((m, n), F32),
        compiler_params=_params(("parallel", "arbitrary"), 48),
        name="ple",
    )(xn, w_gate, p, w_proj, h)


def _split3(x):
    x1 = x.astype(BF16)
    r1 = x - x1.astype(F32)
    x2 = r1.astype(BF16)
    x3 = (r1 - x2.astype(F32)).astype(BF16)
    return x1, x2, x3


def _cumsum_rows_kernel(x_ref, tri_ref, o_ref, p1_ref, p2_ref, p3_ref, carry_ref):
    @pl.when(pl.program_id(0) == 0)
    def _():
        carry_ref[...] = jnp.zeros_like(carry_ref)

    x1, x2, x3 = _split3(x_ref[...])
    tri = tri_ref[...]
    out = (_dot(tri, x1) + _dot(tri, x2) + _dot(tri, x3)) + carry_ref[...]
    o_ref[...] = out
    carry_ref[...] = out[out.shape[0] - 1:, :]
    p1_ref[...], p2_ref[...], p3_ref[...] = _split3(out * (-1.0 / ATTN_SCALE))


def cumsum_rows(x):
    t, h = x.shape
    c = _row_tile(t, 512)
    tri = jnp.asarray(np.tril(np.ones((c, c), np.float32)), BF16)
    row_spec = pl.BlockSpec((c, h), lambda i: (i, 0))
    return pl.pallas_call(
        _cumsum_rows_kernel,
        grid=(t // c,),
        in_specs=[row_spec, pl.BlockSpec((c, c), lambda i: (0, 0))],
        out_specs=[row_spec] * 4,
        out_shape=[jax.ShapeDtypeStruct((t, h), F32)] + [jax.ShapeDtypeStruct((t, h), BF16)] * 3,
        scratch_shapes=[pltpu.VMEM((1, h), F32)],
        compiler_params=_params(("arbitrary",), 32),
        name="cumsum_rows",
    )(x, tri)


def _cumsum_lanes(chunk_refs, tri_ones):
    h = chunk_refs[0].shape[0]
    parts = []
    for ref in chunk_refs:
        y = _dot(jnp.concatenate(_split3(ref[...]), axis=0), tri_ones)
        parts.append(y[0:h] + y[h:2 * h] + y[2 * h:3 * h])
    carry = jnp.zeros((h, LANE), F32)
    outs = []
    for part in parts:
        outs.append(part[:, :LANE] + carry)
        carry = carry + part[:, LANE:]
    return outs


def _silu(x):
    return x * jax.nn.sigmoid(x)


def _compress_prompt_kernel(cb_ref, x_ref, pe_ref, w1_ref, w2_ref, o_ref, *, nb):
    hidden = w1_ref.shape[2]

    def body(l, acc):
        xl = x_ref[pl.ds(l, nb, stride=BLOCK), :] + pe_ref[pl.ds(l, 1), :]
        return acc + _dot(xl.astype(BF16), w1_ref[l])

    acc = lax.fori_loop(0, BLOCK, body, jnp.zeros((nb, hidden), F32))
    out = _dot(_silu(acc).astype(BF16), w2_ref[...])
    o_ref[...] = jnp.zeros_like(o_ref)
    o_ref[0:nb, :] = out


def compress_prompt(proj_f32, col_blocks, pe, w1, w2, n_groups, nbp):
    t = proj_f32.shape[0]
    nb = t // BLOCK
    hidden = w1.shape[-1]
    cb = jnp.asarray(col_blocks, jnp.int32)
    grid_spec = pltpu.PrefetchScalarGridSpec(
        num_scalar_prefetch=1,
        grid=(2, n_groups),
        in_specs=[
            pl.BlockSpec((t, HEAD_DIM), lambda kv, g, cb: (0, cb[kv] + g)),
            pl.BlockSpec((None, BLOCK, HEAD_DIM), lambda kv, g, cb: (kv, 0, 0)),
            pl.BlockSpec((None, BLOCK, HEAD_DIM, hidden), lambda kv, g, cb: (kv, 0, 0, 0)),
            pl.BlockSpec((None, hidden, HEAD_DIM), lambda kv, g, cb: (kv, 0, 0)),
        ],
        out_specs=pl.BlockSpec((None, None, nbp, HEAD_DIM), lambda kv, g, cb: (kv, g, 0, 0)),
    )
    return pl.pallas_call(
        functools.partial(_compress_prompt_kernel, nb=nb),
        grid_spec=grid_spec,
        out_shape=jax.ShapeDtypeStruct((2, n_groups, nbp, HEAD_DIM), F32),
        compiler_params=_params(("arbitrary", "arbitrary"), 40),
        name="compress_prompt",
    )(cb, proj_f32, pe, w1, w2)


def _compress_sample_kernel(pt_ref, *refs, n_pages, n_groups, nb):
    pages = refs[:n_pages]
    new_ref, pe_ref, w1_ref, w2_ref, o_ref, newblk_ref = refs[n_pages:]
    t_new = new_ref.shape[0]
    hidden = w1_ref.shape[2]
    rows_pad = o_ref.shape[0]
    n_rows = (2 * n_pages + 1) * n_groups

    newblk_ref[...] = jnp.zeros_like(newblk_ref)
    for l in range(t_new):
        for g in range(n_groups):
            newblk_ref[l, g:g + 1, :] = new_ref[l:l + 1, g * HEAD_DIM:(g + 1) * HEAD_DIM]

    def gather(l):
        rr = []
        for pg in pages:
            rr.append(pg[l])
            rr.append(pg[l + BLOCK])
        rr.append(newblk_ref[l])
        if rows_pad > n_rows:
            rr.append(jnp.zeros((rows_pad - n_rows, HEAD_DIM), F32))
        return jnp.concatenate(rr, axis=0) + pe_ref[l:l + 1, :]

    acc = jnp.zeros((rows_pad, hidden), F32)
    for l2 in range(BLOCK // 2):
        x = jnp.concatenate([gather(2 * l2), gather(2 * l2 + 1)], axis=1).astype(BF16)
        acc = acc + _dot(x, w1_ref[l2])
    out = _dot(_silu(acc).astype(BF16), w2_ref[...])
    o_ref[...] = jnp.where(_iota(out.shape, 0) < nb * n_groups, out, 0.0)


def compress_sample(pt_flat, nsa_cache, layer, new_rows, new_col_block, pe, w1_pairs, w2, n_seq, n_pages, nb):
    page, _, n_groups, _ = nsa_cache.shape[2:]
    assert page == 2 * BLOCK
    gw = n_groups * HEAD_DIM
    hidden = w1_pairs.shape[-1]
    t_new = new_rows.shape[0] // n_seq
    rows_pad = -(-((2 * n_pages + 1) * n_groups) // 8) * 8
    spec = lambda p: pl.BlockSpec((None, None, page, None, n_groups, HEAD_DIM),
                                  lambda kv, b, pt: (layer, pt[b * n_pages + p], 0, kv, 0, 0))
    grid_spec = pltpu.PrefetchScalarGridSpec(
        num_scalar_prefetch=1,
        grid=(2, n_seq),
        in_specs=[spec(p) for p in range(n_pages)] + [
            pl.BlockSpec((t_new, gw), lambda kv, b, pt: (b, new_col_block + kv)),
            pl.BlockSpec((None, BLOCK, HEAD_DIM), lambda kv, b, pt: (kv, 0, 0)),
            pl.BlockSpec((None, BLOCK // 2, 2 * HEAD_DIM, hidden), lambda kv, b, pt: (kv, 0, 0, 0)),
            pl.BlockSpec((None, hidden, HEAD_DIM), lambda kv, b, pt: (kv, 0, 0)),
        ],
        out_specs=pl.BlockSpec((None, None, rows_pad, HEAD_DIM), lambda kv, b, pt: (b, kv, 0, 0)),
        scratch_shapes=[pltpu.VMEM((BLOCK, n_groups, HEAD_DIM), F32)],
    )
    return pl.pallas_call(
        functools.partial(_compress_sample_kernel, n_pages=n_pages, n_groups=n_groups, nb=nb),
        grid_spec=grid_spec,
        out_shape=jax.ShapeDtypeStruct((n_seq, 2, rows_pad, HEAD_DIM), F32),
        compiler_params=_params(("arbitrary", "arbitrary"), 40),
        name="compress_sample",
    )(pt_flat, *([nsa_cache] * n_pages), new_rows, pe, w1_pairs, w2)


def _softmax_rows(s):
    m = jnp.max(s, axis=-1, keepdims=True)
    e = jnp.exp(s - m)
    return e / jnp.sum(e, axis=-1, keepdims=True)


def _softmax_cols(s):
    m = jnp.max(s, axis=0, keepdims=True)
    e = jnp.exp(s - m)
    return e / jnp.sum(e, axis=0, keepdims=True)


def _top_n_mask_t(score_t, blk_t, n_sel, n_blocks_pad):
    for _ in range(n_sel):
        mx = jnp.max(score_t, axis=0, keepdims=True)
        idx = jnp.min(jnp.where(score_t == mx, blk_t, n_blocks_pad), axis=0, keepdims=True)
        score_t = jnp.where(blk_t == idx, REMOVED, score_t)
    return (score_t == REMOVED).astype(F32)


def _selection_scores_t(imp_t, blk_t, qpos_t):
    cur = qpos_t // BLOCK
    forced = (blk_t == 0) | (blk_t == cur) | (blk_t == cur - 1)
    avail = blk_t * BLOCK <= qpos_t
    return jnp.where(avail, imp_t + jnp.where(forced, FORCE_BONUS, 0.0), NEG_INF)


def _online_tile_t(x, vt_tile, m, acc_ref, shift):
    mx = jnp.max(x, axis=0, keepdims=True) * SCALE2
    if shift is not None:
        mx = mx + shift
    m_new = jnp.maximum(m, mx)
    off = m_new if shift is None else m_new - shift
    p = jnp.exp2(x * SCALE2 - off)
    acc_ref[...] = jnp.exp2(m - m_new) * acc_ref[...] + _dot(vt_tile, p.astype(BF16))
    return m_new


def _queries_t(q):
    return jnp.concatenate([q[:, r * HEAD_DIM:(r + 1) * HEAD_DIM].T for r in range(REP)], axis=1)


def _store_heads(o_ref, o_t):
    for r in range(REP):
        o_ref[:, r * HEAD_DIM:(r + 1) * HEAD_DIM] = o_t[:, r * Q_TILE:(r + 1) * Q_TILE].T.astype(o_ref.dtype)


def _fox_prompt_kernel(first_ref, q_ref, aug_ref, ka_ref, vt_ref, fq_ref, o_ref, acc_ref):
    q0 = pl.program_id(1) * Q_TILE
    cols = REP * Q_TILE
    qa = jnp.concatenate([_queries_t(q_ref[...]), aug_ref[...]], axis=0)
    fq2 = fq_ref[...] * LOG2E
    acc_ref[...] = jnp.zeros_like(acc_ref)
    n_full = q0 // K_TILE

    def scores(j):
        k0 = pl.multiple_of(j * K_TILE, K_TILE)
        return _dot(ka_ref[pl.ds(k0, K_TILE), :], qa)

    def update(j, x, m):
        k0 = pl.multiple_of(j * K_TILE, K_TILE)
        return _online_tile_t(x, vt_ref[:, pl.ds(k0, K_TILE)], m, acc_ref, fq2)

    first = first_ref[pl.program_id(0) * pl.num_programs(1) + pl.program_id(1)]
    m = lax.fori_loop(first, n_full, lambda j, m: update(j, scores(j), m), jnp.full((1, cols), NEG_INF, F32))
    kpos = n_full * K_TILE + _iota((K_TILE, cols), 0)
    qpos = q0 + (_iota((K_TILE, cols), 1) & (Q_TILE - 1))
    update(n_full, jnp.where(kpos <= qpos, scores(n_full), NEG_INF), m)
    acc = acc_ref[...]
    _store_heads(o_ref, acc[0:HEAD_DIM] / acc[HEAD_DIM:HEAD_DIM + 1])


def fox_first_tiles(q, k, f_cum, n_kv):
    assert Q_TILE == K_TILE
    t = q.shape[0]
    nt = t // K_TILE
    qn = jnp.sum(jnp.square(q.astype(F32)).reshape(nt, Q_TILE, n_kv, REP, HEAD_DIM), axis=-1)
    qmax = jnp.sqrt(jnp.max(qn, axis=(1, 3)))
    kn = jnp.sum(jnp.square(k.astype(F32)).reshape(nt, K_TILE, n_kv, HEAD_DIM), axis=-1)
    kmax = jnp.sqrt(jnp.max(kn, axis=1))
    f_first = f_cum[0::K_TILE].reshape(nt, n_kv, REP)
    f_last = f_cum[K_TILE - 1::K_TILE].reshape(nt, n_kv, REP)
    gap = jnp.max(f_first[:, None] - f_last[None, :], axis=-1)
    bound = 1.01 * ATTN_SCALE * qmax[:, None] * (kmax[None, :] + kmax[:, None]) + gap
    earlier = jnp.arange(nt)[None, :, None] < jnp.arange(nt)[:, None, None]
    skippable = (bound < -SKIP_MARGIN) & earlier
    tile_idx = jnp.arange(nt, dtype=jnp.int32)[None, :, None]
    first = jnp.min(jnp.where(skippable, nt, tile_idx), axis=1)
    return first.T.reshape(-1).astype(jnp.int32)


def fox_prompt(first_tiles, proj_bf16, q_blk0, aug, k_aug, v_t, fq_lanes):
    n_kv, _, t = v_t.shape
    cols = REP * Q_TILE
    grid_spec = pltpu.PrefetchScalarGridSpec(
        num_scalar_prefetch=1,
        grid=(n_kv, t // Q_TILE),
        in_specs=[
            pl.BlockSpec((Q_TILE, REP * HEAD_DIM), lambda g, i, first: (i, q_blk0 + g)),
            pl.BlockSpec((HEAD_DIM, cols), lambda g, i, first: (0, 0)),
            pl.BlockSpec((None, t, 2 * HEAD_DIM), lambda g, i, first: (g, 0, 0)),
            pl.BlockSpec((None, V_ROWS, t), lambda g, i, first: (g, 0, 0)),
            pl.BlockSpec((None, None, 1, cols), lambda g, i, first: (g, i, 0, 0)),
        ],
        out_specs=pl.BlockSpec((Q_TILE, REP * HEAD_DIM), lambda g, i, first: (i, g)),
        scratch_shapes=[pltpu.VMEM((V_ROWS, cols), F32)],
    )
    return pl.pallas_call(
        _fox_prompt_kernel,
        grid_spec=grid_spec,
        out_shape=jax.ShapeDtypeStruct((t, n_kv * REP * HEAD_DIM), BF16),
        compiler_params=_params(("arbitrary", "arbitrary"), 40),
        name="fox_prompt",
    )(first_tiles, proj_bf16, aug, k_aug, v_t, fq_lanes)


def _nsa_prompt_kernel(q_ref, ka_ref, vst_ref, kw_ref, vwt_ref, ck_ref, cvt_ref, gate_ref, wmask_ref, o_ref,
                       acc_ref, *, n_sel):
    q0 = pl.program_id(1) * Q_TILE
    cols = REP * Q_TILE
    nbp = ck_ref.shape[0]
    qt = _queries_t(q_ref[...])

    x_c = _dot(ck_ref[...].astype(BF16), qt) * ATTN_SCALE
    blk = _iota((nbp, cols), 0)
    qpos = q0 + (_iota((nbp, cols), 1) & (Q_TILE - 1))
    x_c = jnp.where((blk + 1) * BLOCK - 1 <= qpos, x_c, NEG_INF)
    p_c = _softmax_cols(x_c) * (qpos >= BLOCK - 1).astype(F32)
    o_cmp = _dot(cvt_ref[...], p_c.astype(BF16))

    imp_t = p_c[:, 0:Q_TILE]
    for r in range(1, REP):
        imp_t = imp_t + p_c[:, r * Q_TILE:(r + 1) * Q_TILE]
    blk_t = _iota((nbp, Q_TILE), 0)
    qpos_t = q0 + _iota((nbp, Q_TILE), 1)
    sel_t = _top_n_mask_t(_selection_scores_t(imp_t, blk_t, qpos_t), blk_t, n_sel, nbp)

    sel_bias = jnp.where(sel_t > 0.5, 0.0, NEG_INF).astype(BF16)
    qa = jnp.concatenate([qt, jnp.concatenate([sel_bias] * REP, axis=1)], axis=0)
    acc_ref[...] = jnp.zeros_like(acc_ref)
    n_full = q0 // K_TILE

    def scores(j):
        k0 = pl.multiple_of(j * K_TILE, K_TILE)
        return _dot(ka_ref[pl.ds(k0, K_TILE), :], qa)

    def update(j, x, m):
        k0 = pl.multiple_of(j * K_TILE, K_TILE)
        return _online_tile_t(x, vst_ref[:, pl.ds(k0, K_TILE)], m, acc_ref, None)

    m = lax.fori_loop(0, n_full, lambda j, m: update(j, scores(j), m), jnp.full((1, cols), NEG_INF, F32))
    kpos = n_full * K_TILE + _iota((K_TILE, cols), 0)
    causal = kpos <= q0 + (_iota((K_TILE, cols), 1) & (Q_TILE - 1))
    update(n_full, jnp.where(causal, scores(n_full), NEG_INF), m)
    acc = acc_ref[...]
    o_slc = acc[0:HEAD_DIM] / acc[HEAD_DIM:HEAD_DIM + 1]

    wlen = WINDOW + Q_TILE
    w0 = pl.multiple_of(jnp.maximum(q0 - WINDOW, 0), Q_TILE)
    x_w = _dot(kw_ref[pl.ds(w0, wlen), :], qt) * SCALE2
    x_w = x_w + jnp.concatenate([wmask_ref[...]] * REP, axis=1)
    p_w = jnp.exp2(x_w - jnp.max(x_w, axis=0, keepdims=True))
    a_w = _dot(vwt_ref[:, pl.ds(w0, wlen)], p_w.astype(BF16))
    o_win = a_w[0:HEAD_DIM] / a_w[HEAD_DIM:HEAD_DIM + 1]

    gates = gate_ref[...]
    _store_heads(o_ref, gates[0:1] * o_cmp + gates[1:2] * o_slc + gates[2:3] * o_win)


def _window_masks(t):
    n_var = WINDOW // Q_TILE + 1
    k = np.arange(WINDOW + Q_TILE)[None, :, None]
    q = np.arange(Q_TILE)[None, None, :]
    q0 = (np.arange(n_var) * Q_TILE)[:, None, None]
    dist = q0 + q - (np.maximum(q0 - WINDOW, 0) + k)
    return jnp.asarray(np.where((dist >= 0) & (dist < WINDOW), 0.0, NEG_INF), F32)


def nsa_prompt(proj_bf16, q_blk0, k_aug, kw_blk0, vs_t, vw_t, cmp_k, cmp_vt, gate_lanes, n_sel):
    assert WINDOW % Q_TILE == 0
    wmask = _window_masks(vs_t.shape[2])
    last_var = wmask.shape[0] - 1
    n_groups, _, t = vs_t.shape
    nbp = cmp_k.shape[1]
    cols = REP * Q_TILE
    return pl.pallas_call(
        functools.partial(_nsa_prompt_kernel, n_sel=n_sel),
        grid=(n_groups, t // Q_TILE),
        in_specs=[
            pl.BlockSpec((Q_TILE, REP * HEAD_DIM), lambda g, i: (i, q_blk0 + g)),
            pl.BlockSpec((None, t, HEAD_DIM + nbp), lambda g, i: (g, 0, 0)),
            pl.BlockSpec((None, V_ROWS, t), lambda g, i: (g, 0, 0)),
            pl.BlockSpec((t, HEAD_DIM), lambda g, i: (0, kw_blk0 + g)),
            pl.BlockSpec((None, V_ROWS, t), lambda g, i: (g, 0, 0)),
            pl.BlockSpec((None, nbp, HEAD_DIM), lambda g, i: (g, 0, 0)),
            pl.BlockSpec((None, HEAD_DIM, nbp), lambda g, i: (g, 0, 0)),
            pl.BlockSpec((None, None, 8, cols), lambda g, i: (g, i, 0, 0)),
            pl.BlockSpec((None, WINDOW + Q_TILE, Q_TILE), lambda g, i: (jnp.minimum(i, last_var), 0, 0)),
        ],
        out_specs=pl.BlockSpec((Q_TILE, REP * HEAD_DIM), lambda g, i: (i, g)),
        out_shape=jax.ShapeDtypeStruct((t, n_groups * REP * HEAD_DIM), BF16),
        scratch_shapes=[pltpu.VMEM((V_ROWS, cols), F32)],
        compiler_params=_params(("arbitrary", "arbitrary"), 56),
        name="nsa_prompt",
    )(proj_bf16, k_aug, vs_t, proj_bf16, vw_t, cmp_k, cmp_vt, gate_lanes, wmask)


def _pad_rows(x, n):
    return jnp.concatenate([x, jnp.zeros((n - x.shape[0], x.shape[1]), x.dtype)], axis=0)


def _copy_rows(block_refs, rows_ref):
    views = []
    for p, r in enumerate(block_refs):
        rows_ref[p] = r[...].reshape(rows_ref.shape[1:])
        views.append(rows_ref.at[p])
    return views


def _fox_sample_kernel(pt_ref, *refs, n_pages, n_kv):
    lf_pages = refs[n_pages:2 * n_pages]
    q_ref, new_ref, lf_new_ref, tri_ref, o_ref, kv_ref = refs[2 * n_pages:]
    pages = refs[:n_pages]
    stride = 2 * n_kv
    page = pages[0].shape[0] // stride
    t_new = new_ref.shape[0]
    rows = REP * t_new
    past = n_pages * page
    n_keys = kv_ref.shape[1]
    for c in range(stride):
        for p in range(n_pages):
            kv_ref[c, p * page:(p + 1) * page, :] = pages[p][pl.ds(c, page, stride=stride), :].astype(BF16)
        kv_ref[c, past:n_keys, :] = _pad_rows(new_ref[:, c * HEAD_DIM:(c + 1) * HEAD_DIM], n_keys - past).astype(BF16)
    visible = _iota((t_new, n_keys), 1) <= past + _iota((t_new, n_keys), 0)
    f_chunks = _cumsum_lanes(list(lf_pages) + [lf_new_ref], tri_ref[...])
    f_keys = jnp.concatenate(f_chunks, axis=1)
    f_queries = _pad_rows(f_chunks[-1], LANE).T[0:t_new, :]
    for g in range(n_kv):
        s = _dot_nt(q_ref[g], kv_ref[g]) * ATTN_SCALE
        s = s.reshape(REP, t_new, n_keys) - f_keys[g * REP:(g + 1) * REP, :][:, None, :]
        s = jnp.where(visible[None], s, NEG_INF).reshape(rows, n_keys)
        fq = jnp.concatenate([f_queries[:, g * REP + r:g * REP + r + 1] for r in range(REP)], axis=0)
        prob = _softmax_rows(s + fq).astype(BF16)
        o_ref[g] = _dot(prob, kv_ref[n_kv + g]).astype(o_ref.dtype)


def fox_sample(pt_flat, fox_cache, layer, logf_pool_t, logf_new_t, q_s, new_rows, new_col_block, n_seq, n_pages):
    depth, n_pool, page, _, n_kv, _ = fox_cache.shape
    fox_cache = fox_cache.reshape(depth, n_pool, page * 2 * n_kv, HEAD_DIM)
    t_new = new_rows.shape[0] // n_seq
    rows = REP * t_new
    n_heads = logf_pool_t.shape[1]
    assert page == LANE
    tri = jnp.asarray(np.concatenate([np.triu(np.ones((LANE, LANE), np.float32)),
                                      np.ones((LANE, LANE), np.float32)], axis=1), BF16)
    spec = lambda p: pl.BlockSpec((None, None, page * 2 * n_kv, HEAD_DIM),
                                  lambda b, pt: (layer, pt[b * n_pages + p], 0, 0))
    lf_spec = lambda p: pl.BlockSpec((None, n_heads, page), lambda b, pt: (pt[b * n_pages + p], 0, 0))
    grid_spec = pltpu.PrefetchScalarGridSpec(
        num_scalar_prefetch=1,
        grid=(n_seq,),
        in_specs=[spec(p) for p in range(n_pages)] + [lf_spec(p) for p in range(n_pages)] + [
            pl.BlockSpec((None, n_kv, rows, HEAD_DIM), lambda b, pt: (b, 0, 0, 0)),
            pl.BlockSpec((t_new, 2 * n_kv * HEAD_DIM), lambda b, pt: (b, new_col_block)),
            pl.BlockSpec((None, n_heads, page), lambda b, pt: (b, 0, 0)),
            pl.BlockSpec((LANE, 2 * LANE), lambda b, pt: (0, 0)),
        ],
        out_specs=pl.BlockSpec((None, n_kv, rows, HEAD_DIM), lambda b, pt: (b, 0, 0, 0)),
        scratch_shapes=[pltpu.VMEM((2 * n_kv, (n_pages + 1) * page, HEAD_DIM), BF16)],
    )
    return pl.pallas_call(
        functools.partial(_fox_sample_kernel, n_pages=n_pages, n_kv=n_kv),
        grid_spec=grid_spec,
        out_shape=jax.ShapeDtypeStruct((n_seq, n_kv, rows, HEAD_DIM), BF16),
        compiler_params=_params(("arbitrary",), 52),
        name="fox_sample",
    )(pt_flat, *([fox_cache] * n_pages), *([logf_pool_t] * n_pages), q_s, new_rows, logf_new_t, tri)


def _nsa_sample_kernel(pt_ref, *refs, n_pages, n_groups, n_sel, past_len):
    (q_ref, cmp_ref, new_slc_ref, new_win_ref, win_ref, gate_ref, o_ref, win_out_ref,
     krows_ref, vrows_ref, wrows_ref) = refs[2 * n_pages:]
    page = refs[0].shape[0]
    kpages = _copy_rows(refs[:n_pages], krows_ref)
    vpages = _copy_rows(refs[n_pages:2 * n_pages], vrows_ref)
    t_new = new_slc_ref.shape[0]
    rows = REP * t_new
    gw = n_groups * HEAD_DIM
    nbp = cmp_ref.shape[2]
    n_buf = win_ref.shape[0]
    win_rows = _copy_rows([win_ref], wrows_ref)[0]
    blocks_per_page = page // BLOCK

    qpos = past_len + _iota((t_new, nbp), 0)
    c_mask = (_iota((t_new, nbp), 1) + 1) * BLOCK - 1 <= qpos
    any_vis = (qpos >= BLOCK - 1).astype(F32)
    o_cmp, imps = [], []
    for g in range(n_groups):
        s_c = _dot_nt(q_ref[g], cmp_ref[0, g].astype(BF16)) * ATTN_SCALE
        s_c = jnp.where(c_mask[None], s_c.reshape(REP, t_new, nbp), NEG_INF)
        p_c = _softmax_rows(s_c) * any_vis[None]
        o_cmp.append(_dot(p_c.reshape(rows, nbp).astype(BF16), cmp_ref[1, g].astype(BF16)))
        imp = p_c[0]
        for r in range(1, REP):
            imp = imp + p_c[r]
        imps.append(imp)

    imp_all = _pad_rows(jnp.concatenate(imps, axis=0), LANE)
    blk_t = _iota((nbp, LANE), 0)
    qpos_t = past_len + _iota((nbp, LANE), 1) % t_new
    sel_t = _top_n_mask_t(_selection_scores_t(imp_all.T, blk_t, qpos_t), blk_t, n_sel, nbp)
    sel_all = sel_t.T

    lane_p = _iota((t_new, page), 1)
    row_p = _iota((t_new, page), 0)
    lane_blk = lane_p // BLOCK
    widx = _iota((t_new, n_buf + page), 1)
    wdist = n_buf + _iota((t_new, n_buf + page), 0) - widx
    w_mask = (wdist >= 0) & (wdist < WINDOW) & (widx < n_buf + t_new)
    gates = gate_ref[...]

    for g in range(n_groups):
        q4 = q_ref[g]
        gs = slice(g * HEAD_DIM, (g + 1) * HEAD_DIM)
        vs = slice(gw + g * HEAD_DIM, gw + (g + 1) * HEAD_DIM)
        sel_g = sel_all[g * t_new:(g + 1) * t_new, :]

        chunks = []
        for p in range(n_pages + 1):
            if p < n_pages:
                k_p = kpages[p][pl.ds(g, page, stride=n_groups), :].astype(BF16)
            else:
                k_p = _pad_rows(new_slc_ref[:, gs], page).astype(BF16)
            s = _dot_nt(q4, k_p) * ATTN_SCALE
            picked = jnp.zeros((t_new, page), F32)
            for c in range(blocks_per_page):
                b_idx = p * blocks_per_page + c
                picked = jnp.where(lane_blk == c, sel_g[:, b_idx:b_idx + 1], picked)
            ok = picked > 0.5
            if p == n_pages:
                ok = ok & (lane_p <= row_p)
            chunks.append(jnp.where(ok[None], s.reshape(REP, t_new, page), NEG_INF).reshape(rows, page))
        prob = _softmax_rows(jnp.concatenate(chunks, axis=1)).astype(BF16)
        o_slc = jnp.zeros((rows, HEAD_DIM), F32)
        for p in range(n_pages + 1):
            if p < n_pages:
                v_p = vpages[p][pl.ds(g, page, stride=n_groups), :].astype(BF16)
            else:
                v_p = _pad_rows(new_slc_ref[:, vs], page).astype(BF16)
            o_slc = o_slc + _dot(prob[:, p * page:(p + 1) * page], v_p)

        kw = jnp.concatenate([win_rows[pl.ds(g, n_buf, stride=2 * n_groups), :],
                              _pad_rows(new_win_ref[:, gs], page)], axis=0).astype(BF16)
        vw = jnp.concatenate([win_rows[pl.ds(n_groups + g, n_buf, stride=2 * n_groups), :],
                              _pad_rows(new_win_ref[:, vs], page)], axis=0).astype(BF16)
        s_w = _dot_nt(q4, kw) * ATTN_SCALE
        s_w = jnp.where(w_mask[None], s_w.reshape(REP, t_new, n_buf + page), NEG_INF).reshape(rows, n_buf + page)
        o_win = _dot(_softmax_rows(s_w).astype(BF16), vw)

        gt = gates[g]
        o_ref[g] = (gt[:, 0:1] * o_cmp[g] + gt[:, 1:2] * o_slc + gt[:, 2:3] * o_win).astype(o_ref.dtype)

    win_out_ref[0:n_buf - t_new] = win_ref[t_new:n_buf]
    for c in range(2 * n_groups):
        win_out_ref[n_buf - t_new:n_buf, c // n_groups, c % n_groups, :] = new_win_ref[:, c * HEAD_DIM:(c + 1) * HEAD_DIM]


def nsa_sample(pt_flat, nsa_cache, win_state, layer, q_s, cmp_s, new_rows, slc_col_block, win_col_block, gates_s,
               n_seq, n_pages, n_sel, past_len):
    page, _, n_groups, _ = nsa_cache.shape[2:]
    gw = n_groups * HEAD_DIM
    t_new = new_rows.shape[0] // n_seq
    rows = REP * t_new
    nbp = cmp_s.shape[3]
    n_buf = win_state.shape[2]
    spec = lambda p, slot: pl.BlockSpec((None, None, page, None, n_groups, HEAD_DIM),
                                        lambda b, pt: (layer, pt[b * n_pages + p], 0, slot, 0, 0))
    grid_spec = pltpu.PrefetchScalarGridSpec(
        num_scalar_prefetch=1,
        grid=(n_seq,),
        in_specs=[spec(p, 2) for p in range(n_pages)] + [spec(p, 3) for p in range(n_pages)] + [
            pl.BlockSpec((None, n_groups, rows, HEAD_DIM), lambda b, pt: (b, 0, 0, 0)),
            pl.BlockSpec((None, 2, n_groups, nbp, HEAD_DIM), lambda b, pt: (b, 0, 0, 0, 0)),
            pl.BlockSpec((t_new, 2 * gw), lambda b, pt: (b, slc_col_block)),
            pl.BlockSpec((t_new, 2 * gw), lambda b, pt: (b, win_col_block)),
            pl.BlockSpec((None, None, n_buf, 2, n_groups, HEAD_DIM), lambda b, pt: (layer, b, 0, 0, 0, 0)),
            pl.BlockSpec((None, n_groups, rows, 16), lambda b, pt: (b, 0, 0, 0)),
        ],
        out_specs=[pl.BlockSpec((None, n_groups, rows, HEAD_DIM), lambda b, pt: (b, 0, 0, 0)),
                   pl.BlockSpec((None, n_buf, 2, n_groups, HEAD_DIM), lambda b, pt: (b, 0, 0, 0, 0))],
        scratch_shapes=[pltpu.VMEM((n_pages, page * n_groups, HEAD_DIM), F32),
                        pltpu.VMEM((n_pages, page * n_groups, HEAD_DIM), F32),
                        pltpu.VMEM((1, n_buf * 2 * n_groups, HEAD_DIM), F32)],
    )
    assert n_buf > t_new
    return pl.pallas_call(
        functools.partial(_nsa_sample_kernel, n_pages=n_pages, n_groups=n_groups, n_sel=n_sel, past_len=past_len),
        grid_spec=grid_spec,
        out_shape=[jax.ShapeDtypeStruct((n_seq, n_groups, rows, HEAD_DIM), BF16),
                   jax.ShapeDtypeStruct(win_state.shape[1:], F32)],
        compiler_params=_params(("arbitrary",), 56),
        name="nsa_sample",
    )(pt_flat, *([nsa_cache] * (2 * n_pages)), q_s, cmp_s, new_rows, new_rows, win_state, gates_s)


def _rope_tables(pos):
    half = HEAD_DIM // 2
    inv_freq = ROPE_THETA ** (-jnp.arange(half, dtype=F32) / half)
    ang = pos.astype(F32)[:, None] * inv_freq[None, :]
    cos, sin = jnp.cos(ang), jnp.sin(ang)
    return jnp.concatenate([cos, cos], axis=-1), jnp.concatenate([-sin, sin], axis=-1)


def _heads_major(x, n_seq, t_new, n_groups):
    w = x.shape[1] // (n_groups * REP)
    return x.reshape(n_seq, t_new, n_groups, REP, w).transpose(0, 2, 3, 1, 4).reshape(n_seq, n_groups, REP * t_new, w)


def _tokens_major(x, n_seq, t_new, n_groups):
    w = x.shape[-1]
    return x.reshape(n_seq, n_groups, REP, t_new, w).transpose(0, 3, 1, 2, 4).reshape(n_seq * t_new, n_groups * REP * w)


def _values_t(v, n_groups):
    t = v.shape[0]
    vt = v.T.reshape(n_groups, HEAD_DIM, t)
    return jnp.concatenate([vt, jnp.ones((n_groups, V_ROWS - HEAD_DIM, t), v.dtype)], axis=1)


def _head_lanes(x, n_groups, w):
    t = x.shape[0]
    nq = t // Q_TILE
    x = x.reshape(nq, Q_TILE, n_groups, REP, w).transpose(2, 0, 4, 3, 1)
    return x.reshape(n_groups, nq, w, REP * Q_TILE)


def kernel(x_prompt, x_sample, cache_fox_kv, cache_fox_logf, cache_nsa_kv, state_nsa_win, page_table, p_prompt, p_sample, g_mix, w_in, b_fgate, cmp_pe_k, cmp_w1_k, cmp_w2_k, cmp_pe_v, cmp_w1_v, cmp_w2_v, w_out, g_ffn, w_gate, w_up, w_down, g_ple, w_ple_gate, w_ple_proj, g_final):
    depth = w_in.shape[0]
    _, seq, d_model = x_prompt.shape
    n_seq, t_new, _ = x_sample.shape
    page = cache_fox_kv.shape[2]
    n_pages = page_table.shape[1]
    past_len = n_pages * page
    fox_heads = b_fgate.shape[1]
    n_kv = cache_fox_kv.shape[4]
    n_groups = cache_nsa_kv.shape[4]
    nsa_heads = n_groups * REP
    assert fox_heads == n_kv * REP and w_out.shape[1] == (fox_heads + nsa_heads) * HEAD_DIM
    fq_w, nq_w = fox_heads * HEAD_DIM, nsa_heads * HEAD_DIM
    fkv_w, nkv_w = n_kv * HEAD_DIM, n_groups * HEAD_DIM
    n_gate = 3 * nsa_heads
    assert fox_heads + n_gate <= LANE and seq % K_TILE == 0 and seq >= WINDOW + Q_TILE and 3 * REP <= HEAD_DIM

    sizes = [fq_w, fkv_w, fkv_w, fox_heads, nq_w] + [nkv_w] * 6 + [n_gate]
    off = np.concatenate([[0], np.cumsum(sizes)]).astype(int)
    o_fq, o_fk, o_fv, o_fl, o_nq, o_kc, o_vc, o_ks, o_vs, o_kw, o_vw, o_g = [int(v) for v in off[:-1]]
    c_fq, c_nq = 0, fq_w
    c_fk = c_nq + nq_w
    c_fv = c_fk + fkv_w
    c_kc = c_fv + fkv_w
    c_vc, c_ks, c_vs, c_kw, c_vw = (c_kc + nkv_w * k for k in range(1, 6))
    n_main = c_vw + nkv_w
    tn = min(512, fkv_w, nkv_w)
    rope_ranges = ((c_nq, c_nq + nq_w), (c_kc, c_kc + nkv_w), (c_ks, c_ks + nkv_w), (c_kw, c_kw + nkv_w))

    nb_p = seq // BLOCK
    nbp_p = max(LANE, nb_p)
    nb_s = -(-(past_len + t_new) // BLOCK)
    nbp_s = LANE
    assert nb_s == 2 * n_pages + 1 and nb_s <= LANE and nbp_p % LANE == 0
    n_win_p = min(WINDOW, seq)
    nq_tiles = seq // Q_TILE

    cos_p, sin_p = _rope_tables(jnp.arange(seq))
    cos_s, sin_s = _rope_tables(jnp.tile(past_len + jnp.arange(t_new), n_seq))
    expand_t = jnp.asarray((np.arange(seq)[:, None] // BLOCK) == np.arange(nbp_p)[None, :], BF16)
    aug_np = np.zeros((HEAD_DIM, REP * Q_TILE), np.float32)
    for r in range(REP):
        aug_np[3 * r:3 * r + 3, r * Q_TILE:(r + 1) * Q_TILE] = 1.0
    aug = jnp.asarray(aug_np, BF16)
    pt_flat = page_table.reshape(-1).astype(jnp.int32)

    hp = x_prompt.reshape(seq, d_model)
    hs = x_sample.reshape(n_seq * t_new, d_model)
    outs_p, outs_s = [], []
    for i in range(depth):
        wi = w_in[i]
        w_main = jnp.concatenate(
            [wi[:, o_fq:o_fq + fq_w], wi[:, o_nq:o_nq + nq_w], wi[:, o_fk:o_fl], wi[:, o_kc:o_g]], axis=1).astype(BF16)
        w_small = jnp.concatenate(
            [wi[:, o_fl:o_fl + fox_heads], wi[:, o_g:o_g + n_gate],
             jnp.zeros((d_model, LANE - fox_heads - n_gate), F32)], axis=1).astype(BF16)
        b_small = jnp.concatenate([b_fgate[i], jnp.zeros((LANE - fox_heads,), F32)]).reshape(1, LANE)
        w_o = w_out[i].astype(BF16)
        w_oa, w_ob = w_o[:fq_w], w_o[fq_w:]
        w_g, w_u, w_d = w_gate[i].astype(BF16), w_up[i].astype(BF16), w_down[i].astype(BF16)
        w_pg, w_pp = w_ple_gate[i].astype(BF16), w_ple_proj[i].astype(BF16)
        pe = jnp.stack([cmp_pe_k[i], cmp_pe_v[i]])
        hidden = cmp_w1_k.shape[-1]
        w1 = jnp.stack([cmp_w1_k[i], cmp_w1_v[i]]).reshape(2, BLOCK, HEAD_DIM, hidden).astype(BF16)
        w1_pairs = w1.reshape(2, BLOCK // 2, 2 * HEAD_DIM, hidden)
        w2 = jnp.stack([cmp_w2_k[i], cmp_w2_v[i]]).astype(BF16)

        def tail(h, o_fox, o_nsa, p):
            h1 = outproj(o_fox, o_nsa, w_oa, w_ob, h)
            act = gateup(rmsnorm(h1, g_ffn[i], BF16), w_g, w_u)
            h2 = downproj(act, w_d, h1)
            return ple(rmsnorm(h2, g_ple[i], BF16), w_pg, p.astype(BF16), w_pp, h2)

        xn = rmsnorm(hp, g_mix[i], BF16)
        pf, pb = inproj(xn, w_main, cos_p, sin_p, tn, rope_ranges)
        small = smallproj(xn, w_small, b_small, fox_heads)
        logf = small[:, :fox_heads]
        f_cum, fp1, fp2, fp3 = cumsum_rows(logf)
        pieces = jnp.stack([fp1, fp2, fp3], axis=-1)
        pieces = pieces.reshape(seq, n_kv, REP * 3).transpose(1, 0, 2)
        k_fox = pb[:, c_fk:c_fk + fkv_w].reshape(seq, n_kv, HEAD_DIM).transpose(1, 0, 2)
        k_aug = jnp.concatenate([k_fox, pieces, jnp.zeros((n_kv, seq, HEAD_DIM - REP * 3), BF16)], axis=-1)
        first_tiles = fox_first_tiles(pb[:, c_fq:c_fq + fq_w], pb[:, c_fk:c_fk + fkv_w], f_cum, n_kv)
        o_fox = fox_prompt(first_tiles, pb, c_fq // (REP * HEAD_DIM), aug, k_aug,
                           _values_t(pb[:, c_fv:c_fv + fkv_w], n_kv), _head_lanes(f_cum, n_kv, 1))
        cmp_p = compress_prompt(pf, (c_kc // HEAD_DIM, c_vc // HEAD_DIM), pe, w1, w2, n_groups, nbp_p)
        cmp_vt = cmp_p[1].transpose(0, 2, 1).astype(BF16)
        gate_lanes = _head_lanes(small[:, fox_heads:fox_heads + n_gate], n_groups, 3)
        gate_lanes = jnp.pad(gate_lanes, ((0, 0), (0, 0), (0, 5), (0, 0)))
        k_slc = pb[:, c_ks:c_ks + nkv_w].reshape(seq, n_groups, HEAD_DIM).transpose(1, 0, 2)
        k_aug_nsa = jnp.concatenate([k_slc, jnp.broadcast_to(expand_t, (n_groups,) + expand_t.shape)], axis=-1)
        o_nsa = nsa_prompt(pb, c_nq // (REP * HEAD_DIM), k_aug_nsa, c_kw // HEAD_DIM,
                           _values_t(pb[:, c_vs:c_vs + nkv_w], n_groups), _values_t(pb[:, c_vw:c_vw + nkv_w], n_groups),
                           cmp_p[0], cmp_vt, gate_lanes, min(SEL_TOP_N, nb_p))
        hp = tail(hp, o_fox, o_nsa, p_prompt[i].reshape(seq, -1))
        outs_p.append((
            pf[:, c_fk:c_fk + 2 * fkv_w].reshape(1, seq, 2, n_kv, HEAD_DIM),
            logf.reshape(1, seq, fox_heads),
            pf[:, c_kc:c_kc + 4 * nkv_w].reshape(1, seq, 4, n_groups, HEAD_DIM),
            pf[seq - n_win_p:, c_kw:c_kw + 2 * nkv_w].reshape(1, n_win_p, 2, n_groups, HEAD_DIM),
        ))

        m_s = n_seq * t_new
        xs = rmsnorm(hs, g_mix[i], BF16)
        sf, sb = inproj(xs, w_main, cos_s, sin_s, tn, rope_ranges)
        small_s = smallproj(xs, w_small, b_small, fox_heads)
        logf_s = small_s[:, :fox_heads]
        logf_pool_t = cache_fox_logf[i].transpose(0, 2, 1)
        new_t = jnp.pad(logf_s.reshape(n_seq, t_new, fox_heads).transpose(0, 2, 1),
                        ((0, 0), (0, 0), (0, LANE - t_new)))
        assert c_fk % (2 * fkv_w) == 0 and c_kc % nkv_w == 0 and c_ks % (2 * nkv_w) == 0 and c_kw % (2 * nkv_w) == 0
        q_fox_s = _heads_major(sb[:, c_fq:c_fq + fq_w], n_seq, t_new, n_kv)
        o_fox_s = fox_sample(pt_flat, cache_fox_kv, i, logf_pool_t, new_t, q_fox_s, sf, c_fk // (2 * fkv_w),
                             n_seq, n_pages)
        cmp_raw = compress_sample(pt_flat, cache_nsa_kv, i, sf, c_kc // nkv_w, pe, w1_pairs, w2, n_seq, n_pages, nb_s)
        cmp_s = cmp_raw[:, :, :nb_s * n_groups].reshape(n_seq, 2, nb_s, n_groups, HEAD_DIM).transpose(0, 1, 3, 2, 4)
        cmp_s = jnp.pad(cmp_s, ((0, 0), (0, 0), (0, 0), (0, nbp_s - nb_s), (0, 0)))
        q_nsa_s = _heads_major(sb[:, c_nq:c_nq + nq_w], n_seq, t_new, n_groups)
        gates_s = _heads_major(small_s[:, fox_heads:fox_heads + n_gate], n_seq, t_new, n_groups)
        gates_s = jnp.pad(gates_s, ((0, 0), (0, 0), (0, 0), (0, 13)))
        o_nsa_s, new_win = nsa_sample(pt_flat, cache_nsa_kv, state_nsa_win, i, q_nsa_s, cmp_s, sf,
                                      c_ks // (2 * nkv_w), c_kw // (2 * nkv_w), gates_s,
                                      n_seq, n_pages, min(SEL_TOP_N, nb_s), past_len)
        hs = tail(hs, _tokens_major(o_fox_s, n_seq, t_new, n_kv), _tokens_major(o_nsa_s, n_seq, t_new, n_groups),
                  p_sample[i].reshape(m_s, -1))
        outs_s.append((
            sf[:, c_fk:c_fk + 2 * fkv_w].reshape(n_seq, t_new, 2, n_kv, HEAD_DIM),
            logf_s.reshape(n_seq, t_new, fox_heads),
            sf[:, c_kc:c_kc + 4 * nkv_w].reshape(n_seq, t_new, 4, n_groups, HEAD_DIM),
            new_win,
        ))

    y_prompt = rmsnorm(hp, g_final, F32).reshape(x_prompt.shape)
    y_sample = rmsnorm(hs, g_final, F32).reshape(x_sample.shape)
    stack = lambda outs, j: jnp.stack([r[j] for r in outs], axis=0)
    return (y_prompt, y_sample, stack(outs_p, 0), stack(outs_p, 1), stack(outs_p, 2), stack(outs_p, 3),
            stack(outs_s, 0), stack(outs_s, 1), stack(outs_s, 2), stack(outs_s, 3))
```
